```python
import jax
import jax.numpy as jnp
from jax import lax
import numpy as np

D_MODEL = 1024
BATCH = 8
SEQ = 4096
DEPTH = 4

GRID_W = 64
HEAD_DIM = 64
RMS_EPS = 1e-6
NEG_INF = -1e30
Q_BLOCK = 128

A_HEADS = 4
NA_ROWS_MAX = 8
NA_COLS = 16
B_HEADS = 4
DILATED_BRANCHES = ((128, 1), (512, 4), (2048, 16))
C_HEADS = 8
C_KV_HEADS = 2
ROPE_THETA = 10000.0

A_W = A_HEADS * HEAD_DIM
B_W = B_HEADS * HEAD_DIM
C_W = C_HEADS * HEAD_DIM
C_KV_W = C_KV_HEADS * HEAD_DIM
MIX_W = A_W + B_W + C_W
IN_SPLITS = (A_W, A_W, A_W, B_W, B_W, B_W, C_W, C_KV_W, C_KV_W)
IN_COLS = sum(IN_SPLITS)

N_GROUPS = 4
EXPERTS_PER_GROUP = 8
N_EXPERTS = N_GROUPS * EXPERTS_PER_GROUP
TOP_K_IN_GROUP = 2
D_EXPERT = 512
MOE_BLOCK = 256

kernel_name = "hybrid_na_dilated_gqa_hiermoe_encoder"


def rms_norm(x, g):
    xf = x.astype(jnp.float32)
    y = xf * lax.rsqrt(jnp.mean(xf * xf, axis=-1, keepdims=True) + RMS_EPS)
    return (y * g.astype(jnp.float32)).astype(x.dtype)


def alibi_slopes(n):
    return jnp.exp2(-8.0 * jnp.arange(1, n + 1, dtype=jnp.float32) / n)


def axial_rope(x):
    b, s, h, d = x.shape
    n_freq = d // 4
    t = jnp.arange(s)
    inv_freq = ROPE_THETA ** (-jnp.arange(n_freq, dtype=jnp.float32) / n_freq)
    rows = (t // GRID_W).astype(jnp.float32)
    cols = (t % GRID_W).astype(jnp.float32)
    ang = jnp.stack([rows[:, None] * inv_freq, cols[:, None] * inv_freq], axis=1)
    cos = jnp.cos(ang)[None, :, None]
    sin = jnp.sin(ang)[None, :, None]
    xr = x.astype(jnp.float32).reshape(b, s, h, 2, 2, n_freq)
    x1, x2 = xr[..., 0, :], xr[..., 1, :]
    out = jnp.stack([x1 * cos - x2 * sin, x2 * cos + x1 * sin], axis=-2)
    return out.reshape(b, s, h, d).astype(x.dtype)


def neighbourhood_attention(q, k, v, rpb):
    bsz, s, h, hd = q.shape
    rows = s // GRID_W
    kh = min(NA_ROWS_MAX, rows)
    scale = hd ** -0.5
    qg = q.reshape(bsz, rows, GRID_W, h, hd)
    kg = k.reshape(bsz, rows, GRID_W, h, hd)
    vg = v.reshape(bsz, rows, GRID_W, h, hd)
    cols = jnp.arange(GRID_W)
    col_start = jnp.clip(cols - NA_COLS // 2, 0, GRID_W - NA_COLS)
    col_idx = col_start[:, None] + jnp.arange(NA_COLS)[None, :]
    dc = col_idx - cols[:, None]

    def row_block(args):
        r, q_row = args
        r0 = jnp.clip(r - kh // 2, 0, rows - kh)
        k_rows = lax.dynamic_slice_in_dim(kg, r0, kh, axis=1)
        v_rows = lax.dynamic_slice_in_dim(vg, r0, kh, axis=1)
        k_nb = k_rows[:, :, col_idx]
        v_nb = v_rows[:, :, col_idx]
        sc = jnp.einsum('bqhd,bkqjhd->bhqkj', q_row, k_nb,
                        preferred_element_type=jnp.float32) * scale
        dr = r0 + jnp.arange(kh) - r
        bias = rpb[:, (dr + NA_ROWS_MAX - 1)[None, :, None], (dc + NA_COLS - 1)[:, None, :]]
        sc = sc + bias.astype(jnp.float32)[None]
        p = jax.nn.softmax(sc.reshape(bsz, h, GRID_W, kh * NA_COLS), axis=-1)
        p = p.reshape(bsz, h, GRID_W, kh, NA_COLS).astype(v.dtype)
        return jnp.einsum('bhqkj,bkqjhd->bqhd', p, v_nb)

    out = lax.map(row_block, (jnp.arange(rows), qg.transpose(1, 0, 2, 3, 4)))
    return out.transpose(1, 0, 2, 3, 4).reshape(bsz, s, h * hd)


def dilated_attention(q, k, v):
    bsz, s, h, hd = q.shape
    scale = hd ** -0.5
    slopes = alibi_slopes(h)
    n_blocks = s // Q_BLOCK
    qb = q.reshape(bsz, n_blocks, Q_BLOCK, h, hd).transpose(1, 0, 2, 3, 4)

    def block(args):
        i, q_blk = args
        pos = i * Q_BLOCK + jnp.arange(Q_BLOCK)
        maxes, dens, nums = [], [], []
        for window, dil in DILATED_BRANCHES:
            half = window // (2 * dil)
            offs = dil * jnp.arange(-half, half + 1)
            kp = pos[:, None] + offs[None, :]
            valid = (kp >= 0) & (kp < s)
            kpc = jnp.clip(kp, 0, s - 1)
            k_sel = jnp.take(k, kpc, axis=1)
            v_sel = jnp.take(v, kpc, axis=1)
            sc = jnp.einsum('bqhd,bqjhd->bhqj', q_blk, k_sel,
                            preferred_element_type=jnp.float32) * scale
            sc = sc - slopes[:, None, None] * jnp.abs(offs).astype(jnp.float32)[None, None, :]
            sc = jnp.where(valid[None, None], sc, NEG_INF)
            m = jnp.max(sc, axis=-1)
            p = jnp.exp(sc - m[..., None])
            maxes.append(m)
            dens.append(jnp.sum(p, axis=-1))
            nums.append(jnp.einsum('bhqj,bqjhd->bhqd', p, v_sel.astype(jnp.float32)))
        m_all = jnp.max(jnp.stack(maxes), axis=0)
        wts = [jnp.exp(m - m_all) for m in maxes]
        num = sum(w[..., None] * n for w, n in zip(wts, nums))
        den = sum(w * d for w, d in zip(wts, dens))
        o = num / den[..., None]
        return o.transpose(0, 2, 1, 3).astype(q.dtype)

    out = lax.map(block, (jnp.arange(n_blocks), qb))
    return out.transpose(1, 0, 2, 3, 4).reshape(bsz, s, h * hd)


def gqa_attention(q, k, v):
    bsz, s, hq, hd = q.shape
    hkv = k.shape[2]
    grp = hq // hkv
    scale = hd ** -0.5
    n_blocks = s // Q_BLOCK
    qb = q.reshape(bsz, n_blocks, Q_BLOCK, hkv, grp, hd).transpose(1, 0, 2, 3, 4, 5)

    def block(q_blk):
        sc = jnp.einsum('bqkgd,bskd->bkgqs', q_blk, k,
                        preferred_element_type=jnp.float32) * scale
        p = jax.nn.softmax(sc, axis=-1).astype(v.dtype)
        return jnp.einsum('bkgqs,bskd->bqkgd', p, v)

    out = lax.map(block, qb)
    return out.transpose(1, 0, 2, 3, 4, 5).reshape(bsz, s, hq * hd)


def mixer_sublayer(x, norm_g, w_in, qk_g, rpb, mix_g, w_out):
    bsz, s, _ = x.shape
    h = rms_norm(x, norm_g)
    proj = h @ w_in
    qa, ka, va, qb, kb, vb, qc, kc, vc = jnp.split(proj, list(np.cumsum(IN_SPLITS)[:-1]), axis=-1)

    def heads(t):
        return t.reshape(bsz, s, t.shape[-1] // HEAD_DIM, HEAD_DIM)

    qa = rms_norm(heads(qa), qk_g[0, 0])
    ka = rms_norm(heads(ka), qk_g[0, 1])
    qb = rms_norm(heads(qb), qk_g[1, 0])
    kb = rms_norm(heads(kb), qk_g[1, 1])
    qc = axial_rope(rms_norm(heads(qc), qk_g[2, 0]))
    kc = axial_rope(rms_norm(heads(kc), qk_g[2, 1]))

    oa = neighbourhood_attention(qa, ka, heads(va), rpb)
    ob = dilated_attention(qb, kb, heads(vb))
    oc = gqa_attention(qc, kc, heads(vc))

    merged = jnp.concatenate([
        rms_norm(oa, mix_g[:A_W]),
        rms_norm(ob, mix_g[A_W:A_W + B_W]),
        rms_norm(oc, mix_g[A_W + B_W:]),
    ], axis=-1)
    return merged @ w_out


def hier_moe(h, w_rg, b_rg, w_re, b_re, w_gate, w_up, w_down):
    bsz, s, d = h.shape
    n = bsz * s
    xt = h.reshape(n, d)
    g_logits = jnp.dot(xt, w_rg, preferred_element_type=jnp.float32) + b_rg.astype(jnp.float32)
    g_prob = jax.nn.softmax(g_logits, axis=-1)
    grp = jnp.argmax(g_logits, axis=-1)
    p_grp = jnp.take_along_axis(g_prob, grp[:, None], axis=1)[:, 0]
    e_logits = (jnp.dot(xt, w_re, preferred_element_type=jnp.float32)
                + b_re.astype(jnp.float32)).reshape(n, N_GROUPS, EXPERTS_PER_GROUP)
    e_in = jnp.take_along_axis(e_logits, grp[:, None, None], axis=1)[:, 0]
    top_v, top_i = lax.top_k(e_in, TOP_K_IN_GROUP)
    gate = p_grp[:, None] * jax.nn.softmax(top_v, axis=-1)

    eid = (grp[:, None] * EXPERTS_PER_GROUP + top_i).reshape(-1).astype(jnp.int32)
    tok = jnp.repeat(jnp.arange(n, dtype=jnp.int32), TOP_K_IN_GROUP)
    wgt = gate.reshape(-1).astype(h.dtype)
    n_pairs = n * TOP_K_IN_GROUP
    cap = n_pairs + N_EXPERTS * MOE_BLOCK
    n_blk = cap // MOE_BLOCK

    order = jnp.argsort(eid)
    e_sorted = eid[order]
    counts = jnp.bincount(eid, length=N_EXPERTS)
    starts = jnp.cumsum(counts) - counts
    padded = (counts + MOE_BLOCK - 1) // MOE_BLOCK * MOE_BLOCK
    p_ends = jnp.cumsum(padded)
    p_starts = p_ends - padded
    dest = p_starts[e_sorted] + jnp.arange(n_pairs, dtype=jnp.int32) - starts[e_sorted]
    slot_tok = jnp.zeros((cap,), jnp.int32).at[dest].set(tok[order])
    slot_w = jnp.zeros((cap,), h.dtype).at[dest].set(wgt[order])
    blk_e = jnp.minimum(
        jnp.searchsorted(p_ends, jnp.arange(n_blk, dtype=jnp.int32) * MOE_BLOCK, side='right'),
        N_EXPERTS - 1)

    def expert_block(args):
        e, idx = args
        xb = xt[idx]
        return (jax.nn.silu(xb @ w_gate[e]) * (xb @ w_up[e])) @ w_down[e]

    y = lax.map(expert_block, (blk_e, slot_tok.reshape(n_blk, MOE_BLOCK))).reshape(cap, d)
    out = jnp.zeros_like(xt).at[slot_tok].add(y * slot_w[:, None])
    return out.reshape(bsz, s, d)


def setup_inputs(seed: int = 0) -> dict:
    key = jax.random.key(seed)
    ks = jax.random.split(key, 16)
    f32 = jnp.float32
    nrm = jax.random.normal
    return {
        "x": nrm(ks[0], (BATCH, SEQ, D_MODEL), f32),
        "norm1_g": 1.0 + 0.05 * nrm(ks[1], (DEPTH, D_MODEL), f32),
        "w_in": nrm(ks[2], (DEPTH, D_MODEL, IN_COLS), f32) * D_MODEL ** -0.5,
        "qk_norm_g": 1.0 + 0.05 * nrm(ks[3], (DEPTH, 3, 2, HEAD_DIM), f32),
        "na_rpb": 0.1 * nrm(ks[4], (DEPTH, A_HEADS, 2 * NA_ROWS_MAX - 1, 2 * NA_COLS - 1), f32),
        "mix_norm_g": 1.0 + 0.05 * nrm(ks[5], (DEPTH, MIX_W), f32),
        "w_out": nrm(ks[6], (DEPTH, MIX_W, D_MODEL), f32) * MIX_W ** -0.5,
        "norm2_g": 1.0 + 0.05 * nrm(ks[7], (DEPTH, D_MODEL), f32),
        "w_router_group": nrm(ks[8], (DEPTH, D_MODEL, N_GROUPS), f32) * D_MODEL ** -0.5,
        "b_router_group": 0.01 * nrm(ks[9], (DEPTH, N_GROUPS), f32),
        "w_router_expert": nrm(ks[10], (DEPTH, D_MODEL, N_EXPERTS), f32) * D_MODEL ** -0.5,
        "b_router_expert": 0.01 * nrm(ks[11], (DEPTH, N_EXPERTS), f32),
        "w_gate": nrm(ks[12], (DEPTH, N_EXPERTS, D_MODEL, D_EXPERT), f32) * D_MODEL ** -0.5,
        "w_up": nrm(ks[13], (DEPTH, N_EXPERTS, D_MODEL, D_EXPERT), f32) * D_MODEL ** -0.5,
        "w_down": nrm(ks[14], (DEPTH, N_EXPERTS, D_EXPERT, D_MODEL), f32) * D_EXPERT ** -0.5,
    }


def reference(x, norm1_g, w_in, qk_norm_g, na_rpb, mix_norm_g, w_out, norm2_g,
              w_router_group, b_router_group, w_router_expert, b_router_expert,
              w_gate, w_up, w_down):
    for l in range(DEPTH):
        x = x + mixer_sublayer(x, norm1_g[l], w_in[l], qk_norm_g[l], na_rpb[l],
                               mix_norm_g[l], w_out[l])
        x = x + hier_moe(rms_norm(x, norm2_g[l]), w_router_group[l], b_router_group[l],
                         w_router_expert[l], b_router_expert[l],
                         w_gate[l], w_up[l], w_down[l])
    return x
```

```python
import functools
import math

import numpy as np
import jax
import jax.numpy as jnp
from jax import lax
from jax.experimental import pallas as pl
from jax.experimental.pallas import tpu as pltpu

F32 = jnp.float32
BF16 = jnp.bfloat16

D_MODEL = 1024
DEPTH = 4
GRID_W = 64
HEAD_DIM = 64
RMS_EPS = 1e-6
NEG = -1e30
LOG2E = math.log2(math.e)
SCALE = HEAD_DIM ** -0.5

NA_ROWS = 8
NA_COLS = 16
DILATIONS = (1, 4, 16)
DIL_HALF = 64
ROPE_THETA = 10000.0

N_GROUPS = 4
EXPERTS_PER_GROUP = 8
N_EXPERTS = 32
D_EXPERT = 512
MOE_BLOCK = 256

LANES = 128
CHUNK = 256
CH_QA, CH_KA, CH_VA, CH_QB, CH_KB, CH_VB, CH_QC, CH_KC, CH_VC = 0, 1, 2, 3, 4, 5, 6, 8, 9
N_CHUNKS = 10
PROJ_W = N_CHUNKS * CHUNK
NORM_CHUNKS = (CH_QA, CH_KA, CH_QB, CH_KB, CH_QC, CH_QC + 1, CH_KC)
ROPE_CHUNKS = (CH_QC, CH_QC + 1, CH_KC)

VMEM_LIMIT = 56 * 1024 * 1024

NT_DIMS = (((1,), (1,)), ((), ()))


def _cparams(sem):
    return pltpu.CompilerParams(dimension_semantics=sem, vmem_limit_bytes=VMEM_LIMIT)


def _inproj_kernel(x_ref, g1_ref, w_ref, gain_ref, bd_ref, cos_ref, sa_ref, sb_ref, o_ref):
    x = x_ref[...]
    ms = jnp.mean(x * x, axis=-1, keepdims=True)
    h = (x * lax.rsqrt(ms + RMS_EPS) * g1_ref[...]).astype(BF16)
    for c in range(N_CHUNKS):
        cols = slice(c * CHUNK, (c + 1) * CHUNK)
        p = jnp.dot(h, w_ref[:, cols], preferred_element_type=F32)
        if c in NORM_CHUNKS:
            ss = jnp.dot((p * p).astype(BF16), bd_ref[...], preferred_element_type=F32)
            p = p * lax.rsqrt(ss * (1.0 / HEAD_DIM) + RMS_EPS) * gain_ref[:, cols]
        if c in ROPE_CHUNKS:
            p = (p * cos_ref[...] + pltpu.roll(p, CHUNK - 16, 1) * sa_ref[...]
                 + pltpu.roll(p, 16, 1) * sb_ref[...])
        o_ref[:, cols] = p.astype(BF16)


def _inproj(x2, g1, w, gain, bd, cos_t, sa_t, sb_t, seq, tm=512):
    n = x2.shape[0]
    nsb = seq // tm
    return pl.pallas_call(
        _inproj_kernel,
        grid=(n // tm,),
        in_specs=[
            pl.BlockSpec((tm, D_MODEL), lambda i: (i, 0)),
            pl.BlockSpec((1, D_MODEL), lambda i: (0, 0)),
            pl.BlockSpec((D_MODEL, PROJ_W), lambda i: (0, 0)),
            pl.BlockSpec((1, PROJ_W), lambda i: (0, 0)),
            pl.BlockSpec((CHUNK, CHUNK), lambda i: (0, 0)),
            pl.BlockSpec((tm, CHUNK), lambda i: (i % nsb, 0)),
            pl.BlockSpec((tm, CHUNK), lambda i: (i % nsb, 0)),
            pl.BlockSpec((tm, CHUNK), lambda i: (i % nsb, 0)),
        ],
        out_specs=pl.BlockSpec((tm, PROJ_W), lambda i: (i, 0)),
        out_shape=jax.ShapeDtypeStruct((n, PROJ_W), BF16),
        compiler_params=_cparams(("arbitrary",)),
        name="inproj",
    )(x2, g1, w, gain, bd, cos_t, sa_t, sb_t)


def _stack_head_pair(qp, lo):
    zero = jnp.zeros_like(qp)
    return jnp.concatenate([jnp.where(lo, qp, zero), jnp.where(lo, zero, qp)], axis=0)


def _lo_mask(rows):
    return lax.broadcasted_iota(jnp.int32, (rows, LANES), 1) < HEAD_DIM


def _na_kernel(q_ref, k_ref, v_ref, bias_ref, o_ref, *, rows_per_step, n_rows):
    i = pl.program_id(1)
    lo = _lo_mask(GRID_W)
    win = NA_ROWS * GRID_W
    for rl in range(rows_per_step):
        r = i * rows_per_step + rl
        r0 = jnp.clip(r - NA_ROWS // 2, 0, n_rows - NA_ROWS)
        rr = r - r0
        start = pl.multiple_of(r0 * GRID_W, GRID_W)
        qrows = slice(rl * GRID_W, (rl + 1) * GRID_W)
        for p in range(2):
            cols = slice(p * LANES, (p + 1) * LANES)
            qs = _stack_head_pair(q_ref[0, qrows, cols], lo)
            kw = k_ref[0, pl.ds(start, win), cols]
            vw = v_ref[0, pl.ds(start, win), cols]
            s = lax.dot_general(qs, kw, NT_DIMS, preferred_element_type=F32)
            s = s + bias_ref[p, rr]
            m = jnp.max(s, axis=-1, keepdims=True)
            e = jnp.exp2(s - m)
            l = jnp.sum(e, axis=-1, keepdims=True)
            pv = jnp.dot(e.astype(BF16), vw, preferred_element_type=F32)
            o = pv / l
            o_ref[0, qrows, cols] = jnp.where(lo, o[:GRID_W], o[GRID_W:]).astype(BF16)


def _na_attention(proj3, bias, rows_per_step=8):
    b, s, _ = proj3.shape
    n_rows = s // GRID_W
    tq = rows_per_step * GRID_W
    return pl.pallas_call(
        functools.partial(_na_kernel, rows_per_step=rows_per_step, n_rows=n_rows),
        grid=(b, n_rows // rows_per_step),
        in_specs=[
            pl.BlockSpec((1, tq, CHUNK), lambda bi, i: (bi, i, CH_QA)),
            pl.BlockSpec((1, s, CHUNK), lambda bi, i: (bi, 0, CH_KA)),
            pl.BlockSpec((1, s, CHUNK), lambda bi, i: (bi, 0, CH_VA)),
            pl.BlockSpec((2, NA_ROWS, 2 * GRID_W, NA_ROWS * GRID_W), lambda bi, i: (0, 0, 0, 0)),
        ],
        out_specs=pl.BlockSpec((1, tq, CHUNK), lambda bi, i: (bi, i, 0)),
        out_shape=jax.ShapeDtypeStruct((b, s, CHUNK), BF16),
        compiler_params=_cparams(("arbitrary", "arbitrary")),
        name="na_attention",
    )(proj3, proj3, proj3, bias)


def _na_bias_table(rpb):
    c = np.arange(GRID_W)
    cs = np.clip(c - NA_COLS // 2, 0, GRID_W - NA_COLS)
    kc = np.arange(GRID_W)
    inwin = (kc[None, :] >= cs[:, None]) & (kc[None, :] < cs[:, None] + NA_COLS)
    dc = np.clip(kc[None, :] - c[:, None] + NA_COLS - 1, 0, 2 * NA_COLS - 2)
    rr = np.arange(NA_ROWS)
    dr = rr[None, :] - rr[:, None] + NA_ROWS - 1
    tab = rpb[:, dr[:, None, :, None], dc[None, :, None, :]]
    tab = jnp.where(inwin[None, None, :, None, :], tab * LOG2E, NEG)
    tab = tab.reshape(2, 2, NA_ROWS, GRID_W, NA_ROWS * GRID_W)
    return tab.transpose(0, 2, 1, 3, 4).reshape(2, NA_ROWS, 2 * GRID_W, NA_ROWS * GRID_W)


DIL_TQ = 128
DIL_WIN = DIL_TQ + 2 * DIL_HALF


def _dil_kernel(q_ref, k_ref, v_ref, bias_ref, o_ref, lw_ref, *, length):
    ub = pl.program_id(2)
    nb = pl.num_programs(2)
    ws = pl.multiple_of(jnp.clip(ub * DIL_TQ - DIL_HALF, 0, length - DIL_WIN), DIL_HALF)
    case = jnp.where(ub == 0, 0, jnp.where(ub == nb - 1, 2, 1))
    lo = _lo_mask(DIL_TQ)
    for p in range(2):
        cols = slice(p * LANES, (p + 1) * LANES)
        qs = _stack_head_pair(q_ref[0, :, cols], lo)
        kw = k_ref[0, pl.ds(ws, DIL_WIN), cols]
        vw = v_ref[0, pl.ds(ws, DIL_WIN), cols]
        s = lax.dot_general(qs, kw, NT_DIMS, preferred_element_type=F32)
        s = s + bias_ref[case, p]
        m = jnp.max(s, axis=-1, keepdims=True)
        e = jnp.exp2(s - m)
        l = jnp.sum(e, axis=-1, keepdims=True)
        pv = jnp.dot(e.astype(BF16), vw, preferred_element_type=F32)
        o = pv / l
        lw = jnp.broadcast_to(m + jnp.log2(l), (2 * DIL_TQ, LANES))
        o_ref[0, :, cols] = jnp.where(lo, o[:DIL_TQ], o[DIL_TQ:]).astype(BF16)
        lw_ref[0, :, cols] = jnp.where(lo, lw[:DIL_TQ], lw[DIL_TQ:])


def _dil_bias_table(dil):
    slopes = np.exp2(-8.0 * np.arange(1, 5) / 4.0)
    qi = np.arange(DIL_TQ)[:, None]
    kj = np.arange(DIL_WIN)[None, :]
    tab = np.zeros((3, 4, DIL_TQ, DIL_WIN), np.float32)
    for case, shift in enumerate((0, -DIL_HALF, -2 * DIL_HALF)):
        rel = kj + shift - qi
        for h in range(4):
            tab[case, h] = np.where(np.abs(rel) <= DIL_HALF, -slopes[h] * dil * np.abs(rel) * LOG2E, NEG)
    return tab.reshape(3, 2, 2 * DIL_TQ, DIL_WIN)


def _dilated_branch(proj3, dil):
    b, s, _ = proj3.shape
    length = s // dil
    view = proj3.reshape(b, length, dil * PROJ_W)
    nb = length // DIL_TQ
    o, lw = pl.pallas_call(
        functools.partial(_dil_kernel, length=length),
        grid=(b, dil, nb),
        in_specs=[
            pl.BlockSpec((1, DIL_TQ, CHUNK), lambda bi, r, u: (bi, u, r * N_CHUNKS + CH_QB)),
            pl.BlockSpec((1, length, CHUNK), lambda bi, r, u: (bi, 0, r * N_CHUNKS + CH_KB)),
            pl.BlockSpec((1, length, CHUNK), lambda bi, r, u: (bi, 0, r * N_CHUNKS + CH_VB)),
            pl.BlockSpec((3, 2, 2 * DIL_TQ, DIL_WIN), lambda bi, r, u: (0, 0, 0, 0)),
        ],
        out_specs=[
            pl.BlockSpec((1, DIL_TQ, CHUNK), lambda bi, r, u: (bi, u, r)),
            pl.BlockSpec((1, DIL_TQ, CHUNK), lambda bi, r, u: (bi, u, r)),
        ],
        out_shape=[
            jax.ShapeDtypeStruct((b, length, dil * CHUNK), BF16),
            jax.ShapeDtypeStruct((b, length, dil * CHUNK), F32),
        ],
        compiler_params=_cparams(("arbitrary", "arbitrary", "arbitrary")),
        name=f"dilated_d{dil}",
    )(view, view, view, jnp.asarray(_dil_bias_table(dil)))
    return o.reshape(b * s, CHUNK), lw.reshape(b * s, CHUNK)


def _gqa_kernel(q_ref, k_ref, v_ref, o_ref, m_s, l_s, acc_s, *, tq, tk, seq):
    lo = _lo_mask(tq)
    n_kt = seq // tk
    for g in range(2):
        kcols = slice(g * LANES, (g + 1) * LANES)
        qs = jnp.concatenate(
            [_stack_head_pair(q_ref[0, :, (2 * g + j) * LANES:(2 * g + j + 1) * LANES], lo) for j in range(2)],
            axis=0)
        m_s[...] = jnp.full(m_s.shape, NEG, F32)
        l_s[...] = jnp.zeros(l_s.shape, F32)
        acc_s[...] = jnp.zeros(acc_s.shape, F32)

        def body(kt, carry):
            k0 = pl.multiple_of(kt * tk, tk)
            kw = k_ref[0, pl.ds(k0, tk), kcols]
            vw = v_ref[0, pl.ds(k0, tk), kcols]
            s = lax.dot_general(qs, kw, NT_DIMS, preferred_element_type=F32)
            m_old = m_s[...]
            m_new = jnp.maximum(m_old, jnp.max(s, axis=-1, keepdims=True))
            alpha = jnp.exp2(m_old - m_new)
            e = jnp.exp2(s - m_new)
            l_s[...] = alpha * l_s[...] + jnp.sum(e, axis=-1, keepdims=True)
            acc_s[...] = alpha * acc_s[...] + jnp.dot(e.astype(BF16), vw, preferred_element_type=F32)
            m_s[...] = m_new
            return carry

        lax.fori_loop(0, n_kt, body, 0)
        o = acc_s[...] / l_s[...]
        for j in range(2):
            pair = jnp.where(lo, o[(2 * j) * tq:(2 * j + 1) * tq], o[(2 * j + 1) * tq:(2 * j + 2) * tq])
            o_ref[0, :, (2 * g + j) * LANES:(2 * g + j + 1) * LANES] = pair.astype(BF16)


def _gqa_attention(proj3, tq=256, tk=512):
    b, s, _ = proj3.shape
    return pl.pallas_call(
        functools.partial(_gqa_kernel, tq=tq, tk=tk, seq=s),
        grid=(b, s // tq),
        in_specs=[
            pl.BlockSpec((1, tq, 2 * CHUNK), lambda bi, i: (bi, i, CH_QC // 2)),
            pl.BlockSpec((1, s, CHUNK), lambda bi, i: (bi, 0, CH_KC)),
            pl.BlockSpec((1, s, CHUNK), lambda bi, i: (bi, 0, CH_VC)),
        ],
        out_specs=pl.BlockSpec((1, tq, 2 * CHUNK), lambda bi, i: (bi, i, 0)),
        out_shape=jax.ShapeDtypeStruct((b, s, 2 * CHUNK), BF16),
        scratch_shapes=[
            pltpu.VMEM((4 * tq, 1), F32),
            pltpu.VMEM((4 * tq, 1), F32),
            pltpu.VMEM((4 * tq, LANES), F32),
        ],
        compiler_params=_cparams(("arbitrary", "arbitrary")),
        name="gqa_attention",
    )(proj3, proj3, proj3)


def _rms(v, g):
    return v * lax.rsqrt(jnp.mean(v * v, axis=-1, keepdims=True) + RMS_EPS) * g


def _outproj_kernel(oa_ref, o1_ref, o4_ref, o16_ref, lw1_ref, lw4_ref, lw16_ref, oc_ref, x_ref,
                    w_ref, mg_ref, g2_ref, wrh_ref, wrl_ref, br_ref, x1_ref, h_ref, route_ref):
    lw1, lw4, lw16 = lw1_ref[...], lw4_ref[...], lw16_ref[...]
    lwm = jnp.maximum(jnp.maximum(lw1, lw4), lw16)
    w1, w4, w16 = jnp.exp2(lw1 - lwm), jnp.exp2(lw4 - lwm), jnp.exp2(lw16 - lwm)
    ob = (w1 * o1_ref[...].astype(F32) + w4 * o4_ref[...].astype(F32)
          + w16 * o16_ref[...].astype(F32)) / (w1 + w4 + w16)
    merged = jnp.concatenate([
        _rms(oa_ref[...].astype(F32), mg_ref[:, 0:CHUNK]).astype(BF16),
        _rms(ob, mg_ref[:, CHUNK:2 * CHUNK]).astype(BF16),
        _rms(oc_ref[...].astype(F32), mg_ref[:, 2 * CHUNK:]).astype(BF16),
    ], axis=-1)
    x1 = x_ref[...] + jnp.dot(merged, w_ref[...], preferred_element_type=F32)
    x1_ref[...] = x1
    h = _rms(x1, g2_ref[...])
    h_ref[...] = h

    hh = h.astype(BF16)
    hl = (h - hh.astype(F32)).astype(BF16)
    logits = (jnp.dot(hh, wrh_ref[...], preferred_element_type=F32)
              + jnp.dot(hl, wrh_ref[...], preferred_element_type=F32)
              + jnp.dot(hh, wrl_ref[...], preferred_element_type=F32)) + br_ref[...]
    lane = lax.broadcasted_iota(jnp.int32, logits.shape, 1)
    lane_f = lane.astype(F32)
    big = float(LANES)
    is_g = lane < N_GROUPS
    gl = jnp.where(is_g, logits, NEG)
    gmax = jnp.max(gl, axis=-1, keepdims=True)
    grp = jnp.min(jnp.where(gl == gmax, lane_f, big), axis=-1, keepdims=True)
    p_grp = 1.0 / jnp.sum(jnp.where(is_g, jnp.exp(logits - gmax), 0.0), axis=-1, keepdims=True)
    e_lo = N_GROUPS + EXPERTS_PER_GROUP * grp
    in_grp = (lane_f >= e_lo) & (lane_f < e_lo + EXPERTS_PER_GROUP)
    el = jnp.where(in_grp, logits, NEG)
    t1 = jnp.max(el, axis=-1, keepdims=True)
    i1 = jnp.min(jnp.where(el == t1, lane_f, big), axis=-1, keepdims=True)
    el2 = jnp.where(lane_f == i1, NEG, el)
    t2 = jnp.max(el2, axis=-1, keepdims=True)
    i2 = jnp.min(jnp.where(el2 == t2, lane_f, big), axis=-1, keepdims=True)
    e21 = jnp.exp(t2 - t1)
    gate1 = p_grp / (1.0 + e21)
    gate2 = p_grp * e21 / (1.0 + e21)
    route = jnp.where(lane == 0, i1 - N_GROUPS,
                      jnp.where(lane == 1, i2 - N_GROUPS,
                                jnp.where(lane == 2, gate1, jnp.where(lane == 3, gate2, 0.0))))
    route_ref[...] = route


def _outproj(oa, ob_parts, oc, x2, w_out, mix_g, g2, wr_hi, wr_lo, b_r, tm=512):
    n = x2.shape[0]
    (o1, lw1), (o4, lw4), (o16, lw16) = ob_parts
    row = lambda w: pl.BlockSpec((tm, w), lambda i: (i, 0))
    full = lambda r, w: pl.BlockSpec((r, w), lambda i: (0, 0))
    return pl.pallas_call(
        _outproj_kernel,
        grid=(n // tm,),
        in_specs=[row(CHUNK), row(CHUNK), row(CHUNK), row(CHUNK), row(CHUNK), row(CHUNK), row(CHUNK),
                  row(2 * CHUNK), row(D_MODEL),
                  full(D_MODEL, D_MODEL), full(1, D_MODEL), full(1, D_MODEL),
                  full(D_MODEL, LANES), full(D_MODEL, LANES), full(1, LANES)],
        out_specs=[row(D_MODEL), row(D_MODEL), row(LANES)],
        out_shape=[jax.ShapeDtypeStruct((n, D_MODEL), F32),
                   jax.ShapeDtypeStruct((n, D_MODEL), F32),
                   jax.ShapeDtypeStruct((n, LANES), F32)],
        compiler_params=_cparams(("arbitrary",)),
        name="outproj_router",
    )(oa, o1, o4, o16, lw1, lw4, lw16, oc, x2, w_out, mix_g, g2, wr_hi, wr_lo, b_r)


META_ROWS = 384


def _rank_kernel(route_ref, dest_ref, meta_ref, cnt_s, base_s, tri_s, *, tm):
    phase = pl.program_id(0)
    i = pl.program_id(1)
    lane = lax.broadcasted_iota(jnp.int32, (tm, LANES), 1)
    lane_f = lane.astype(F32)
    route = route_ref[...]
    oh0 = jnp.where(lane_f == route[:, 0:1], 1.0, 0.0)
    oh1 = jnp.where(lane_f == route[:, 1:2], 1.0, 0.0)
    oh = oh0 + oh1
    colsum = jnp.sum(oh, axis=0, keepdims=True)

    @pl.when((phase == 0) & (i == 0))
    def _():
        cnt_s[...] = jnp.zeros(cnt_s.shape, F32)
        r = lax.broadcasted_iota(jnp.int32, (tm, tm), 0)
        c = lax.broadcasted_iota(jnp.int32, (tm, tm), 1)
        tri_s[...] = jnp.where(c < r, 1.0, 0.0).astype(BF16)

    @pl.when(phase == 0)
    def _():
        cnt_s[...] = cnt_s[...] + colsum

    @pl.when((phase == 1) & (i == 0))
    def _():
        lane8 = lax.broadcasted_iota(jnp.int32, cnt_s.shape, 1)
        padded = jnp.floor((cnt_s[...] + (MOE_BLOCK - 1)) * (1.0 / MOE_BLOCK)) * MOE_BLOCK
        ends = padded
        for sh in (1, 2, 4, 8, 16):
            ends = ends + jnp.where(lane8 >= sh, pltpu.roll(ends, sh, 1), 0.0)
        base_s[...] = ends - padded
        pos = lax.broadcasted_iota(jnp.int32, (META_ROWS, LANES), 0).astype(F32) * MOE_BLOCK
        lane_m = lax.broadcasted_iota(jnp.int32, (META_ROWS, LANES), 1)
        ends_row = ends[0:1, :]
        hit = jnp.where((lane_m < N_EXPERTS) & (ends_row <= pos), 1.0, 0.0)
        blk_e = jnp.minimum(jnp.sum(hit, axis=-1, keepdims=True), N_EXPERTS - 1.0)
        total = jnp.sum(jnp.where(lane_m[0:1] == N_EXPERTS - 1, ends_row, 0.0), axis=-1, keepdims=True)
        row_m = lax.broadcasted_iota(jnp.int32, (META_ROWS, 1), 0)
        meta = jnp.where(row_m == META_ROWS - 1, total * (1.0 / MOE_BLOCK), blk_e)
        meta_ref[...] = meta.astype(jnp.int32)

    @pl.when(phase == 1)
    def _():
        prior = jnp.dot(tri_s[...], oh.astype(BF16), preferred_element_type=F32)
        val = base_s[0:1, :] + prior
        d0 = jnp.sum(oh0 * val, axis=-1, keepdims=True)
        d1 = jnp.sum(oh1 * val, axis=-1, keepdims=True)
        dest_ref[...] = jnp.where(lane == 0, d0, jnp.where(lane == 1, d1, 0.0)).astype(jnp.int32)
        base_s[...] = base_s[...] + colsum


def _rank(route, tm=512):
    n = route.shape[0]
    return pl.pallas_call(
        functools.partial(_rank_kernel, tm=tm),
        grid=(2, n // tm),
        in_specs=[pl.BlockSpec((tm, LANES), lambda p, i: (i, 0))],
        out_specs=[pl.BlockSpec((tm, LANES), lambda p, i: (i * p, 0)),
                   pl.BlockSpec((META_ROWS, 1), lambda p, i: (0, 0))],
        out_shape=[jax.ShapeDtypeStruct((n, LANES), jnp.int32),
                   jax.ShapeDtypeStruct((META_ROWS, 1), jnp.int32)],
        scratch_shapes=[pltpu.VMEM((8, LANES), F32), pltpu.VMEM((8, LANES), F32), pltpu.VMEM((tm, tm), BF16)],
        compiler_params=_cparams(("arbitrary", "arbitrary")),
        name="moe_rank",
    )(route)


DMA_CHUNK = 256


def _dispatch_kernel(dest_ref, h_hbm, xs_in_hbm, xs_hbm, sem, *, n):
    del xs_in_hbm
    n_ch = n // DMA_CHUNK

    def issue(c, slot):
        def body(j, carry):
            t = c * DMA_CHUNK + j
            src = h_hbm.at[pl.ds(t, 1)]
            pltpu.make_async_copy(src, xs_hbm.at[pl.ds(dest_ref[t], 1)], sem.at[slot]).start()
            pltpu.make_async_copy(src, xs_hbm.at[pl.ds(dest_ref[n + t], 1)], sem.at[slot]).start()
            return carry
        lax.fori_loop(0, DMA_CHUNK, body, 0, unroll=8)

    def drain(slot):
        pltpu.make_async_copy(h_hbm.at[pl.ds(0, 2 * DMA_CHUNK)], xs_hbm.at[pl.ds(0, 2 * DMA_CHUNK)],
                              sem.at[slot]).wait()

    def outer(c, carry):
        slot = c % 2
        issue(c, slot)

        @pl.when(c > 0)
        def _():
            drain(1 - slot)
        return carry

    lax.fori_loop(0, n_ch, outer, 0)
    drain((n_ch - 1) % 2)


def _dispatch(dest_flat, h, cap):
    n = h.shape[0]
    zeros = jnp.zeros((cap, D_MODEL), F32)
    return pl.pallas_call(
        functools.partial(_dispatch_kernel, n=n),
        grid_spec=pltpu.PrefetchScalarGridSpec(
            num_scalar_prefetch=1,
            grid=(1,),
            in_specs=[pl.BlockSpec(memory_space=pl.ANY), pl.BlockSpec(memory_space=pl.ANY)],
            out_specs=pl.BlockSpec(memory_space=pl.ANY),
            scratch_shapes=[pltpu.SemaphoreType.DMA((2,))],
        ),
        out_shape=jax.ShapeDtypeStruct((cap, D_MODEL), F32),
        input_output_aliases={2: 0},
        compiler_params=pltpu.CompilerParams(dimension_semantics=("arbitrary",)),
        name="moe_dispatch",
    )(dest_flat, h, zeros)


def _expert_kernel(blk_e_ref, nact_ref, xs_ref, wg_ref, wu_ref, wd_ref, y_ref, wg_s, wu_s, wd_s):
    i = pl.program_id(0)
    active = i < nact_ref[0]
    changed = (i == 0) | (blk_e_ref[i] != blk_e_ref[jnp.maximum(i - 1, 0)])

    @pl.when(active & changed)
    def _():
        wg_s[...] = wg_ref[...].astype(BF16)
        wu_s[...] = wu_ref[...].astype(BF16)
        wd_s[...] = wd_ref[...].astype(BF16)

    @pl.when(active)
    def _():
        xb = xs_ref[...].astype(BF16)
        g = jnp.dot(xb, wg_s[...], preferred_element_type=F32)
        u = jnp.dot(xb, wu_s[...], preferred_element_type=F32)
        a = (g * jax.nn.sigmoid(g) * u).astype(BF16)
        y_ref[...] = jnp.dot(a, wd_s[...], preferred_element_type=F32)

    @pl.when(jnp.logical_not(active))
    def _():
        y_ref[...] = jnp.zeros(y_ref.shape, F32)


def _experts(blk_e, nact, xs, w_gate, w_up, w_down, layer):
    cap = xs.shape[0]
    n_blk = cap // MOE_BLOCK

    def blk(i, be, na):
        return jnp.minimum(i, na[0] - 1)

    return pl.pallas_call(
        _expert_kernel,
        grid_spec=pltpu.PrefetchScalarGridSpec(
            num_scalar_prefetch=2,
            grid=(n_blk,),
            in_specs=[
                pl.BlockSpec((MOE_BLOCK, D_MODEL), lambda i, be, na: (blk(i, be, na), 0)),
                pl.BlockSpec((None, None, D_MODEL, D_EXPERT), lambda i, be, na: (layer, be[blk(i, be, na)], 0, 0)),
                pl.BlockSpec((None, None, D_MODEL, D_EXPERT), lambda i, be, na: (layer, be[blk(i, be, na)], 0, 0)),
                pl.BlockSpec((None, None, D_EXPERT, D_MODEL), lambda i, be, na: (layer, be[blk(i, be, na)], 0, 0)),
            ],
            out_specs=pl.BlockSpec((MOE_BLOCK, D_MODEL), lambda i, be, na: (i, 0)),
            scratch_shapes=[pltpu.VMEM((D_MODEL, D_EXPERT), BF16), pltpu.VMEM((D_MODEL, D_EXPERT), BF16),
                            pltpu.VMEM((D_EXPERT, D_MODEL), BF16)],
        ),
        out_shape=jax.ShapeDtypeStruct((cap, D_MODEL), F32),
        compiler_params=_cparams(("arbitrary",)),
        name="moe_experts",
    )(blk_e, nact, xs, w_gate, w_up, w_down)


def _combine_kernel(dest_ref, x1_ref, route_ref, y_hbm, o_ref, yb, sem, *, n, tc):
    t0 = pl.program_id(0) * tc

    def body(j, carry):
        pltpu.make_async_copy(y_hbm.at[pl.ds(dest_ref[t0 + j], 1)], yb.at[0, pl.ds(j, 1)], sem.at[0]).start()
        pltpu.make_async_copy(y_hbm.at[pl.ds(dest_ref[n + t0 + j], 1)], yb.at[1, pl.ds(j, 1)], sem.at[0]).start()
        return carry

    lax.fori_loop(0, tc, body, 0, unroll=8)
    for k in range(2):
        pltpu.make_async_copy(y_hbm.at[pl.ds(0, tc)], yb.at[k], sem.at[0]).wait()
    route = route_ref[...]
    o_ref[...] = x1_ref[...] + route[:, 2:3] * yb[0] + route[:, 3:4] * yb[1]


def _combine(dest_flat, x1, route, y, tc=256):
    n = x1.shape[0]
    return pl.pallas_call(
        functools.partial(_combine_kernel, n=n, tc=tc),
        grid_spec=pltpu.PrefetchScalarGridSpec(
            num_scalar_prefetch=1,
            grid=(n // tc,),
            in_specs=[pl.BlockSpec((tc, D_MODEL), lambda i, d: (i, 0)),
                      pl.BlockSpec((tc, LANES), lambda i, d: (i, 0)),
                      pl.BlockSpec(memory_space=pl.ANY)],
            out_specs=pl.BlockSpec((tc, D_MODEL), lambda i, d: (i, 0)),
            scratch_shapes=[pltpu.VMEM((2, tc, D_MODEL), F32), pltpu.SemaphoreType.DMA((1,))],
        ),
        out_shape=jax.ShapeDtypeStruct((n, D_MODEL), F32),
        compiler_params=_cparams(("arbitrary",)),
        name="moe_combine",
    )(dest_flat, x1, route, y)


def _rope_tables(seq):
    n_freq = HEAD_DIM // 4
    t = np.arange(seq)
    inv_freq = jnp.asarray(ROPE_THETA, F32) ** (-jnp.arange(n_freq, dtype=F32) / n_freq)
    pos = jnp.stack([jnp.asarray(t // GRID_W, F32), jnp.asarray(t % GRID_W, F32)], axis=1)
    ang = pos[:, :, None] * inv_freq
    cos = jnp.broadcast_to(jnp.cos(ang)[:, :, None, :], (seq, 2, 2, n_freq)).reshape(seq, HEAD_DIM)
    sin = jnp.broadcast_to(jnp.sin(ang)[:, :, None, :], (seq, 2, 2, n_freq)).reshape(seq, HEAD_DIM)
    first_half = (np.arange(HEAD_DIM) % (2 * n_freq)) < n_freq
    sa = jnp.where(first_half, -sin, 0.0)
    sb = jnp.where(first_half, 0.0, sin)
    rep = CHUNK // HEAD_DIM
    return jnp.tile(cos, (1, rep)), jnp.tile(sa, (1, rep)), jnp.tile(sb, (1, rep))


def _prep_w_in(w):
    base = w[:, :2048]
    kc0, kc1 = w[:, 2048:2112], w[:, 2112:2176]
    vc0, vc1 = w[:, 2176:2240], w[:, 2240:2304]
    return jnp.concatenate([base, kc0, kc0, kc1, kc1, vc0, vc0, vc1, vc1], axis=1).astype(BF16)


def _prep_gain(qk_g):
    qs = SCALE * LOG2E
    ones = jnp.ones((CHUNK,), F32)
    t4 = lambda g: jnp.tile(g, CHUNK // HEAD_DIM)
    return jnp.concatenate([
        t4(qk_g[0, 0]) * qs, t4(qk_g[0, 1]), ones,
        t4(qk_g[1, 0]) * qs, t4(qk_g[1, 1]), ones,
        t4(qk_g[2, 0]) * qs, t4(qk_g[2, 0]) * qs, t4(qk_g[2, 1]), ones,
    ])[None, :]


def _split_hi_lo(w):
    hi = w.astype(BF16)
    lo = (w - hi.astype(F32)).astype(BF16)
    return hi, lo


def _mixer_layer(x2, bsz, seq, tables, norm1_g, w_in, qk_g, rpb, mix_g, w_out, norm2_g, w_rg, b_rg, w_re, b_re):
    n, d = x2.shape
    cos_t, sa_t, sb_t, bd = tables
    proj = _inproj(x2, norm1_g[None, :], _prep_w_in(w_in), _prep_gain(qk_g), bd, cos_t, sa_t, sb_t, seq)
    proj3 = proj.reshape(bsz, seq, PROJ_W)
    oa = _na_attention(proj3, _na_bias_table(rpb)).reshape(n, CHUNK)
    ob_parts = [_dilated_branch(proj3, dil) for dil in DILATIONS]
    oc = _gqa_attention(proj3).reshape(n, 2 * CHUNK)

    w_r = jnp.zeros((d, LANES), F32)
    w_r = w_r.at[:, :N_GROUPS].set(w_rg).at[:, N_GROUPS:N_GROUPS + N_EXPERTS].set(w_re)
    b_r = jnp.zeros((1, LANES), F32)
    b_r = b_r.at[0, :N_GROUPS].set(b_rg).at[0, N_GROUPS:N_GROUPS + N_EXPERTS].set(b_re)
    wr_hi, wr_lo = _split_hi_lo(w_r)
    return _outproj(oa, ob_parts, oc, x2, w_out.astype(BF16), mix_g[None, :], norm2_g[None, :], wr_hi, wr_lo, b_r)


def _moe_layer(x1, h, route, w_gate, w_up, w_down, layer):
    n = x1.shape[0]
    cap = 2 * n + N_EXPERTS * MOE_BLOCK
    n_blk = cap // MOE_BLOCK
    dest, meta = _rank(route)
    dest_flat = dest[:, :2].T.reshape(2 * n)
    xs = _dispatch(dest_flat, h, cap)
    y = _experts(meta[:n_blk, 0], meta[META_ROWS - 1:, 0], xs, w_gate, w_up, w_down, layer)
    return _combine(dest_flat, x1, route, y)


def _tables(seq):
    hd = np.arange(CHUNK) // HEAD_DIM
    return (*_rope_tables(seq), jnp.asarray(hd[:, None] == hd[None, :], BF16))


@jax.jit
def kernel(x, norm1_g, w_in, qk_norm_g, na_rpb, mix_norm_g, w_out, norm2_g, w_router_group, b_router_group,
           w_router_expert, b_router_expert, w_gate, w_up, w_down):
    bsz, seq, d = x.shape
    tables = _tables(seq)
    x2 = x.reshape(bsz * seq, d)
    for l in range(DEPTH):
        x1, h, route = _mixer_layer(x2, bsz, seq, tables, norm1_g[l], w_in[l], qk_norm_g[l], na_rpb[l],
                                    mix_norm_g[l], w_out[l], norm2_g[l], w_router_group[l], b_router_group[l],
                                    w_router_expert[l], b_router_expert[l])
        x2 = _moe_layer(x1, h, route, w_gate, w_up, w_down, l)
    return x2.reshape(bsz, seq, d)
```

```python
import functools
import math

import numpy as np
import jax
import jax.numpy as jnp
from jax import lax
from jax.experimental import pallas as pl
from jax.experimental.pallas import tpu as pltpu

F32 = jnp.float32
BF16 = jnp.bfloat16

D_MODEL = 1024
DEPTH = 4
GRID_W = 64
HEAD_DIM = 64
RMS_EPS = 1e-6
NEG = -1e30
LOG2E = math.log2(math.e)
SCALE = HEAD_DIM ** -0.5

NA_ROWS = 8
NA_COLS = 16
DILATIONS = (1, 4, 16)
DIL_HALF = 64
ROPE_THETA = 10000.0

N_GROUPS = 4
EXPERTS_PER_GROUP = 8
N_EXPERTS = 32
D_EXPERT = 512
MOE_BLOCK = 256

LANES = 128
CHUNK = 256
CH_QA, CH_KA, CH_VA, CH_QB, CH_KB, CH_VB, CH_QC, CH_KC, CH_VC = 0, 1, 2, 3, 4, 5, 6, 8, 9
N_CHUNKS = 10
PROJ_W = N_CHUNKS * CHUNK
NORM_CHUNKS = (CH_QA, CH_KA, CH_QB, CH_KB, CH_QC, CH_QC + 1, CH_KC)
ROPE_CHUNKS = (CH_QC, CH_QC + 1, CH_KC)

VMEM_LIMIT = 56 * 1024 * 1024

NT_DIMS = (((1,), (1,)), ((), ()))


def _cparams(sem):
    return pltpu.CompilerParams(dimension_semantics=sem, vmem_limit_bytes=VMEM_LIMIT)


def _inproj_kernel(x_ref, g1_ref, w_ref, gain_ref, bd_ref, cos_ref, sa_ref, sb_ref, o_ref, o4_ref, o16_ref, stage_ref):
    tm = x_ref.shape[0]
    x = x_ref[...]
    ms = jnp.mean(x * x, axis=-1, keepdims=True)
    h = (x * lax.rsqrt(ms + RMS_EPS) * g1_ref[...]).astype(BF16)
    for c in range(N_CHUNKS):
        cols = slice(c * CHUNK, (c + 1) * CHUNK)
        p = jnp.dot(h, w_ref[:, cols], preferred_element_type=F32)
        if c in NORM_CHUNKS:
            ss = jnp.dot((p * p).astype(BF16), bd_ref[...], preferred_element_type=F32)
            p = p * lax.rsqrt(ss * (1.0 / HEAD_DIM) + RMS_EPS) * gain_ref[:, cols]
        if c in ROPE_CHUNKS:
            p = (p * cos_ref[...] + pltpu.roll(p, CHUNK - 16, 1) * sa_ref[...]
                 + pltpu.roll(p, 16, 1) * sb_ref[...])
        o_ref[:, cols] = p.astype(BF16)
        if c in (CH_QB, CH_KB, CH_VB):
            for hf in range(CHUNK // LANES):
                stage_ref[hf] = p[:, hf * LANES:(hf + 1) * LANES]
                d0 = (c - CH_QB) * CHUNK + hf * LANES
                for dil, od_ref in ((4, o4_ref), (16, o16_ref)):
                    for r in range(dil):
                        rows = stage_ref[hf, pl.ds(r, tm // dil, stride=dil), :]
                        od_ref[0, r, :, d0:d0 + LANES] = rows.astype(BF16)


def _inproj(x2, g1, w, gain, bd, cos_t, sa_t, sb_t, seq, tm=512):
    n = x2.shape[0]
    nsb = seq // tm
    bsz = n // seq
    dil_spec = lambda dil: pl.BlockSpec((1, dil, tm // dil, 3 * CHUNK), lambda i: (i // nsb, 0, i % nsb, 0))
    dil_shape = lambda dil: jax.ShapeDtypeStruct((bsz, dil, seq // dil, 3 * CHUNK), BF16)
    return pl.pallas_call(
        _inproj_kernel,
        grid=(n // tm,),
        in_specs=[
            pl.BlockSpec((tm, D_MODEL), lambda i: (i, 0)),
            pl.BlockSpec((1, D_MODEL), lambda i: (0, 0)),
            pl.BlockSpec((D_MODEL, PROJ_W), lambda i: (0, 0)),
            pl.BlockSpec((1, PROJ_W), lambda i: (0, 0)),
            pl.BlockSpec((CHUNK, CHUNK), lambda i: (0, 0)),
            pl.BlockSpec((tm, CHUNK), lambda i: (i % nsb, 0)),
            pl.BlockSpec((tm, CHUNK), lambda i: (i % nsb, 0)),
            pl.BlockSpec((tm, CHUNK), lambda i: (i % nsb, 0)),
        ],
        out_specs=[pl.BlockSpec((tm, PROJ_W), lambda i: (i, 0)), dil_spec(4), dil_spec(16)],
        out_shape=[jax.ShapeDtypeStruct((n, PROJ_W), BF16), dil_shape(4), dil_shape(16)],
        scratch_shapes=[pltpu.VMEM((CHUNK // LANES, tm, LANES), F32)],
        compiler_params=_cparams(("arbitrary",)),
        name="inproj",
    )(x2, g1, w, gain, bd, cos_t, sa_t, sb_t)


def _stack_head_pair(qp, lo):
    zero = jnp.zeros_like(qp)
    return jnp.concatenate([jnp.where(lo, qp, zero), jnp.where(lo, zero, qp)], axis=0)


def _lo_mask(rows):
    return lax.broadcasted_iota(jnp.int32, (rows, LANES), 1) < HEAD_DIM


def _na_kernel(q_ref, k_ref, v_ref, bias_ref, o_ref, *, rows_per_step, n_rows):
    i = pl.program_id(1)
    lo = _lo_mask(GRID_W)
    win = NA_ROWS * GRID_W
    for rl in range(rows_per_step):
        r = i * rows_per_step + rl
        r0 = jnp.clip(r - NA_ROWS // 2, 0, n_rows - NA_ROWS)
        rr = r - r0
        start = pl.multiple_of(r0 * GRID_W, GRID_W)
        qrows = slice(rl * GRID_W, (rl + 1) * GRID_W)
        for p in range(2):
            cols = slice(p * LANES, (p + 1) * LANES)
            qs = _stack_head_pair(q_ref[0, qrows, cols], lo)
            kw = k_ref[0, pl.ds(start, win), cols]
            vw = v_ref[0, pl.ds(start, win), cols]
            s = lax.dot_general(qs, kw, NT_DIMS, preferred_element_type=F32)
            s = s + bias_ref[p, rr]
            m = jnp.max(s, axis=-1, keepdims=True)
            e = jnp.exp2(s - m)
            l = jnp.sum(e, axis=-1, keepdims=True)
            pv = jnp.dot(e.astype(BF16), vw, preferred_element_type=F32)
            o = pv / l
            o_ref[0, qrows, cols] = jnp.where(lo, o[:GRID_W], o[GRID_W:]).astype(BF16)


def _na_attention(proj3, bias, rows_per_step=8):
    b, s, _ = proj3.shape
    n_rows = s // GRID_W
    tq = rows_per_step * GRID_W
    return pl.pallas_call(
        functools.partial(_na_kernel, rows_per_step=rows_per_step, n_rows=n_rows),
        grid=(b, n_rows // rows_per_step),
        in_specs=[
            pl.BlockSpec((1, tq, CHUNK), lambda bi, i: (bi, i, CH_QA)),
            pl.BlockSpec((1, s, CHUNK), lambda bi, i: (bi, 0, CH_KA)),
            pl.BlockSpec((1, s, CHUNK), lambda bi, i: (bi, 0, CH_VA)),
            pl.BlockSpec((2, NA_ROWS, 2 * GRID_W, NA_ROWS * GRID_W), lambda bi, i: (0, 0, 0, 0)),
        ],
        out_specs=pl.BlockSpec((1, tq, CHUNK), lambda bi, i: (bi, i, 0)),
        out_shape=jax.ShapeDtypeStruct((b, s, CHUNK), BF16),
        compiler_params=_cparams(("arbitrary", "arbitrary")),
        name="na_attention",
    )(proj3, proj3, proj3, bias)


def _na_bias_table(rpb):
    c = np.arange(GRID_W)
    cs = np.clip(c - NA_COLS // 2, 0, GRID_W - NA_COLS)
    kc = np.arange(GRID_W)
    inwin = (kc[None, :] >= cs[:, None]) & (kc[None, :] < cs[:, None] + NA_COLS)
    dc = kc[None, :] - c[:, None] + NA_COLS - 1
    rr = np.arange(NA_ROWS)
    dr = rr[None, :] - rr[:, None] + NA_ROWS - 1
    sel_r = jnp.asarray(dr[:, :, None] == np.arange(2 * NA_ROWS - 1), F32)
    sel_c = jnp.asarray((dc[:, :, None] == np.arange(2 * NA_COLS - 1)) & inwin[:, :, None], F32)
    tab = jnp.einsum("rka,hab,cjb->hrckj", sel_r, rpb, sel_c, precision=lax.Precision.HIGHEST)
    tab = jnp.where(inwin[None, None, :, None, :], tab * LOG2E, NEG)
    tab = tab.reshape(2, 2, NA_ROWS, GRID_W, NA_ROWS * GRID_W)
    return tab.transpose(0, 2, 1, 3, 4).reshape(2, NA_ROWS, 2 * GRID_W, NA_ROWS * GRID_W)


DIL_TQ = 128
DIL_WIN = DIL_TQ + 2 * DIL_HALF


def _dil_kernel(q_ref, k_ref, v_ref, bias_ref, o_ref, lw_ref, *, length):
    ub = pl.program_id(2)
    nb = pl.num_programs(2)
    ws = pl.multiple_of(jnp.clip(ub * DIL_TQ - DIL_HALF, 0, length - DIL_WIN), DIL_HALF)
    case = jnp.where(ub == 0, 0, jnp.where(ub == nb - 1, 2, 1))
    lo = _lo_mask(DIL_TQ)
    for p in range(2):
        cols = slice(p * LANES, (p + 1) * LANES)
        qs = _stack_head_pair(q_ref[:, cols], lo)
        kw = k_ref[pl.ds(ws, DIL_WIN), cols]
        vw = v_ref[pl.ds(ws, DIL_WIN), cols]
        s = lax.dot_general(qs, kw, NT_DIMS, preferred_element_type=F32)
        s = s + bias_ref[case, p]
        m = jnp.max(s, axis=-1, keepdims=True)
        e = jnp.exp2(s - m)
        l = jnp.sum(e, axis=-1, keepdims=True)
        pv = jnp.dot(e.astype(BF16), vw, preferred_element_type=F32)
        o = pv / l
        lw = jnp.broadcast_to(m + jnp.log2(l), (2 * DIL_TQ, LANES))
        o_ref[0, :, cols] = jnp.where(lo, o[:DIL_TQ], o[DIL_TQ:]).astype(BF16)
        lw_ref[0, :, cols] = jnp.where(lo, lw[:DIL_TQ], lw[DIL_TQ:])


def _dil_bias_table(dil):
    slopes = np.exp2(-8.0 * np.arange(1, 5) / 4.0)
    qi = np.arange(DIL_TQ)[:, None]
    kj = np.arange(DIL_WIN)[None, :]
    tab = np.zeros((3, 4, DIL_TQ, DIL_WIN), np.float32)
    for case, shift in enumerate((0, -DIL_HALF, -2 * DIL_HALF)):
        rel = kj + shift - qi
        for h in range(4):
            tab[case, h] = np.where(np.abs(rel) <= DIL_HALF, -slopes[h] * dil * np.abs(rel) * LOG2E, NEG)
    return tab.reshape(3, 2, 2 * DIL_TQ, DIL_WIN)


def _dilated_branch(src, q_chunk):
    b, dil, length, _ = src.shape
    nb = length // DIL_TQ
    o, lw = pl.pallas_call(
        functools.partial(_dil_kernel, length=length),
        grid=(b, dil, nb),
        in_specs=[
            pl.BlockSpec((None, None, DIL_TQ, CHUNK), lambda bi, r, u: (bi, r, u, q_chunk)),
            pl.BlockSpec((None, None, length, CHUNK), lambda bi, r, u: (bi, r, 0, q_chunk + 1)),
            pl.BlockSpec((None, None, length, CHUNK), lambda bi, r, u: (bi, r, 0, q_chunk + 2)),
            pl.BlockSpec((3, 2, 2 * DIL_TQ, DIL_WIN), lambda bi, r, u: (0, 0, 0, 0)),
        ],
        out_specs=[
            pl.BlockSpec((1, DIL_TQ, CHUNK), lambda bi, r, u: (bi, u, r)),
            pl.BlockSpec((1, DIL_TQ, CHUNK), lambda bi, r, u: (bi, u, r)),
        ],
        out_shape=[
            jax.ShapeDtypeStruct((b, length, dil * CHUNK), BF16),
            jax.ShapeDtypeStruct((b, length, dil * CHUNK), F32),
        ],
        compiler_params=_cparams(("arbitrary", "arbitrary", "arbitrary")),
        name=f"dilated_d{dil}",
    )(src, src, src, jnp.asarray(_dil_bias_table(dil)))
    n = b * dil * length
    return o.reshape(n, CHUNK), lw.reshape(n, CHUNK)


def _gqa_kernel(q_ref, k_ref, v_ref, o_ref, m_s, l_s, acc_s, *, tq, tk, seq):
    lo = _lo_mask(tq)
    n_kt = seq // tk
    for g in range(2):
        kcols = slice(g * LANES, (g + 1) * LANES)
        qs = jnp.concatenate(
            [_stack_head_pair(q_ref[0, :, (2 * g + j) * LANES:(2 * g + j + 1) * LANES], lo) for j in range(2)],
            axis=0)
        m_s[...] = jnp.full(m_s.shape, NEG, F32)
        l_s[...] = jnp.zeros(l_s.shape, F32)
        acc_s[...] = jnp.zeros(acc_s.shape, F32)

        def body(kt, carry):
            k0 = pl.multiple_of(kt * tk, tk)
            kw = k_ref[0, pl.ds(k0, tk), kcols]
            vw = v_ref[0, pl.ds(k0, tk), kcols]
            s = lax.dot_general(qs, kw, NT_DIMS, preferred_element_type=F32)
            m_old = m_s[...]
            m_new = jnp.maximum(m_old, jnp.max(s, axis=-1, keepdims=True))
            alpha = jnp.exp2(m_old - m_new)
            e = jnp.exp2(s - pltpu.repeat(m_new, tk // LANES, axis=1))
            l_s[...] = alpha * l_s[...] + jnp.sum(e, axis=-1, keepdims=True)
            acc_s[...] = alpha * acc_s[...] + jnp.dot(e.astype(BF16), vw, preferred_element_type=F32)
            m_s[...] = m_new
            return carry

        lax.fori_loop(0, n_kt, body, 0)
        o = acc_s[...] / l_s[...]
        for j in range(2):
            pair = jnp.where(lo, o[(2 * j) * tq:(2 * j + 1) * tq], o[(2 * j + 1) * tq:(2 * j + 2) * tq])
            o_ref[0, :, (2 * g + j) * LANES:(2 * g + j + 1) * LANES] = pair.astype(BF16)


def _gqa_attention(proj3, tq=256, tk=512):
    b, s, _ = proj3.shape
    return pl.pallas_call(
        functools.partial(_gqa_kernel, tq=tq, tk=tk, seq=s),
        grid=(b, s // tq),
        in_specs=[
            pl.BlockSpec((1, tq, 2 * CHUNK), lambda bi, i: (bi, i, CH_QC // 2)),
            pl.BlockSpec((1, s, CHUNK), lambda bi, i: (bi, 0, CH_KC)),
            pl.BlockSpec((1, s, CHUNK), lambda bi, i: (bi, 0, CH_VC)),
        ],
        out_specs=pl.BlockSpec((1, tq, 2 * CHUNK), lambda bi, i: (bi, i, 0)),
        out_shape=jax.ShapeDtypeStruct((b, s, 2 * CHUNK), BF16),
        scratch_shapes=[
            pltpu.VMEM((4 * tq, LANES), F32),
            pltpu.VMEM((4 * tq, LANES), F32),
            pltpu.VMEM((4 * tq, LANES), F32),
        ],
        compiler_params=_cparams(("arbitrary", "arbitrary")),
        name="gqa_attention",
    )(proj3, proj3, proj3)


def _rms(v, g):
    return v * lax.rsqrt(jnp.mean(v * v, axis=-1, keepdims=True) + RMS_EPS) * g


def _outproj_kernel(oa_ref, o1_ref, o4_ref, o16_ref, lw1_ref, lw4_ref, lw16_ref, oc_ref, x_ref,
                    w_ref, mg_ref, g2_ref, wrh_ref, wrl_ref, br_ref, x1_ref, h_ref, route_ref):
    lw1, lw4, lw16 = lw1_ref[...], lw4_ref[...], lw16_ref[...]
    lwm = jnp.maximum(jnp.maximum(lw1, lw4), lw16)
    w1, w4, w16 = jnp.exp2(lw1 - lwm), jnp.exp2(lw4 - lwm), jnp.exp2(lw16 - lwm)
    ob = (w1 * o1_ref[...].astype(F32) + w4 * o4_ref[...].astype(F32)
          + w16 * o16_ref[...].astype(F32)) / (w1 + w4 + w16)
    merged = jnp.concatenate([
        _rms(oa_ref[...].astype(F32), mg_ref[:, 0:CHUNK]).astype(BF16),
        _rms(ob, mg_ref[:, CHUNK:2 * CHUNK]).astype(BF16),
        _rms(oc_ref[...].astype(F32), mg_ref[:, 2 * CHUNK:]).astype(BF16),
    ], axis=-1)
    x1 = x_ref[...] + jnp.dot(merged, w_ref[...], preferred_element_type=F32)
    x1_ref[...] = x1
    h = _rms(x1, g2_ref[...])
    h_ref[...] = h

    hh = h.astype(BF16)
    hl = (h - hh.astype(F32)).astype(BF16)
    logits = (jnp.dot(hh, wrh_ref[...], preferred_element_type=F32)
              + jnp.dot(hl, wrh_ref[...], preferred_element_type=F32)
              + jnp.dot(hh, wrl_ref[...], preferred_element_type=F32)) + br_ref[...]
    lane = lax.broadcasted_iota(jnp.int32, logits.shape, 1)
    lane_f = lane.astype(F32)
    big = float(LANES)
    is_g = lane < N_GROUPS
    gl = jnp.where(is_g, logits, NEG)
    gmax = jnp.max(gl, axis=-1, keepdims=True)
    grp = jnp.min(jnp.where(gl == gmax, lane_f, big), axis=-1, keepdims=True)
    p_grp = 1.0 / jnp.sum(jnp.where(is_g, jnp.exp(logits - gmax), 0.0), axis=-1, keepdims=True)
    e_lo = N_GROUPS + EXPERTS_PER_GROUP * grp
    in_grp = (lane_f >= e_lo) & (lane_f < e_lo + EXPERTS_PER_GROUP)
    el = jnp.where(in_grp, logits, NEG)
    t1 = jnp.max(el, axis=-1, keepdims=True)
    i1 = jnp.min(jnp.where(el == t1, lane_f, big), axis=-1, keepdims=True)
    el2 = jnp.where(lane_f == i1, NEG, el)
    t2 = jnp.max(el2, axis=-1, keepdims=True)
    i2 = jnp.min(jnp.where(el2 == t2, lane_f, big), axis=-1, keepdims=True)
    e21 = jnp.exp(t2 - t1)
    gate1 = p_grp / (1.0 + e21)
    gate2 = p_grp * e21 / (1.0 + e21)
    route = jnp.where(lane == 0, i1 - N_GROUPS,
                      jnp.where(lane == 1, i2 - N_GROUPS,
                                jnp.where(lane == 2, gate1, jnp.where(lane == 3, gate2, 0.0))))
    route_ref[...] = route


def _outproj(oa, ob_parts, oc, x2, w_out, mix_g, g2, wr_hi, wr_lo, b_r, tm=512):
    n = x2.shape[0]
    (o1, lw1), (o4, lw4), (o16, lw16) = ob_parts
    row = lambda w: pl.BlockSpec((tm, w), lambda i: (i, 0))
    full = lambda r, w: pl.BlockSpec((r, w), lambda i: (0, 0))
    return pl.pallas_call(
        _outproj_kernel,
        grid=(n // tm,),
        in_specs=[row(CHUNK), row(CHUNK), row(CHUNK), row(CHUNK), row(CHUNK), row(CHUNK), row(CHUNK),
                  row(2 * CHUNK), row(D_MODEL),
                  full(D_MODEL, D_MODEL), full(1, D_MODEL), full(1, D_MODEL),
                  full(D_MODEL, LANES), full(D_MODEL, LANES), full(1, LANES)],
        out_specs=[row(D_MODEL), row(D_MODEL), row(LANES)],
        out_shape=[jax.ShapeDtypeStruct((n, D_MODEL), F32),
                   jax.ShapeDtypeStruct((n, D_MODEL), F32),
                   jax.ShapeDtypeStruct((n, LANES), F32)],
        compiler_params=_cparams(("arbitrary",)),
        name="outproj_router",
    )(oa, o1, o4, o16, lw1, lw4, lw16, oc, x2, w_out, mix_g, g2, wr_hi, wr_lo, b_r)


META_ROWS = 384


def _rank_kernel(route_ref, dest_ref, meta_ref, cnt_s, base_s, tri_s, *, tm):
    phase = pl.program_id(0)
    i = pl.program_id(1)
    lane = lax.broadcasted_iota(jnp.int32, (tm, LANES), 1)
    lane_f = lane.astype(F32)
    route = route_ref[...]
    oh0 = jnp.where(lane_f == route[:, 0:1], 1.0, 0.0)
    oh1 = jnp.where(lane_f == route[:, 1:2], 1.0, 0.0)
    oh = oh0 + oh1
    colsum = jnp.sum(oh, axis=0, keepdims=True)

    @pl.when((phase == 0) & (i == 0))
    def _():
        cnt_s[...] = jnp.zeros(cnt_s.shape, F32)
        r = lax.broadcasted_iota(jnp.int32, (tm, tm), 0)
        c = lax.broadcasted_iota(jnp.int32, (tm, tm), 1)
        tri_s[...] = jnp.where(c < r, 1.0, 0.0).astype(BF16)

    @pl.when(phase == 0)
    def _():
        cnt_s[...] = cnt_s[...] + colsum

    @pl.when((phase == 1) & (i == 0))
    def _():
        lane8 = lax.broadcasted_iota(jnp.int32, cnt_s.shape, 1)
        padded = jnp.floor((cnt_s[...] + (MOE_BLOCK - 1)) * (1.0 / MOE_BLOCK)) * MOE_BLOCK
        ends = padded
        for sh in (1, 2, 4, 8, 16):
            ends = ends + jnp.where(lane8 >= sh, pltpu.roll(ends, sh, 1), 0.0)
        base_s[...] = ends - padded
        pos = lax.broadcasted_iota(jnp.int32, (META_ROWS, LANES), 0).astype(F32) * MOE_BLOCK
        lane_m = lax.broadcasted_iota(jnp.int32, (META_ROWS, LANES), 1)
        ends_row = ends[0:1, :]
        hit = jnp.where((lane_m < N_EXPERTS) & (ends_row <= pos), 1.0, 0.0)
        blk_e = jnp.minimum(jnp.sum(hit, axis=-1, keepdims=True), N_EXPERTS - 1.0)
        total = jnp.sum(jnp.where(lane_m[0:1] == N_EXPERTS - 1, ends_row, 0.0), axis=-1, keepdims=True)
        row_m = lax.broadcasted_iota(jnp.int32, (META_ROWS, 1), 0)
        meta = jnp.where(row_m == META_ROWS - 1, total * (1.0 / MOE_BLOCK), blk_e)
        meta_ref[...] = meta.astype(jnp.int32)

    @pl.when(phase == 1)
    def _():
        prior = jnp.dot(tri_s[...], oh.astype(BF16), preferred_element_type=F32)
        val = base_s[0:1, :] + prior
        d0 = jnp.sum(oh0 * val, axis=-1, keepdims=True)
        d1 = jnp.sum(oh1 * val, axis=-1, keepdims=True)
        dest_ref[...] = jnp.where(lane == 0, d0, jnp.where(lane == 1, d1, 0.0)).astype(jnp.int32)
        base_s[...] = base_s[...] + colsum


def _rank(route, tm=512):
    n = route.shape[0]
    return pl.pallas_call(
        functools.partial(_rank_kernel, tm=tm),
        grid=(2, n // tm),
        in_specs=[pl.BlockSpec((tm, LANES), lambda p, i: (i, 0))],
        out_specs=[pl.BlockSpec((tm, LANES), lambda p, i: (i * p, 0)),
                   pl.BlockSpec((META_ROWS, 1), lambda p, i: (0, 0))],
        out_shape=[jax.ShapeDtypeStruct((n, LANES), jnp.int32),
                   jax.ShapeDtypeStruct((META_ROWS, 1), jnp.int32)],
        scratch_shapes=[pltpu.VMEM((8, LANES), F32), pltpu.VMEM((8, LANES), F32), pltpu.VMEM((tm, tm), BF16)],
        compiler_params=_cparams(("arbitrary", "arbitrary")),
        name="moe_rank",
    )(route)


def _dispatch_kernel(dest_ref, h_ref, xs_in_hbm, xs_hbm, sem, *, n, tc):
    del xs_in_hbm
    t0 = pl.program_id(0) * tc

    def body(j, carry):
        src = h_ref.at[pl.ds(j, 1)]
        pltpu.make_async_copy(src, xs_hbm.at[pl.ds(dest_ref[t0 + j], 1)], sem.at[0]).start()
        pltpu.make_async_copy(src, xs_hbm.at[pl.ds(dest_ref[n + t0 + j], 1)], sem.at[0]).start()
        return carry

    lax.fori_loop(0, tc, body, 0, unroll=8)
    for _ in range(2):
        pltpu.make_async_copy(h_ref, xs_hbm.at[pl.ds(0, tc)], sem.at[0]).wait()


def _dispatch(dest_flat, h, cap, tc=256):
    n = h.shape[0]
    zeros = jnp.zeros((cap, D_MODEL), F32)
    return pl.pallas_call(
        functools.partial(_dispatch_kernel, n=n, tc=tc),
        grid_spec=pltpu.PrefetchScalarGridSpec(
            num_scalar_prefetch=1,
            grid=(n // tc,),
            in_specs=[pl.BlockSpec((tc, D_MODEL), lambda i, d: (i, 0)), pl.BlockSpec(memory_space=pl.ANY)],
            out_specs=pl.BlockSpec(memory_space=pl.ANY),
            scratch_shapes=[pltpu.SemaphoreType.DMA((1,))],
        ),
        out_shape=jax.ShapeDtypeStruct((cap, D_MODEL), F32),
        input_output_aliases={2: 0},
        compiler_params=_cparams(("arbitrary",)),
        name="moe_dispatch",
    )(dest_flat, h, zeros)


def _expert_kernel(blk_e_ref, nact_ref, xs_ref, wg_ref, wu_ref, wd_ref, y_ref, wg_s, wu_s, wd_s):
    i = pl.program_id(0)
    active = i < nact_ref[0]
    changed = (i == 0) | (blk_e_ref[i] != blk_e_ref[jnp.maximum(i - 1, 0)])

    @pl.when(active & changed)
    def _():
        wg_s[...] = wg_ref[...].astype(BF16)
        wu_s[...] = wu_ref[...].astype(BF16)
        wd_s[...] = wd_ref[...].astype(BF16)

    @pl.when(active)
    def _():
        xb = xs_ref[...].astype(BF16)
        g = jnp.dot(xb, wg_s[...], preferred_element_type=F32)
        u = jnp.dot(xb, wu_s[...], preferred_element_type=F32)
        a = (g * jax.nn.sigmoid(g) * u).astype(BF16)
        y_ref[...] = jnp.dot(a, wd_s[...], preferred_element_type=F32)

    @pl.when(jnp.logical_not(active))
    def _():
        y_ref[...] = jnp.zeros(y_ref.shape, F32)


def _experts(blk_e, nact, xs, w_gate, w_up, w_down, layer):
    cap = xs.shape[0]
    n_blk = cap // MOE_BLOCK

    def blk(i, be, na):
        return jnp.minimum(i, na[0] - 1)

    return pl.pallas_call(
        _expert_kernel,
        grid_spec=pltpu.PrefetchScalarGridSpec(
            num_scalar_prefetch=2,
            grid=(n_blk,),
            in_specs=[
                pl.BlockSpec((MOE_BLOCK, D_MODEL), lambda i, be, na: (blk(i, be, na), 0)),
                pl.BlockSpec((None, None, D_MODEL, D_EXPERT), lambda i, be, na: (layer, be[blk(i, be, na)], 0, 0)),
                pl.BlockSpec((None, None, D_MODEL, D_EXPERT), lambda i, be, na: (layer, be[blk(i, be, na)], 0, 0)),
                pl.BlockSpec((None, None, D_EXPERT, D_MODEL), lambda i, be, na: (layer, be[blk(i, be, na)], 0, 0)),
            ],
            out_specs=pl.BlockSpec((MOE_BLOCK, D_MODEL), lambda i, be, na: (i, 0)),
            scratch_shapes=[pltpu.VMEM((D_MODEL, D_EXPERT), BF16), pltpu.VMEM((D_MODEL, D_EXPERT), BF16),
                            pltpu.VMEM((D_EXPERT, D_MODEL), BF16)],
        ),
        out_shape=jax.ShapeDtypeStruct((cap, D_MODEL), F32),
        compiler_params=_cparams(("arbitrary",)),
        name="moe_experts",
    )(blk_e, nact, xs, w_gate, w_up, w_down)


def _combine_kernel(dest_ref, x1_ref, route_ref, y_hbm, o_ref, yb, sem, *, n, tc):
    t0 = pl.program_id(0) * tc

    def body(j, carry):
        pltpu.make_async_copy(y_hbm.at[pl.ds(dest_ref[t0 + j], 1)], yb.at[0, pl.ds(j, 1)], sem.at[0]).start()
        pltpu.make_async_copy(y_hbm.at[pl.ds(dest_ref[n + t0 + j], 1)], yb.at[1, pl.ds(j, 1)], sem.at[0]).start()
        return carry

    lax.fori_loop(0, tc, body, 0, unroll=8)
    for k in range(2):
        pltpu.make_async_copy(y_hbm.at[pl.ds(0, tc)], yb.at[k], sem.at[0]).wait()
    route = route_ref[...]
    o_ref[...] = x1_ref[...] + route[:, 2:3] * yb[0] + route[:, 3:4] * yb[1]


def _combine(dest_flat, x1, route, y, tc=256):
    n = x1.shape[0]
    return pl.pallas_call(
        functools.partial(_combine_kernel, n=n, tc=tc),
        grid_spec=pltpu.PrefetchScalarGridSpec(
            num_scalar_prefetch=1,
            grid=(n // tc,),
            in_specs=[pl.BlockSpec((tc, D_MODEL), lambda i, d: (i, 0)),
                      pl.BlockSpec((tc, LANES), lambda i, d: (i, 0)),
                      pl.BlockSpec(memory_space=pl.ANY)],
            out_specs=pl.BlockSpec((tc, D_MODEL), lambda i, d: (i, 0)),
            scratch_shapes=[pltpu.VMEM((2, tc, D_MODEL), F32), pltpu.SemaphoreType.DMA((1,))],
        ),
        out_shape=jax.ShapeDtypeStruct((n, D_MODEL), F32),
        compiler_params=_cparams(("arbitrary",)),
        name="moe_combine",
    )(dest_flat, x1, route, y)


def _rope_tables(seq):
    n_freq = HEAD_DIM // 4
    t = np.arange(seq)
    inv_freq = jnp.asarray(ROPE_THETA, F32) ** (-jnp.arange(n_freq, dtype=F32) / n_freq)
    pos = jnp.stack([jnp.asarray(t // GRID_W, F32), jnp.asarray(t % GRID_W, F32)], axis=1)
    ang = pos[:, :, None] * inv_freq
    cos = jnp.broadcast_to(jnp.cos(ang)[:, :, None, :], (seq, 2, 2, n_freq)).reshape(seq, HEAD_DIM)
    sin = jnp.broadcast_to(jnp.sin(ang)[:, :, None, :], (seq, 2, 2, n_freq)).reshape(seq, HEAD_DIM)
    first_half = (np.arange(HEAD_DIM) % (2 * n_freq)) < n_freq
    sa = jnp.where(first_half, -sin, 0.0)
    sb = jnp.where(first_half, 0.0, sin)
    rep = CHUNK // HEAD_DIM
    return jnp.tile(cos, (1, rep)), jnp.tile(sa, (1, rep)), jnp.tile(sb, (1, rep))


def _prep_w_in(w):
    base = w[:, :2048]
    kc0, kc1 = w[:, 2048:2112], w[:, 2112:2176]
    vc0, vc1 = w[:, 2176:2240], w[:, 2240:2304]
    return jnp.concatenate([base, kc0, kc0, kc1, kc1, vc0, vc0, vc1, vc1], axis=1).astype(BF16)


def _prep_gain(qk_g):
    qs = SCALE * LOG2E
    ones = jnp.ones((CHUNK,), F32)
    t4 = lambda g: jnp.tile(g, CHUNK // HEAD_DIM)
    return jnp.concatenate([
        t4(qk_g[0, 0]) * qs, t4(qk_g[0, 1]), ones,
        t4(qk_g[1, 0]) * qs, t4(qk_g[1, 1]), ones,
        t4(qk_g[2, 0]) * qs, t4(qk_g[2, 0]) * qs, t4(qk_g[2, 1]), ones,
    ])[None, :]


def _split_hi_lo(w):
    hi = w.astype(BF16)
    lo = (w - hi.astype(F32)).astype(BF16)
    return hi, lo


def _mixer_layer(x2, bsz, seq, tables, norm1_g, w_in, qk_g, rpb, mix_g, w_out, norm2_g, w_rg, b_rg, w_re, b_re):
    n, d = x2.shape
    cos_t, sa_t, sb_t, bd = tables
    proj, pb4, pb16 = _inproj(x2, norm1_g[None, :], _prep_w_in(w_in), _prep_gain(qk_g), bd, cos_t, sa_t, sb_t, seq)
    proj3 = proj.reshape(bsz, seq, PROJ_W)
    oa = _na_attention(proj3, _na_bias_table(rpb)).reshape(n, CHUNK)
    ob_parts = [_dilated_branch(proj3[:, None], CH_QB), _dilated_branch(pb4, 0), _dilated_branch(pb16, 0)]
    oc = _gqa_attention(proj3).reshape(n, 2 * CHUNK)

    w_r = jnp.zeros((d, LANES), F32)
    w_r = w_r.at[:, :N_GROUPS].set(w_rg).at[:, N_GROUPS:N_GROUPS + N_EXPERTS].set(w_re)
    b_r = jnp.zeros((1, LANES), F32)
    b_r = b_r.at[0, :N_GROUPS].set(b_rg).at[0, N_GROUPS:N_GROUPS + N_EXPERTS].set(b_re)
    wr_hi, wr_lo = _split_hi_lo(w_r)
    return _outproj(oa, ob_parts, oc, x2, w_out.astype(BF16), mix_g[None, :], norm2_g[None, :], wr_hi, wr_lo, b_r)


def _moe_layer(x1, h, route, w_gate, w_up, w_down, layer):
    n = x1.shape[0]
    cap = 2 * n + N_EXPERTS * MOE_BLOCK
    n_blk = cap // MOE_BLOCK
    dest, meta = _rank(route)
    dest_flat = dest[:, :2].T.reshape(2 * n)
    xs = _dispatch(dest_flat, h, cap)
    y = _experts(meta[:n_blk, 0], meta[META_ROWS - 1:, 0], xs, w_gate, w_up, w_down, layer)
    return _combine(dest_flat, x1, route, y)


def _tables(seq):
    hd = np.arange(CHUNK) // HEAD_DIM
    return (*_rope_tables(seq), jnp.asarray(hd[:, None] == hd[None, :], BF16))


@jax.jit
def kernel(x, norm1_g, w_in, qk_norm_g, na_rpb, mix_norm_g, w_out, norm2_g, w_router_group, b_router_group,
           w_router_expert, b_router_expert, w_gate, w_up, w_down):
    bsz, seq, d = x.shape
    tables = _tables(seq)
    x2 = x.reshape(bsz * seq, d)
    for l in range(DEPTH):
        x1, h, route = _mixer_layer(x2, bsz, seq, tables, norm1_g[l], w_in[l], qk_norm_g[l], na_rpb[l],
                                    mix_norm_g[l], w_out[l], norm2_g[l], w_router_group[l], b_router_group[l],
                                    w_router_expert[l], b_router_expert[l])
        x2 = _moe_layer(x1, h, route, w_gate, w_up, w_down, l)
    return x2.reshape(bsz, seq, d)
```

```python
import functools
import math

import numpy as np
import jax
import jax.numpy as jnp
from jax import lax
from jax.experimental import pallas as pl
from jax.experimental.pallas import tpu as pltpu

F32 = jnp.float32
BF16 = jnp.bfloat16

D_MODEL = 1024
DEPTH = 4
GRID_W = 64
HEAD_DIM = 64
RMS_EPS = 1e-6
NEG = -1e30
LOG2E = math.log2(math.e)
SCALE = HEAD_DIM ** -0.5

NA_ROWS = 8
NA_COLS = 16
DILATIONS = (1, 4, 16)
DIL_HALF = 64
ROPE_THETA = 10000.0

N_GROUPS = 4
EXPERTS_PER_GROUP = 8
N_EXPERTS = 32
D_EXPERT = 512
MOE_BLOCK = 256

LANES = 128
CHUNK = 256
CH_QA, CH_KA, CH_VA, CH_QB, CH_KB, CH_VB, CH_QC, CH_KC, CH_VC = 0, 1, 2, 3, 4, 5, 6, 8, 9
N_CHUNKS = 10
PROJ_W = N_CHUNKS * CHUNK
NORM_CHUNKS = (CH_QA, CH_KA, CH_QB, CH_KB, CH_QC, CH_QC + 1, CH_KC)
ROPE_CHUNKS = (CH_QC, CH_QC + 1, CH_KC)

VMEM_LIMIT = 56 * 1024 * 1024

NT_DIMS = (((1,), (1,)), ((), ()))


def _cparams(sem):
    return pltpu.CompilerParams(dimension_semantics=sem, vmem_limit_bytes=VMEM_LIMIT)


def _inproj_kernel(x_ref, g1_ref, w_ref, gain_ref, bd_ref, cos_ref, sa_ref, sb_ref, o_ref, o4_ref, o16_ref, stage_ref):
    tm = x_ref.shape[0]
    x = x_ref[...]
    ms = jnp.mean(x * x, axis=-1, keepdims=True)
    h = (x * lax.rsqrt(ms + RMS_EPS) * g1_ref[...]).astype(BF16)
    for c in range(N_CHUNKS):
        cols = slice(c * CHUNK, (c + 1) * CHUNK)
        p = jnp.dot(h, w_ref[:, cols], preferred_element_type=F32)
        if c in NORM_CHUNKS:
            ss = jnp.dot((p * p).astype(BF16), bd_ref[...], preferred_element_type=F32)
            p = p * lax.rsqrt(ss * (1.0 / HEAD_DIM) + RMS_EPS) * gain_ref[:, cols]
        if c in ROPE_CHUNKS:
            p = (p * cos_ref[...] + pltpu.roll(p, CHUNK - 16, 1) * sa_ref[...]
                 + pltpu.roll(p, 16, 1) * sb_ref[...])
        if c == CH_VC:
            lane = lax.broadcasted_iota(jnp.int32, p.shape, 1)
            p = jnp.where(lane % LANES >= HEAD_DIM, 1.0, p)
        o_ref[:, cols] = p.astype(BF16)
        if c in (CH_QB, CH_KB, CH_VB):
            for hf in range(CHUNK // LANES):
                stage_ref[hf] = p[:, hf * LANES:(hf + 1) * LANES]
                d0 = (c - CH_QB) * CHUNK + hf * LANES
                for dil, od_ref in ((4, o4_ref), (16, o16_ref)):
                    for r in range(dil):
                        rows = stage_ref[hf, pl.ds(r, tm // dil, stride=dil), :]
                        od_ref[0, r, :, d0:d0 + LANES] = rows.astype(BF16)


def _inproj(x2, g1, w, gain, bd, cos_t, sa_t, sb_t, seq, tm=512):
    n = x2.shape[0]
    nsb = seq // tm
    bsz = n // seq
    dil_spec = lambda dil: pl.BlockSpec((1, dil, tm // dil, 3 * CHUNK), lambda i: (i // nsb, 0, i % nsb, 0))
    dil_shape = lambda dil: jax.ShapeDtypeStruct((bsz, dil, seq // dil, 3 * CHUNK), BF16)
    return pl.pallas_call(
        _inproj_kernel,
        grid=(n // tm,),
        in_specs=[
            pl.BlockSpec((tm, D_MODEL), lambda i: (i, 0)),
            pl.BlockSpec((1, D_MODEL), lambda i: (0, 0)),
            pl.BlockSpec((D_MODEL, PROJ_W), lambda i: (0, 0)),
            pl.BlockSpec((1, PROJ_W), lambda i: (0, 0)),
            pl.BlockSpec((CHUNK, CHUNK), lambda i: (0, 0)),
            pl.BlockSpec((tm, CHUNK), lambda i: (i % nsb, 0)),
            pl.BlockSpec((tm, CHUNK), lambda i: (i % nsb, 0)),
            pl.BlockSpec((tm, CHUNK), lambda i: (i % nsb, 0)),
        ],
        out_specs=[pl.BlockSpec((tm, PROJ_W), lambda i: (i, 0)), dil_spec(4), dil_spec(16)],
        out_shape=[jax.ShapeDtypeStruct((n, PROJ_W), BF16), dil_shape(4), dil_shape(16)],
        scratch_shapes=[pltpu.VMEM((CHUNK // LANES, tm, LANES), F32)],
        compiler_params=_cparams(("arbitrary",)),
        name="inproj",
    )(x2, g1, w, gain, bd, cos_t, sa_t, sb_t)


def _stack_head_pair(qp, lo):
    zero = jnp.zeros_like(qp)
    return jnp.concatenate([jnp.where(lo, qp, zero), jnp.where(lo, zero, qp)], axis=0)


def _pack_bf16_pair(a, b):
    ua = lax.bitcast_convert_type(a.astype(BF16).astype(F32), jnp.uint32)
    ub = lax.bitcast_convert_type(b.astype(BF16).astype(F32), jnp.uint32)
    return ua | (ub >> 16)


def _unpack_bf16_pair(u):
    hi = lax.bitcast_convert_type(u & jnp.uint32(0xFFFF0000), F32)
    lo = lax.bitcast_convert_type(u << 16, F32)
    return hi, lo


def _lo_mask(rows):
    return lax.broadcasted_iota(jnp.int32, (rows, LANES), 1) < HEAD_DIM


def _na_kernel(q_ref, k_ref, v_ref, bias_ref, o_ref, *, rows_per_step, n_rows):
    i = pl.program_id(1)
    lo = _lo_mask(GRID_W)
    win = NA_ROWS * GRID_W
    for rl in range(rows_per_step):
        r = i * rows_per_step + rl
        r0 = jnp.clip(r - NA_ROWS // 2, 0, n_rows - NA_ROWS)
        rr = r - r0
        start = pl.multiple_of(r0 * GRID_W, GRID_W)
        qrows = slice(rl * GRID_W, (rl + 1) * GRID_W)
        for p in range(2):
            cols = slice(p * LANES, (p + 1) * LANES)
            qs = _stack_head_pair(q_ref[0, qrows, cols], lo)
            kw = k_ref[0, pl.ds(start, win), cols]
            vw = v_ref[0, pl.ds(start, win), cols]
            s = lax.dot_general(qs, kw, NT_DIMS, preferred_element_type=F32)
            s = s + bias_ref[p, rr]
            m = jnp.max(s, axis=-1, keepdims=True)
            e = jnp.exp2(s - m)
            l = jnp.sum(e, axis=-1, keepdims=True)
            pv = jnp.dot(e.astype(BF16), vw, preferred_element_type=F32)
            o = pv / l
            o_ref[0, qrows, cols] = jnp.where(lo, o[:GRID_W], o[GRID_W:]).astype(BF16)


def _na_attention(proj3, bias, rows_per_step=8):
    b, s, _ = proj3.shape
    n_rows = s // GRID_W
    tq = rows_per_step * GRID_W
    return pl.pallas_call(
        functools.partial(_na_kernel, rows_per_step=rows_per_step, n_rows=n_rows),
        grid=(b, n_rows // rows_per_step),
        in_specs=[
            pl.BlockSpec((1, tq, CHUNK), lambda bi, i: (bi, i, CH_QA)),
            pl.BlockSpec((1, s, CHUNK), lambda bi, i: (bi, 0, CH_KA)),
            pl.BlockSpec((1, s, CHUNK), lambda bi, i: (bi, 0, CH_VA)),
            pl.BlockSpec((2, NA_ROWS, 2 * GRID_W, NA_ROWS * GRID_W), lambda bi, i: (0, 0, 0, 0)),
        ],
        out_specs=pl.BlockSpec((1, tq, CHUNK), lambda bi, i: (bi, i, 0)),
        out_shape=jax.ShapeDtypeStruct((b, s, CHUNK), BF16),
        compiler_params=_cparams(("arbitrary", "arbitrary")),
        name="na_attention",
    )(proj3, proj3, proj3, bias)


def _na_bias_table(rpb):
    c = np.arange(GRID_W)
    cs = np.clip(c - NA_COLS // 2, 0, GRID_W - NA_COLS)
    kc = np.arange(GRID_W)
    inwin = (kc[None, :] >= cs[:, None]) & (kc[None, :] < cs[:, None] + NA_COLS)
    dc = kc[None, :] - c[:, None] + NA_COLS - 1
    rr = np.arange(NA_ROWS)
    dr = rr[None, :] - rr[:, None] + NA_ROWS - 1
    sel_r = jnp.asarray(dr[:, :, None] == np.arange(2 * NA_ROWS - 1), F32)
    sel_c = jnp.asarray((dc[:, :, None] == np.arange(2 * NA_COLS - 1)) & inwin[:, :, None], F32)
    tab = jnp.einsum("rka,hab,cjb->hrckj", sel_r, rpb, sel_c, precision=lax.Precision.HIGHEST)
    tab = jnp.where(inwin[None, None, :, None, :], tab * LOG2E, NEG)
    tab = tab.reshape(2, 2, NA_ROWS, GRID_W, NA_ROWS * GRID_W)
    return tab.transpose(0, 2, 1, 3, 4).reshape(2, NA_ROWS, 2 * GRID_W, NA_ROWS * GRID_W)


DIL_TQ = 128
DIL_WIN = DIL_TQ + 2 * DIL_HALF


def _dil_kernel(q_ref, k_ref, v_ref, bias_ref, o_ref, lw_ref, *, length):
    ub = pl.program_id(2)
    nb = pl.num_programs(2)
    ws = pl.multiple_of(jnp.clip(ub * DIL_TQ - DIL_HALF, 0, length - DIL_WIN), DIL_HALF)
    case = jnp.where(ub == 0, 0, jnp.where(ub == nb - 1, 2, 1))
    lo = _lo_mask(DIL_TQ)
    for p in range(2):
        cols = slice(p * LANES, (p + 1) * LANES)
        qs = _stack_head_pair(q_ref[:, cols], lo)
        kw = k_ref[pl.ds(ws, DIL_WIN), cols]
        vw = v_ref[pl.ds(ws, DIL_WIN), cols]
        s = lax.dot_general(qs, kw, NT_DIMS, preferred_element_type=F32)
        s = s + bias_ref[case, p]
        m = jnp.max(s, axis=-1, keepdims=True)
        e = jnp.exp2(s - m)
        l = jnp.sum(e, axis=-1, keepdims=True)
        pv = jnp.dot(e.astype(BF16), vw, preferred_element_type=F32)
        o = pv / l
        lw = jnp.broadcast_to(m + jnp.log2(l), (2 * DIL_TQ, LANES))
        o_ref[0, :, cols] = jnp.where(lo, o[:DIL_TQ], o[DIL_TQ:]).astype(BF16)
        lw_ref[0, :, cols] = jnp.where(lo, lw[:DIL_TQ], lw[DIL_TQ:])


def _dil_bias_table(dil):
    slopes = np.exp2(-8.0 * np.arange(1, 5) / 4.0)
    qi = np.arange(DIL_TQ)[:, None]
    kj = np.arange(DIL_WIN)[None, :]
    tab = np.zeros((3, 4, DIL_TQ, DIL_WIN), np.float32)
    for case, shift in enumerate((0, -DIL_HALF, -2 * DIL_HALF)):
        rel = kj + shift - qi
        for h in range(4):
            tab[case, h] = np.where(np.abs(rel) <= DIL_HALF, -slopes[h] * dil * np.abs(rel) * LOG2E, NEG)
    return tab.reshape(3, 2, 2 * DIL_TQ, DIL_WIN)


def _dilated_branch(src, q_chunk):
    b, dil, length, _ = src.shape
    nb = length // DIL_TQ
    o, lw = pl.pallas_call(
        functools.partial(_dil_kernel, length=length),
        grid=(b, dil, nb),
        in_specs=[
            pl.BlockSpec((None, None, DIL_TQ, CHUNK), lambda bi, r, u: (bi, r, u, q_chunk)),
            pl.BlockSpec((None, None, length, CHUNK), lambda bi, r, u: (bi, r, 0, q_chunk + 1)),
            pl.BlockSpec((None, None, length, CHUNK), lambda bi, r, u: (bi, r, 0, q_chunk + 2)),
            pl.BlockSpec((3, 2, 2 * DIL_TQ, DIL_WIN), lambda bi, r, u: (0, 0, 0, 0)),
        ],
        out_specs=[
            pl.BlockSpec((1, DIL_TQ, CHUNK), lambda bi, r, u: (bi, u, r)),
            pl.BlockSpec((1, DIL_TQ, CHUNK), lambda bi, r, u: (bi, u, r)),
        ],
        out_shape=[
            jax.ShapeDtypeStruct((b, length, dil * CHUNK), BF16),
            jax.ShapeDtypeStruct((b, length, dil * CHUNK), F32),
        ],
        compiler_params=_cparams(("arbitrary", "arbitrary", "arbitrary")),
        name=f"dilated_d{dil}",
    )(src, src, src, jnp.asarray(_dil_bias_table(dil)))
    n = b * dil * length
    return o.reshape(n, CHUNK), lw.reshape(n, CHUNK)


def _gqa_kernel(q_ref, k_ref, v_ref, o_ref, m_s, acc_s, s_s, *, tq, tk, seq):
    lo = _lo_mask(tq)
    n_kt = seq // tk
    qs = jnp.concatenate([_stack_head_pair(q_ref[0, :, j * LANES:(j + 1) * LANES], lo) for j in range(2)],
                         axis=0)
    m_s[...] = jnp.full(m_s.shape, NEG, F32)
    acc_s[...] = jnp.zeros(acc_s.shape, F32)

    def scores(kt, slot):
        k0 = pl.multiple_of(kt * tk, tk)
        s_s[slot] = lax.dot_general(qs, k_ref[0, pl.ds(k0, tk), :], NT_DIMS, preferred_element_type=F32)

    def softmax_values(kt, slot):
        k0 = pl.multiple_of(kt * tk, tk)
        s = s_s[slot]
        m_old = m_s[...]
        m_new = jnp.maximum(m_old, jnp.max(s, axis=-1, keepdims=True))
        alpha = jnp.exp2(m_old - m_new)
        e = jnp.exp2((s - jnp.concatenate([m_new] * (tk // LANES), axis=1)).astype(BF16))
        m_s[...] = m_new
        pv = jnp.dot(e, v_ref[0, pl.ds(k0, tk), :], preferred_element_type=F32)
        acc_s[...] = alpha * acc_s[...] + pv

    scores(0, 0)

    def body(j, carry):
        scores(2 * j + 1, 1)
        softmax_values(2 * j, 0)
        scores(2 * j + 2, 0)
        softmax_values(2 * j + 1, 1)
        return carry

    lax.fori_loop(0, n_kt // 2 - 1, body, 0, unroll=True)
    scores(n_kt - 1, 1)
    softmax_values(n_kt - 2, 0)
    softmax_values(n_kt - 1, 1)

    acc = acc_s[...]
    o = acc / pltpu.roll(acc, HEAD_DIM, 1)
    for j in range(2):
        odd = pltpu.roll(o[(2 * j + 1) * tq:(2 * j + 2) * tq], HEAD_DIM, 1)
        pair = jnp.where(lo, o[(2 * j) * tq:(2 * j + 1) * tq], odd)
        o_ref[0, :, j * LANES:(j + 1) * LANES] = pair.astype(BF16)


def _gqa_attention(proj3, tq=256, tk=512):
    b, s, _ = proj3.shape
    assert (s // tk) % 2 == 0 and s // tk >= 2
    return pl.pallas_call(
        functools.partial(_gqa_kernel, tq=tq, tk=tk, seq=s),
        grid=(b, 2, s // tq),
        in_specs=[
            pl.BlockSpec((1, tq, CHUNK), lambda bi, g, i: (bi, i, CH_QC + g)),
            pl.BlockSpec((1, s, LANES), lambda bi, g, i: (bi, 0, 2 * CH_KC + g)),
            pl.BlockSpec((1, s, LANES), lambda bi, g, i: (bi, 0, 2 * CH_VC + g)),
        ],
        out_specs=pl.BlockSpec((1, tq, CHUNK), lambda bi, g, i: (bi, i, g)),
        out_shape=jax.ShapeDtypeStruct((b, s, 2 * CHUNK), BF16),
        scratch_shapes=[
            pltpu.VMEM((4 * tq, LANES), F32),
            pltpu.VMEM((4 * tq, LANES), F32),
            pltpu.VMEM((2, 4 * tq, tk), F32),
        ],
        compiler_params=_cparams(("arbitrary", "arbitrary", "arbitrary")),
        name="gqa_attention",
    )(proj3, proj3, proj3)


def _rms(v, g):
    return v * lax.rsqrt(jnp.mean(v * v, axis=-1, keepdims=True) + RMS_EPS) * g


def _outproj_kernel(oa_ref, o1_ref, o4_ref, o16_ref, lw1_ref, lw4_ref, lw16_ref, oc_ref, x_ref,
                    w_ref, mg_ref, g2_ref, wrh_ref, wrl_ref, br_ref, x1_ref, h_ref, route_ref):
    lw1, lw4, lw16 = lw1_ref[...], lw4_ref[...], lw16_ref[...]
    lwm = jnp.maximum(jnp.maximum(lw1, lw4), lw16)
    w1, w4, w16 = jnp.exp2(lw1 - lwm), jnp.exp2(lw4 - lwm), jnp.exp2(lw16 - lwm)
    ob = (w1 * o1_ref[...].astype(F32) + w4 * o4_ref[...].astype(F32)
          + w16 * o16_ref[...].astype(F32)) / (w1 + w4 + w16)
    merged = jnp.concatenate([
        _rms(oa_ref[...].astype(F32), mg_ref[:, 0:CHUNK]).astype(BF16),
        _rms(ob, mg_ref[:, CHUNK:2 * CHUNK]).astype(BF16),
        _rms(oc_ref[...].astype(F32), mg_ref[:, 2 * CHUNK:]).astype(BF16),
    ], axis=-1)
    x1 = x_ref[...] + jnp.dot(merged, w_ref[...], preferred_element_type=F32)
    x1_ref[...] = x1
    h = _rms(x1, g2_ref[...])
    h_ref[...] = _pack_bf16_pair(h[:, :D_MODEL // 2], h[:, D_MODEL // 2:])

    hh = h.astype(BF16)
    hl = (h - hh.astype(F32)).astype(BF16)
    logits = (jnp.dot(hh, wrh_ref[...], preferred_element_type=F32)
              + jnp.dot(hl, wrh_ref[...], preferred_element_type=F32)
              + jnp.dot(hh, wrl_ref[...], preferred_element_type=F32)) + br_ref[...]
    lane = lax.broadcasted_iota(jnp.int32, logits.shape, 1)
    lane_f = lane.astype(F32)
    big = float(LANES)
    is_g = lane < N_GROUPS
    gl = jnp.where(is_g, logits, NEG)
    gmax = jnp.max(gl, axis=-1, keepdims=True)
    grp = jnp.min(jnp.where(gl == gmax, lane_f, big), axis=-1, keepdims=True)
    p_grp = 1.0 / jnp.sum(jnp.where(is_g, jnp.exp(logits - gmax), 0.0), axis=-1, keepdims=True)
    e_lo = N_GROUPS + EXPERTS_PER_GROUP * grp
    in_grp = (lane_f >= e_lo) & (lane_f < e_lo + EXPERTS_PER_GROUP)
    el = jnp.where(in_grp, logits, NEG)
    t1 = jnp.max(el, axis=-1, keepdims=True)
    i1 = jnp.min(jnp.where(el == t1, lane_f, big), axis=-1, keepdims=True)
    el2 = jnp.where(lane_f == i1, NEG, el)
    t2 = jnp.max(el2, axis=-1, keepdims=True)
    i2 = jnp.min(jnp.where(el2 == t2, lane_f, big), axis=-1, keepdims=True)
    e21 = jnp.exp(t2 - t1)
    gate1 = p_grp / (1.0 + e21)
    gate2 = p_grp * e21 / (1.0 + e21)
    route = jnp.where(lane == 0, i1 - N_GROUPS,
                      jnp.where(lane == 1, i2 - N_GROUPS,
                                jnp.where(lane == 2, gate1, jnp.where(lane == 3, gate2, 0.0))))
    route_ref[...] = route


def _outproj(oa, ob_parts, oc, x2, w_out, mix_g, g2, wr_hi, wr_lo, b_r, tm=512):
    n = x2.shape[0]
    (o1, lw1), (o4, lw4), (o16, lw16) = ob_parts
    row = lambda w: pl.BlockSpec((tm, w), lambda i: (i, 0))
    full = lambda r, w: pl.BlockSpec((r, w), lambda i: (0, 0))
    return pl.pallas_call(
        _outproj_kernel,
        grid=(n // tm,),
        in_specs=[row(CHUNK), row(CHUNK), row(CHUNK), row(CHUNK), row(CHUNK), row(CHUNK), row(CHUNK),
                  row(2 * CHUNK), row(D_MODEL),
                  full(D_MODEL, D_MODEL), full(1, D_MODEL), full(1, D_MODEL),
                  full(D_MODEL, LANES), full(D_MODEL, LANES), full(1, LANES)],
        out_specs=[row(D_MODEL), row(D_MODEL // 2), row(LANES)],
        out_shape=[jax.ShapeDtypeStruct((n, D_MODEL), F32),
                   jax.ShapeDtypeStruct((n, D_MODEL // 2), jnp.uint32),
                   jax.ShapeDtypeStruct((n, LANES), F32)],
        compiler_params=_cparams(("arbitrary",)),
        name="outproj_router",
    )(oa, o1, o4, o16, lw1, lw4, lw16, oc, x2, w_out, mix_g, g2, wr_hi, wr_lo, b_r)


META_ROWS = 384


def _rank_kernel(route_ref, dest_ref, meta_ref, cnt_s, base_s, tri_s, *, tm):
    phase = pl.program_id(0)
    i = pl.program_id(1)
    lane = lax.broadcasted_iota(jnp.int32, (tm, LANES), 1)
    lane_f = lane.astype(F32)
    route = route_ref[...]
    oh0 = jnp.where(lane_f == route[:, 0:1], 1.0, 0.0)
    oh1 = jnp.where(lane_f == route[:, 1:2], 1.0, 0.0)
    oh = oh0 + oh1
    colsum = jnp.sum(oh, axis=0, keepdims=True)

    @pl.when((phase == 0) & (i == 0))
    def _():
        cnt_s[...] = jnp.zeros(cnt_s.shape, F32)
        r = lax.broadcasted_iota(jnp.int32, (tm, tm), 0)
        c = lax.broadcasted_iota(jnp.int32, (tm, tm), 1)
        tri_s[...] = jnp.where(c < r, 1.0, 0.0).astype(BF16)

    @pl.when(phase == 0)
    def _():
        cnt_s[...] = cnt_s[...] + colsum

    @pl.when((phase == 1) & (i == 0))
    def _():
        lane8 = lax.broadcasted_iota(jnp.int32, cnt_s.shape, 1)
        padded = jnp.floor((cnt_s[...] + (MOE_BLOCK - 1)) * (1.0 / MOE_BLOCK)) * MOE_BLOCK
        ends = padded
        for sh in (1, 2, 4, 8, 16):
            ends = ends + jnp.where(lane8 >= sh, pltpu.roll(ends, sh, 1), 0.0)
        base_s[...] = ends - padded
        pos = lax.broadcasted_iota(jnp.int32, (META_ROWS, LANES), 0).astype(F32) * MOE_BLOCK
        lane_m = lax.broadcasted_iota(jnp.int32, (META_ROWS, LANES), 1)
        ends_row = ends[0:1, :]
        hit = jnp.where((lane_m < N_EXPERTS) & (ends_row <= pos), 1.0, 0.0)
        blk_e = jnp.minimum(jnp.sum(hit, axis=-1, keepdims=True), N_EXPERTS - 1.0)
        total = jnp.sum(jnp.where(lane_m[0:1] == N_EXPERTS - 1, ends_row, 0.0), axis=-1, keepdims=True)
        row_m = lax.broadcasted_iota(jnp.int32, (META_ROWS, 1), 0)
        meta = jnp.where(row_m == META_ROWS - 1, total * (1.0 / MOE_BLOCK), blk_e)
        meta_ref[...] = meta.astype(jnp.int32)

    @pl.when(phase == 1)
    def _():
        prior = jnp.dot(tri_s[...], oh.astype(BF16), preferred_element_type=F32)
        val = base_s[0:1, :] + prior
        d0 = jnp.sum(oh0 * val, axis=-1, keepdims=True)
        d1 = jnp.sum(oh1 * val, axis=-1, keepdims=True)
        dest_ref[...] = jnp.where(lane == 0, d0, jnp.where(lane == 1, d1, 0.0)).astype(jnp.int32)
        base_s[...] = base_s[...] + colsum


def _rank(route, tm=512):
    n = route.shape[0]
    return pl.pallas_call(
        functools.partial(_rank_kernel, tm=tm),
        grid=(2, n // tm),
        in_specs=[pl.BlockSpec((tm, LANES), lambda p, i: (i, 0))],
        out_specs=[pl.BlockSpec((tm, LANES), lambda p, i: (i * p, 0)),
                   pl.BlockSpec((META_ROWS, 1), lambda p, i: (0, 0))],
        out_shape=[jax.ShapeDtypeStruct((n, LANES), jnp.int32),
                   jax.ShapeDtypeStruct((META_ROWS, 1), jnp.int32)],
        scratch_shapes=[pltpu.VMEM((8, LANES), F32), pltpu.VMEM((8, LANES), F32), pltpu.VMEM((tm, tm), BF16)],
        compiler_params=_cparams(("arbitrary", "arbitrary")),
        name="moe_rank",
    )(route)


def _dispatch_kernel(dest_ref, h_ref, xs_in_hbm, xs_hbm, sem, *, n, tc):
    del xs_in_hbm
    t0 = pl.program_id(0) * tc

    def body(j, carry):
        src = h_ref.at[pl.ds(j, 1)]
        pltpu.make_async_copy(src, xs_hbm.at[pl.ds(dest_ref[t0 + j], 1)], sem.at[0]).start()
        pltpu.make_async_copy(src, xs_hbm.at[pl.ds(dest_ref[n + t0 + j], 1)], sem.at[0]).start()
        return carry

    lax.fori_loop(0, tc, body, 0, unroll=8)
    for _ in range(2):
        pltpu.make_async_copy(h_ref, xs_hbm.at[pl.ds(0, tc)], sem.at[0]).wait()


def _dispatch(dest_flat, h, cap, tc=256):
    n, w = h.shape
    zeros = jnp.zeros((cap, w), h.dtype)
    return pl.pallas_call(
        functools.partial(_dispatch_kernel, n=n, tc=tc),
        grid_spec=pltpu.PrefetchScalarGridSpec(
            num_scalar_prefetch=1,
            grid=(n // tc,),
            in_specs=[pl.BlockSpec((tc, w), lambda i, d: (i, 0)), pl.BlockSpec(memory_space=pl.ANY)],
            out_specs=pl.BlockSpec(memory_space=pl.ANY),
            scratch_shapes=[pltpu.SemaphoreType.DMA((1,))],
        ),
        out_shape=jax.ShapeDtypeStruct((cap, w), h.dtype),
        input_output_aliases={2: 0},
        compiler_params=_cparams(("arbitrary",)),
        name="moe_dispatch",
    )(dest_flat, h, zeros)


def _expert_kernel(blk_e_ref, nact_ref, xs_ref, wg_ref, wu_ref, wd_ref, y_ref, wg_s, wu_s, wd_s):
    i = pl.program_id(0)
    active = i < nact_ref[0]
    changed = (i == 0) | (blk_e_ref[i] != blk_e_ref[jnp.maximum(i - 1, 0)])

    @pl.when(active & changed)
    def _():
        wg_s[...] = wg_ref[...].astype(BF16)
        wu_s[...] = wu_ref[...].astype(BF16)
        wd_s[...] = wd_ref[...].astype(BF16)

    @pl.when(active)
    def _():
        hi, lo = _unpack_bf16_pair(xs_ref[...])
        xb = jnp.concatenate([hi.astype(BF16), lo.astype(BF16)], axis=1)
        g = jnp.dot(xb, wg_s[...], preferred_element_type=F32)
        u = jnp.dot(xb, wu_s[...], preferred_element_type=F32)
        a = (g * jax.nn.sigmoid(g) * u).astype(BF16)
        y = jnp.dot(a, wd_s[...], preferred_element_type=F32)
        y_ref[...] = _pack_bf16_pair(y[:, :D_MODEL // 2], y[:, D_MODEL // 2:])

    @pl.when(jnp.logical_not(active))
    def _():
        y_ref[...] = jnp.zeros(y_ref.shape, y_ref.dtype)


def _experts(blk_e, nact, xs, w_gate, w_up, w_down, layer):
    cap = xs.shape[0]
    n_blk = cap // MOE_BLOCK

    def blk(i, be, na):
        return jnp.minimum(i, na[0] - 1)

    return pl.pallas_call(
        _expert_kernel,
        grid_spec=pltpu.PrefetchScalarGridSpec(
            num_scalar_prefetch=2,
            grid=(n_blk,),
            in_specs=[
                pl.BlockSpec((MOE_BLOCK, D_MODEL // 2), lambda i, be, na: (blk(i, be, na), 0)),
                pl.BlockSpec((None, None, D_MODEL, D_EXPERT), lambda i, be, na: (layer, be[blk(i, be, na)], 0, 0)),
                pl.BlockSpec((None, None, D_MODEL, D_EXPERT), lambda i, be, na: (layer, be[blk(i, be, na)], 0, 0)),
                pl.BlockSpec((None, None, D_EXPERT, D_MODEL), lambda i, be, na: (layer, be[blk(i, be, na)], 0, 0)),
            ],
            out_specs=pl.BlockSpec((MOE_BLOCK, D_MODEL // 2), lambda i, be, na: (i, 0)),
            scratch_shapes=[pltpu.VMEM((D_MODEL, D_EXPERT), BF16), pltpu.VMEM((D_MODEL, D_EXPERT), BF16),
                            pltpu.VMEM((D_EXPERT, D_MODEL), BF16)],
        ),
        out_shape=jax.ShapeDtypeStruct((cap, D_MODEL // 2), jnp.uint32),
        compiler_params=_cparams(("arbitrary",)),
        name="moe_experts",
    )(blk_e, nact, xs, w_gate, w_up, w_down)


def _combine_kernel(dest_ref, x1_ref, route_ref, y_hbm, o_ref, yb, sem, *, n, tc):
    t0 = pl.program_id(0) * tc

    def body(j, carry):
        pltpu.make_async_copy(y_hbm.at[pl.ds(dest_ref[t0 + j], 1)], yb.at[0, pl.ds(j, 1)], sem.at[0]).start()
        pltpu.make_async_copy(y_hbm.at[pl.ds(dest_ref[n + t0 + j], 1)], yb.at[1, pl.ds(j, 1)], sem.at[0]).start()
        return carry

    lax.fori_loop(0, tc, body, 0, unroll=8)
    for k in range(2):
        pltpu.make_async_copy(y_hbm.at[pl.ds(0, tc)], yb.at[k], sem.at[0]).wait()
    route = route_ref[...]
    hi0, lo0 = _unpack_bf16_pair(yb[0])
    hi1, lo1 = _unpack_bf16_pair(yb[1])
    half = D_MODEL // 2
    o_ref[:, :half] = x1_ref[:, :half] + route[:, 2:3] * hi0 + route[:, 3:4] * hi1
    o_ref[:, half:] = x1_ref[:, half:] + route[:, 2:3] * lo0 + route[:, 3:4] * lo1


def _combine(dest_flat, x1, route, y, tc=256):
    n = x1.shape[0]
    return pl.pallas_call(
        functools.partial(_combine_kernel, n=n, tc=tc),
        grid_spec=pltpu.PrefetchScalarGridSpec(
            num_scalar_prefetch=1,
            grid=(n // tc,),
            in_specs=[pl.BlockSpec((tc, D_MODEL), lambda i, d: (i, 0)),
                      pl.BlockSpec((tc, LANES), lambda i, d: (i, 0)),
                      pl.BlockSpec(memory_space=pl.ANY)],
            out_specs=pl.BlockSpec((tc, D_MODEL), lambda i, d: (i, 0)),
            scratch_shapes=[pltpu.VMEM((2, tc, D_MODEL // 2), jnp.uint32), pltpu.SemaphoreType.DMA((1,))],
        ),
        out_shape=jax.ShapeDtypeStruct((n, D_MODEL), F32),
        compiler_params=_cparams(("arbitrary",)),
        name="moe_combine",
    )(dest_flat, x1, route, y)


def _rope_tables(seq):
    n_freq = HEAD_DIM // 4
    t = np.arange(seq)
    inv_freq = jnp.asarray(ROPE_THETA, F32) ** (-jnp.arange(n_freq, dtype=F32) / n_freq)
    pos = jnp.stack([jnp.asarray(t // GRID_W, F32), jnp.asarray(t % GRID_W, F32)], axis=1)
    ang = pos[:, :, None] * inv_freq
    cos = jnp.broadcast_to(jnp.cos(ang)[:, :, None, :], (seq, 2, 2, n_freq)).reshape(seq, HEAD_DIM)
    sin = jnp.broadcast_to(jnp.sin(ang)[:, :, None, :], (seq, 2, 2, n_freq)).reshape(seq, HEAD_DIM)
    first_half = (np.arange(HEAD_DIM) % (2 * n_freq)) < n_freq
    sa = jnp.where(first_half, -sin, 0.0)
    sb = jnp.where(first_half, 0.0, sin)
    rep = CHUNK // HEAD_DIM
    return jnp.tile(cos, (1, rep)), jnp.tile(sa, (1, rep)), jnp.tile(sb, (1, rep))


def _prep_w_in(w):
    base = w[:, :2048]
    kc0, kc1 = w[:, 2048:2112], w[:, 2112:2176]
    vc0, vc1 = w[:, 2176:2240], w[:, 2240:2304]
    z = jnp.zeros_like(vc0)
    return jnp.concatenate([base, kc0, kc0, kc1, kc1, vc0, z, vc1, z], axis=1).astype(BF16)


def _prep_gain(qk_g):
    qs = SCALE * LOG2E
    ones = jnp.ones((CHUNK,), F32)
    t4 = lambda g: jnp.tile(g, CHUNK // HEAD_DIM)
    return jnp.concatenate([
        t4(qk_g[0, 0]) * qs, t4(qk_g[0, 1]), ones,
        t4(qk_g[1, 0]) * qs, t4(qk_g[1, 1]), ones,
        t4(qk_g[2, 0]) * qs, t4(qk_g[2, 0]) * qs, t4(qk_g[2, 1]), ones,
    ])[None, :]


def _split_hi_lo(w):
    hi = w.astype(BF16)
    lo = (w - hi.astype(F32)).astype(BF16)
    return hi, lo


def _mixer_layer(x2, bsz, seq, tables, norm1_g, w_in, qk_g, rpb, mix_g, w_out, norm2_g, w_rg, b_rg, w_re, b_re):
    n, d = x2.shape
    cos_t, sa_t, sb_t, bd = tables
    proj, pb4, pb16 = _inproj(x2, norm1_g[None, :], _prep_w_in(w_in), _prep_gain(qk_g), bd, cos_t, sa_t, sb_t, seq)
    proj3 = proj.reshape(bsz, seq, PROJ_W)
    oa = _na_attention(proj3, _na_bias_table(rpb)).reshape(n, CHUNK)
    ob_parts = [_dilated_branch(proj3[:, None], CH_QB), _dilated_branch(pb4, 0), _dilated_branch(pb16, 0)]
    oc = _gqa_attention(proj3).reshape(n, 2 * CHUNK)

    w_r = jnp.zeros((d, LANES), F32)
    w_r = w_r.at[:, :N_GROUPS].set(w_rg).at[:, N_GROUPS:N_GROUPS + N_EXPERTS].set(w_re)
    b_r = jnp.zeros((1, LANES), F32)
    b_r = b_r.at[0, :N_GROUPS].set(b_rg).at[0, N_GROUPS:N_GROUPS + N_EXPERTS].set(b_re)
    wr_hi, wr_lo = _split_hi_lo(w_r)
    return _outproj(oa, ob_parts, oc, x2, w_out.astype(BF16), mix_g[None, :], norm2_g[None, :], wr_hi, wr_lo, b_r)


def _moe_layer(x1, h, route, w_gate, w_up, w_down, layer):
    n = x1.shape[0]
    cap = 2 * n + N_EXPERTS * MOE_BLOCK
    n_blk = cap // MOE_BLOCK
    dest, meta = _rank(route)
    dest_flat = dest[:, :2].T.reshape(2 * n)
    xs = _dispatch(dest_flat, h, cap)
    y = _experts(meta[:n_blk, 0], meta[META_ROWS - 1:, 0], xs, w_gate, w_up, w_down, layer)
    return _combine(dest_flat, x1, route, y)


def _tables(seq):
    hd = np.arange(CHUNK) // HEAD_DIM
    return (*_rope_tables(seq), jnp.asarray(hd[:, None] == hd[None, :], BF16))


@jax.jit
def kernel(x, norm1_g, w_in, qk_norm_g, na_rpb, mix_norm_g, w_out, norm2_g, w_router_group, b_router_group,
           w_router_expert, b_router_expert, w_gate, w_up, w_down):
    bsz, seq, d = x.shape
    tables = _tables(seq)
    x2 = x.reshape(bsz * seq, d)
    for l in range(DEPTH):
        x1, h, route = _mixer_layer(x2, bsz, seq, tables, norm1_g[l], w_in[l], qk_norm_g[l], na_rpb[l],
                                    mix_norm_g[l], w_out[l], norm2_g[l], w_router_group[l], b_router_group[l],
                                    w_router_expert[l], b_router_expert[l])
        x2 = _moe_layer(x1, h, route, w_gate, w_up, w_down, l)
    return x2.reshape(bsz, seq, d)
```

```python
import functools
import math

import numpy as np
import jax
import jax.numpy as jnp
from jax import lax
from jax.experimental import pallas as pl
from jax.experimental.pallas import tpu as pltpu

F32 = jnp.float32
BF16 = jnp.bfloat16

D_MODEL = 1024
DEPTH = 4
GRID_W = 64
HEAD_DIM = 64
RMS_EPS = 1e-6
NEG = -1e30
LOG2E = math.log2(math.e)
SCALE = HEAD_DIM ** -0.5

NA_ROWS = 8
NA_COLS = 16
DILATIONS = (1, 4, 16)
DIL_HALF = 64
ROPE_THETA = 10000.0

N_GROUPS = 4
EXPERTS_PER_GROUP = 8
N_EXPERTS = 32
D_EXPERT = 512
MOE_BLOCK = 256

LANES = 128
CHUNK = 256
CH_QA, CH_KA, CH_VA, CH_QB, CH_KB, CH_VB, CH_QC, CH_KC, CH_VC = 0, 1, 2, 3, 4, 5, 6, 8, 9
N_CHUNKS = 10
PROJ_W = N_CHUNKS * CHUNK
NORM_CHUNKS = (CH_QA, CH_KA, CH_QB, CH_KB, CH_QC, CH_QC + 1, CH_KC)
ROPE_CHUNKS = (CH_QC, CH_QC + 1, CH_KC)

VMEM_LIMIT = 56 * 1024 * 1024

NT_DIMS = (((1,), (1,)), ((), ()))


def _cparams(sem):
    return pltpu.CompilerParams(dimension_semantics=sem, vmem_limit_bytes=VMEM_LIMIT)


INPROJ_PARTS = 2


def _inproj_kernel(x_ref, g1_ref, w_ref, gain_ref, bd_ref, cos_ref, sa_ref, sb_ref, o_ref, o4_ref, o16_ref, stage_ref):
    tm = x_ref.shape[0]
    tp = tm // INPROJ_PARTS
    part_rows = [slice(i * tp, (i + 1) * tp) for i in range(INPROJ_PARTS)]
    chunk_cols = [slice(c * CHUNK, (c + 1) * CHUNK) for c in range(N_CHUNKS)]
    n_stage = 3 * CHUNK // LANES

    normed = []
    for rows in part_rows:
        x = x_ref[rows, :]
        ms = jnp.mean(x * x, axis=-1, keepdims=True)
        normed.append((x * lax.rsqrt(ms + RMS_EPS) * g1_ref[...]).astype(BF16))
    projs = [jnp.dot(h, w_ref[...], preferred_element_type=F32) for h in normed]
    for part, (rows, proj) in enumerate(zip(part_rows, projs)):
        sumsq = {c: jnp.dot(jnp.square(proj[:, chunk_cols[c]]).astype(BF16), bd_ref[...],
                            preferred_element_type=F32) for c in NORM_CHUNKS}
        for c in range(N_CHUNKS):
            cols = chunk_cols[c]
            p = proj[:, cols]
            if c in NORM_CHUNKS:
                p = p * lax.rsqrt(sumsq[c] * (1.0 / HEAD_DIM) + RMS_EPS) * gain_ref[:, cols]
            if c in ROPE_CHUNKS:
                p = (p * cos_ref[rows, :] + pltpu.roll(p, CHUNK - 16, 1) * sa_ref[rows, :]
                     + pltpu.roll(p, 16, 1) * sb_ref[rows, :])
            if c == CH_VC:
                lane = lax.broadcasted_iota(jnp.int32, p.shape, 1)
                p = jnp.where(lane % LANES >= HEAD_DIM, 1.0, p)
            o_ref[rows, cols] = p.astype(BF16)
            if c in (CH_QB, CH_KB, CH_VB):
                for hf in range(CHUNK // LANES):
                    lane0 = ((c - CH_QB) * (CHUNK // LANES) + hf) * LANES
                    slot = part * n_stage + lane0 // LANES
                    stage_ref[slot] = p[:, hf * LANES:(hf + 1) * LANES]
                    for dil, od_ref in ((4, o4_ref), (16, o16_ref)):
                        nr = tp // dil
                        for r in range(dil):
                            picked = stage_ref[slot, pl.ds(r, nr, stride=dil), :]
                            od_ref[0, r, part * nr:(part + 1) * nr, lane0:lane0 + LANES] = picked.astype(BF16)


def _inproj(x2, g1, w, gain, bd, cos_t, sa_t, sb_t, seq, tm=512):
    n = x2.shape[0]
    nsb = seq // tm
    bsz = n // seq
    dil_spec = lambda dil: pl.BlockSpec((1, dil, tm // dil, 3 * CHUNK), lambda i: (i // nsb, 0, i % nsb, 0))
    dil_shape = lambda dil: jax.ShapeDtypeStruct((bsz, dil, seq // dil, 3 * CHUNK), BF16)
    return pl.pallas_call(
        _inproj_kernel,
        grid=(n // tm,),
        in_specs=[
            pl.BlockSpec((tm, D_MODEL), lambda i: (i, 0)),
            pl.BlockSpec((1, D_MODEL), lambda i: (0, 0)),
            pl.BlockSpec((D_MODEL, PROJ_W), lambda i: (0, 0)),
            pl.BlockSpec((1, PROJ_W), lambda i: (0, 0)),
            pl.BlockSpec((CHUNK, CHUNK), lambda i: (0, 0)),
            pl.BlockSpec((tm, CHUNK), lambda i: (i % nsb, 0)),
            pl.BlockSpec((tm, CHUNK), lambda i: (i % nsb, 0)),
            pl.BlockSpec((tm, CHUNK), lambda i: (i % nsb, 0)),
        ],
        out_specs=[pl.BlockSpec((tm, PROJ_W), lambda i: (i, 0)), dil_spec(4), dil_spec(16)],
        out_shape=[jax.ShapeDtypeStruct((n, PROJ_W), BF16), dil_shape(4), dil_shape(16)],
        scratch_shapes=[pltpu.VMEM((INPROJ_PARTS * 3 * CHUNK // LANES, tm // INPROJ_PARTS, LANES), F32)],
        compiler_params=_cparams(("arbitrary",)),
        name="inproj",
    )(x2, g1, w, gain, bd, cos_t, sa_t, sb_t)


def _stack_head_pair(qp, lo):
    zero = jnp.zeros_like(qp)
    return jnp.concatenate([jnp.where(lo, qp, zero), jnp.where(lo, zero, qp)], axis=0)


def _pack_bf16_pair(a, b):
    ua = lax.bitcast_convert_type(a.astype(BF16).astype(F32), jnp.uint32)
    ub = lax.bitcast_convert_type(b.astype(BF16).astype(F32), jnp.uint32)
    return ua | (ub >> 16)


def _unpack_bf16_pair(u):
    hi = lax.bitcast_convert_type(u & jnp.uint32(0xFFFF0000), F32)
    lo = lax.bitcast_convert_type(u << 16, F32)
    return hi, lo


def _lo_mask(rows):
    return lax.broadcasted_iota(jnp.int32, (rows, LANES), 1) < HEAD_DIM


NA_ROW_GROUP = 4


def _na_kernel(q_ref, k_ref, v_ref, bias_ref, o_ref, *, rows_per_step, n_rows):
    i = pl.program_id(1)
    lo = _lo_mask(GRID_W)
    win = NA_ROWS * GRID_W

    def window(rl):
        r = i * rows_per_step + rl
        r0 = jnp.clip(r - NA_ROWS // 2, 0, n_rows - NA_ROWS)
        return pl.multiple_of(r0 * GRID_W, GRID_W), r - r0

    for g0 in range(0, rows_per_step, NA_ROW_GROUP):
        chains = [(rl, p) for rl in range(g0, g0 + NA_ROW_GROUP) for p in range(2)]
        scores = []
        for rl, p in chains:
            start, rr = window(rl)
            cols = slice(p * LANES, (p + 1) * LANES)
            qs = _stack_head_pair(q_ref[0, rl * GRID_W:(rl + 1) * GRID_W, cols], lo)
            s = lax.dot_general(qs, k_ref[0, pl.ds(start, win), cols], NT_DIMS, preferred_element_type=F32)
            scores.append(s + bias_ref[p, rr])
        stats = []
        for s in scores:
            e = jnp.exp2(s - jnp.max(s, axis=-1, keepdims=True))
            stats.append((e, jnp.sum(e, axis=-1, keepdims=True)))
        for (rl, p), (e, l) in zip(chains, stats):
            start, _ = window(rl)
            cols = slice(p * LANES, (p + 1) * LANES)
            pv = jnp.dot(e.astype(BF16), v_ref[0, pl.ds(start, win), cols], preferred_element_type=F32)
            o = pv / l
            o_ref[0, rl * GRID_W:(rl + 1) * GRID_W, cols] = jnp.where(lo, o[:GRID_W], o[GRID_W:]).astype(BF16)


def _na_attention(proj3, bias, rows_per_step=8):
    b, s, _ = proj3.shape
    n_rows = s // GRID_W
    tq = rows_per_step * GRID_W
    return pl.pallas_call(
        functools.partial(_na_kernel, rows_per_step=rows_per_step, n_rows=n_rows),
        grid=(b, n_rows // rows_per_step),
        in_specs=[
            pl.BlockSpec((1, tq, CHUNK), lambda bi, i: (bi, i, CH_QA)),
            pl.BlockSpec((1, s, CHUNK), lambda bi, i: (bi, 0, CH_KA)),
            pl.BlockSpec((1, s, CHUNK), lambda bi, i: (bi, 0, CH_VA)),
            pl.BlockSpec((2, NA_ROWS, 2 * GRID_W, NA_ROWS * GRID_W), lambda bi, i: (0, 0, 0, 0)),
        ],
        out_specs=pl.BlockSpec((1, tq, CHUNK), lambda bi, i: (bi, i, 0)),
        out_shape=jax.ShapeDtypeStruct((b, s, CHUNK), BF16),
        compiler_params=_cparams(("arbitrary", "arbitrary")),
        name="na_attention",
    )(proj3, proj3, proj3, bias)


def _na_bias_table(rpb):
    c = np.arange(GRID_W)
    cs = np.clip(c - NA_COLS // 2, 0, GRID_W - NA_COLS)
    kc = np.arange(GRID_W)
    inwin = (kc[None, :] >= cs[:, None]) & (kc[None, :] < cs[:, None] + NA_COLS)
    dc = kc[None, :] - c[:, None] + NA_COLS - 1
    rr = np.arange(NA_ROWS)
    dr = rr[None, :] - rr[:, None] + NA_ROWS - 1
    sel_r = jnp.asarray(dr[:, :, None] == np.arange(2 * NA_ROWS - 1), F32)
    sel_c = jnp.asarray((dc[:, :, None] == np.arange(2 * NA_COLS - 1)) & inwin[:, :, None], F32)
    tab = jnp.einsum("rka,hab,cjb->hrckj", sel_r, rpb, sel_c, precision=lax.Precision.HIGHEST)
    tab = jnp.where(inwin[None, None, :, None, :], tab * LOG2E, NEG)
    tab = tab.reshape(2, 2, NA_ROWS, GRID_W, NA_ROWS * GRID_W)
    return tab.transpose(0, 2, 1, 3, 4).reshape(2, NA_ROWS, 2 * GRID_W, NA_ROWS * GRID_W)


DIL_TQ = 128
DIL_WIN = DIL_TQ + 2 * DIL_HALF


def _dil_kernel(q_ref, k_ref, v_ref, bias_ref, o_ref, lw_ref, *, length, rg, ug, nb):
    u0 = pl.program_id(2) * ug
    lo = _lo_mask(DIL_TQ)
    chains = [(r, ub, p) for r in range(rg) for ub in range(ug) for p in range(2)]

    def window(ub):
        gb = u0 + ub
        ws = pl.multiple_of(jnp.clip(gb * DIL_TQ - DIL_HALF, 0, length - DIL_WIN), DIL_HALF)
        case = jnp.where(gb == 0, 0, jnp.where(gb == nb - 1, 2, 1))
        return ws, case

    scores = []
    for r, ub, p in chains:
        ws, case = window(ub)
        cols = slice(p * LANES, (p + 1) * LANES)
        qs = _stack_head_pair(q_ref[r, ub * DIL_TQ:(ub + 1) * DIL_TQ, cols], lo)
        s = lax.dot_general(qs, k_ref[r, pl.ds(ws, DIL_WIN), cols], NT_DIMS, preferred_element_type=F32)
        scores.append(s + bias_ref[case, p])
    stats = []
    for s in scores:
        m = jnp.max(s, axis=-1, keepdims=True)
        e = jnp.exp2(s - m)
        stats.append((m, e, jnp.sum(e, axis=-1, keepdims=True)))
    for (r, ub, p), (m, e, l) in zip(chains, stats):
        ws, _ = window(ub)
        cols = slice(p * LANES, (p + 1) * LANES)
        pv = jnp.dot(e.astype(BF16), v_ref[r, pl.ds(ws, DIL_WIN), cols], preferred_element_type=F32)
        o = pv / l
        lw = jnp.broadcast_to(m + jnp.log2(l), (2 * DIL_TQ, LANES))
        rows = slice(ub * DIL_TQ, (ub + 1) * DIL_TQ)
        ocols = slice(r * CHUNK + p * LANES, r * CHUNK + (p + 1) * LANES)
        o_ref[0, rows, ocols] = jnp.where(lo, o[:DIL_TQ], o[DIL_TQ:]).astype(BF16)
        lw_ref[0, rows, ocols] = jnp.where(lo, lw[:DIL_TQ], lw[DIL_TQ:])


def _dil_bias_table(dil):
    slopes = np.exp2(-8.0 * np.arange(1, 5) / 4.0)
    qi = np.arange(DIL_TQ)[:, None]
    kj = np.arange(DIL_WIN)[None, :]
    tab = np.zeros((3, 4, DIL_TQ, DIL_WIN), np.float32)
    for case, shift in enumerate((0, -DIL_HALF, -2 * DIL_HALF)):
        rel = kj + shift - qi
        for h in range(4):
            tab[case, h] = np.where(np.abs(rel) <= DIL_HALF, -slopes[h] * dil * np.abs(rel) * LOG2E, NEG)
    return tab.reshape(3, 2, 2 * DIL_TQ, DIL_WIN)


def _dilated_branch(src, q_chunk, rg, ug):
    b, dil, length, _ = src.shape
    nb = length // DIL_TQ
    o, lw = pl.pallas_call(
        functools.partial(_dil_kernel, length=length, rg=rg, ug=ug, nb=nb),
        grid=(b, dil // rg, nb // ug),
        in_specs=[
            pl.BlockSpec((None, rg, ug * DIL_TQ, CHUNK), lambda bi, r, u: (bi, r, u, q_chunk)),
            pl.BlockSpec((None, rg, length, CHUNK), lambda bi, r, u: (bi, r, 0, q_chunk + 1)),
            pl.BlockSpec((None, rg, length, CHUNK), lambda bi, r, u: (bi, r, 0, q_chunk + 2)),
            pl.BlockSpec((3, 2, 2 * DIL_TQ, DIL_WIN), lambda bi, r, u: (0, 0, 0, 0)),
        ],
        out_specs=[
            pl.BlockSpec((1, ug * DIL_TQ, rg * CHUNK), lambda bi, r, u: (bi, u, r)),
            pl.BlockSpec((1, ug * DIL_TQ, rg * CHUNK), lambda bi, r, u: (bi, u, r)),
        ],
        out_shape=[
            jax.ShapeDtypeStruct((b, length, dil * CHUNK), BF16),
            jax.ShapeDtypeStruct((b, length, dil * CHUNK), F32),
        ],
        compiler_params=_cparams(("arbitrary", "arbitrary", "arbitrary")),
        name=f"dilated_d{dil}",
    )(src, src, src, jnp.asarray(_dil_bias_table(dil)))
    n = b * dil * length
    return o.reshape(n, CHUNK), lw.reshape(n, CHUNK)


def _gqa_kernel(q_ref, k_ref, v_ref, o_ref, m_s, acc_s, s_s, *, tq, tk, seq):
    lo = _lo_mask(tq)
    n_kt = seq // tk
    qs = jnp.concatenate([_stack_head_pair(q_ref[0, :, j * LANES:(j + 1) * LANES], lo) for j in range(2)],
                         axis=0)
    m_s[...] = jnp.full(m_s.shape, NEG, F32)
    acc_s[...] = jnp.zeros(acc_s.shape, F32)

    def scores(kt, slot):
        k0 = pl.multiple_of(kt * tk, tk)
        s_s[slot] = lax.dot_general(qs, k_ref[0, pl.ds(k0, tk), :], NT_DIMS, preferred_element_type=F32)

    def softmax_values(kt, slot):
        k0 = pl.multiple_of(kt * tk, tk)
        s = s_s[slot]
        m_old = m_s[...]
        m_new = jnp.maximum(m_old, jnp.max(s, axis=-1, keepdims=True))
        alpha = jnp.exp2(m_old - m_new)
        e = jnp.exp2((s - jnp.concatenate([m_new] * (tk // LANES), axis=1)).astype(BF16))
        m_s[...] = m_new
        pv = jnp.dot(e, v_ref[0, pl.ds(k0, tk), :], preferred_element_type=F32)
        acc_s[...] = alpha * acc_s[...] + pv

    scores(0, 0)

    def body(j, carry):
        scores(2 * j + 1, 1)
        softmax_values(2 * j, 0)
        scores(2 * j + 2, 0)
        softmax_values(2 * j + 1, 1)
        return carry

    lax.fori_loop(0, n_kt // 2 - 1, body, 0, unroll=True)
    scores(n_kt - 1, 1)
    softmax_values(n_kt - 2, 0)
    softmax_values(n_kt - 1, 1)

    acc = acc_s[...]
    o = acc / pltpu.roll(acc, HEAD_DIM, 1)
    for j in range(2):
        odd = pltpu.roll(o[(2 * j + 1) * tq:(2 * j + 2) * tq], HEAD_DIM, 1)
        pair = jnp.where(lo, o[(2 * j) * tq:(2 * j + 1) * tq], odd)
        o_ref[0, :, j * LANES:(j + 1) * LANES] = pair.astype(BF16)


def _gqa_attention(proj3, tq=256, tk=512):
    b, s, _ = proj3.shape
    assert (s // tk) % 2 == 0 and s // tk >= 2
    return pl.pallas_call(
        functools.partial(_gqa_kernel, tq=tq, tk=tk, seq=s),
        grid=(b, 2, s // tq),
        in_specs=[
            pl.BlockSpec((1, tq, CHUNK), lambda bi, g, i: (bi, i, CH_QC + g)),
            pl.BlockSpec((1, s, LANES), lambda bi, g, i: (bi, 0, 2 * CH_KC + g)),
            pl.BlockSpec((1, s, LANES), lambda bi, g, i: (bi, 0, 2 * CH_VC + g)),
        ],
        out_specs=pl.BlockSpec((1, tq, CHUNK), lambda bi, g, i: (bi, i, g)),
        out_shape=jax.ShapeDtypeStruct((b, s, 2 * CHUNK), BF16),
        scratch_shapes=[
            pltpu.VMEM((4 * tq, LANES), F32),
            pltpu.VMEM((4 * tq, LANES), F32),
            pltpu.VMEM((2, 4 * tq, tk), F32),
        ],
        compiler_params=_cparams(("arbitrary", "arbitrary", "arbitrary")),
        name="gqa_attention",
    )(proj3, proj3, proj3)


def _rms(v, g):
    return v * lax.rsqrt(jnp.mean(v * v, axis=-1, keepdims=True) + RMS_EPS) * g


def _outproj_kernel(oa_ref, o1_ref, o4_ref, o16_ref, lw1_ref, lw4_ref, lw16_ref, oc_ref, x_ref,
                    w_ref, mg_ref, g2_ref, wrh_ref, wrl_ref, br_ref, x1_ref, h_ref, route_ref):
    lw1, lw4, lw16 = lw1_ref[...], lw4_ref[...], lw16_ref[...]
    lwm = jnp.maximum(jnp.maximum(lw1, lw4), lw16)
    w1, w4, w16 = jnp.exp2(lw1 - lwm), jnp.exp2(lw4 - lwm), jnp.exp2(lw16 - lwm)
    ob = (w1 * o1_ref[...].astype(F32) + w4 * o4_ref[...].astype(F32)
          + w16 * o16_ref[...].astype(F32)) / (w1 + w4 + w16)
    merged = jnp.concatenate([
        _rms(oa_ref[...].astype(F32), mg_ref[:, 0:CHUNK]).astype(BF16),
        _rms(ob, mg_ref[:, CHUNK:2 * CHUNK]).astype(BF16),
        _rms(oc_ref[...].astype(F32), mg_ref[:, 2 * CHUNK:]).astype(BF16),
    ], axis=-1)
    x1 = x_ref[...] + jnp.dot(merged, w_ref[...], preferred_element_type=F32)
    x1_ref[...] = x1
    h = _rms(x1, g2_ref[...])
    h_ref[...] = _pack_bf16_pair(h[:, :D_MODEL // 2], h[:, D_MODEL // 2:])

    hh = h.astype(BF16)
    hl = (h - hh.astype(F32)).astype(BF16)
    logits = (jnp.dot(hh, wrh_ref[...], preferred_element_type=F32)
              + jnp.dot(hl, wrh_ref[...], preferred_element_type=F32)
              + jnp.dot(hh, wrl_ref[...], preferred_element_type=F32)) + br_ref[...]
    lane = lax.broadcasted_iota(jnp.int32, logits.shape, 1)
    lane_f = lane.astype(F32)
    big = float(LANES)
    is_g = lane < N_GROUPS
    gl = jnp.where(is_g, logits, NEG)
    gmax = jnp.max(gl, axis=-1, keepdims=True)
    grp = jnp.min(jnp.where(gl == gmax, lane_f, big), axis=-1, keepdims=True)
    p_grp = 1.0 / jnp.sum(jnp.where(is_g, jnp.exp(logits - gmax), 0.0), axis=-1, keepdims=True)
    e_lo = N_GROUPS + EXPERTS_PER_GROUP * grp
    in_grp = (lane_f >= e_lo) & (lane_f < e_lo + EXPERTS_PER_GROUP)
    el = jnp.where(in_grp, logits, NEG)
    t1 = jnp.max(el, axis=-1, keepdims=True)
    i1 = jnp.min(jnp.where(el == t1, lane_f, big), axis=-1, keepdims=True)
    el2 = jnp.where(lane_f == i1, NEG, el)
    t2 = jnp.max(el2, axis=-1, keepdims=True)
    i2 = jnp.min(jnp.where(el2 == t2, lane_f, big), axis=-1, keepdims=True)
    e21 = jnp.exp(t2 - t1)
    gate1 = p_grp / (1.0 + e21)
    gate2 = p_grp * e21 / (1.0 + e21)
    route = jnp.where(lane == 0, i1 - N_GROUPS,
                      jnp.where(lane == 1, i2 - N_GROUPS,
                                jnp.where(lane == 2, gate1, jnp.where(lane == 3, gate2, 0.0))))
    route_ref[...] = route


def _outproj(oa, ob_parts, oc, x2, w_out, mix_g, g2, wr_hi, wr_lo, b_r, tm=512):
    n = x2.shape[0]
    (o1, lw1), (o4, lw4), (o16, lw16) = ob_parts
    row = lambda w: pl.BlockSpec((tm, w), lambda i: (i, 0))
    full = lambda r, w: pl.BlockSpec((r, w), lambda i: (0, 0))
    return pl.pallas_call(
        _outproj_kernel,
        grid=(n // tm,),
        in_specs=[row(CHUNK), row(CHUNK), row(CHUNK), row(CHUNK), row(CHUNK), row(CHUNK), row(CHUNK),
                  row(2 * CHUNK), row(D_MODEL),
                  full(D_MODEL, D_MODEL), full(1, D_MODEL), full(1, D_MODEL),
                  full(D_MODEL, LANES), full(D_MODEL, LANES), full(1, LANES)],
        out_specs=[row(D_MODEL), row(D_MODEL // 2), row(LANES)],
        out_shape=[jax.ShapeDtypeStruct((n, D_MODEL), F32),
                   jax.ShapeDtypeStruct((n, D_MODEL // 2), jnp.uint32),
                   jax.ShapeDtypeStruct((n, LANES), F32)],
        compiler_params=_cparams(("arbitrary",)),
        name="outproj_router",
    )(oa, o1, o4, o16, lw1, lw4, lw16, oc, x2, w_out, mix_g, g2, wr_hi, wr_lo, b_r)


META_ROWS = 384


def _rank_kernel(route_ref, dest_ref, meta_ref, cnt_s, base_s, tri_s, *, tm):
    phase = pl.program_id(0)
    i = pl.program_id(1)
    lane = lax.broadcasted_iota(jnp.int32, (tm, LANES), 1)
    lane_f = lane.astype(F32)
    route = route_ref[...]
    oh0 = jnp.where(lane_f == route[:, 0:1], 1.0, 0.0)
    oh1 = jnp.where(lane_f == route[:, 1:2], 1.0, 0.0)
    oh = oh0 + oh1
    colsum = jnp.sum(oh, axis=0, keepdims=True)

    @pl.when((phase == 0) & (i == 0))
    def _():
        cnt_s[...] = jnp.zeros(cnt_s.shape, F32)
        r = lax.broadcasted_iota(jnp.int32, (tm, tm), 0)
        c = lax.broadcasted_iota(jnp.int32, (tm, tm), 1)
        tri_s[...] = jnp.where(c < r, 1.0, 0.0).astype(BF16)

    @pl.when(phase == 0)
    def _():
        cnt_s[...] = cnt_s[...] + colsum

    @pl.when((phase == 1) & (i == 0))
    def _():
        lane8 = lax.broadcasted_iota(jnp.int32, cnt_s.shape, 1)
        padded = jnp.floor((cnt_s[...] + (MOE_BLOCK - 1)) * (1.0 / MOE_BLOCK)) * MOE_BLOCK
        ends = padded
        for sh in (1, 2, 4, 8, 16):
            ends = ends + jnp.where(lane8 >= sh, pltpu.roll(ends, sh, 1), 0.0)
        base_s[...] = ends - padded
        pos = lax.broadcasted_iota(jnp.int32, (META_ROWS, LANES), 0).astype(F32) * MOE_BLOCK
        lane_m = lax.broadcasted_iota(jnp.int32, (META_ROWS, LANES), 1)
        ends_row = ends[0:1, :]
        hit = jnp.where((lane_m < N_EXPERTS) & (ends_row <= pos), 1.0, 0.0)
        blk_e = jnp.minimum(jnp.sum(hit, axis=-1, keepdims=True), N_EXPERTS - 1.0)
        total = jnp.sum(jnp.where(lane_m[0:1] == N_EXPERTS - 1, ends_row, 0.0), axis=-1, keepdims=True)
        row_m = lax.broadcasted_iota(jnp.int32, (META_ROWS, 1), 0)
        meta = jnp.where(row_m == META_ROWS - 1, total * (1.0 / MOE_BLOCK), blk_e)
        meta_ref[...] = meta.astype(jnp.int32)

    @pl.when(phase == 1)
    def _():
        prior = jnp.dot(tri_s[...], oh.astype(BF16), preferred_element_type=F32)
        val = base_s[0:1, :] + prior
        d0 = jnp.sum(oh0 * val, axis=-1, keepdims=True)
        d1 = jnp.sum(oh1 * val, axis=-1, keepdims=True)
        dest_ref[...] = jnp.where(lane == 0, d0, jnp.where(lane == 1, d1, 0.0)).astype(jnp.int32)
        base_s[...] = base_s[...] + colsum


def _rank(route, tm=512):
    n = route.shape[0]
    return pl.pallas_call(
        functools.partial(_rank_kernel, tm=tm),
        grid=(2, n // tm),
        in_specs=[pl.BlockSpec((tm, LANES), lambda p, i: (i, 0))],
        out_specs=[pl.BlockSpec((tm, LANES), lambda p, i: (i * p, 0)),
                   pl.BlockSpec((META_ROWS, 1), lambda p, i: (0, 0))],
        out_shape=[jax.ShapeDtypeStruct((n, LANES), jnp.int32),
                   jax.ShapeDtypeStruct((META_ROWS, 1), jnp.int32)],
        scratch_shapes=[pltpu.VMEM((8, LANES), F32), pltpu.VMEM((8, LANES), F32), pltpu.VMEM((tm, tm), BF16)],
        compiler_params=_cparams(("arbitrary", "arbitrary")),
        name="moe_rank",
    )(route)


def _dispatch_kernel(dest_ref, h_ref, xs_in_hbm, xs_hbm, sem, *, n, tc):
    del xs_in_hbm
    t0 = pl.program_id(0) * tc

    def body(j, carry):
        src = h_ref.at[pl.ds(j, 1)]
        pltpu.make_async_copy(src, xs_hbm.at[pl.ds(dest_ref[t0 + j], 1)], sem.at[0]).start(priority=0)
        pltpu.make_async_copy(src, xs_hbm.at[pl.ds(dest_ref[n + t0 + j], 1)], sem.at[0]).start(priority=1)
        return carry

    lax.fori_loop(0, tc, body, 0, unroll=8)
    for _ in range(2):
        pltpu.make_async_copy(h_ref, xs_hbm.at[pl.ds(0, tc)], sem.at[0]).wait()


def _dispatch(dest_flat, h, cap, tc=256):
    n, w = h.shape
    zeros = jnp.zeros((cap, w), h.dtype)
    return pl.pallas_call(
        functools.partial(_dispatch_kernel, n=n, tc=tc),
        grid_spec=pltpu.PrefetchScalarGridSpec(
            num_scalar_prefetch=1,
            grid=(n // tc,),
            in_specs=[pl.BlockSpec((tc, w), lambda i, d: (i, 0)), pl.BlockSpec(memory_space=pl.ANY)],
            out_specs=pl.BlockSpec(memory_space=pl.ANY),
            scratch_shapes=[pltpu.SemaphoreType.DMA((1,))],
        ),
        out_shape=jax.ShapeDtypeStruct((cap, w), h.dtype),
        input_output_aliases={2: 0},
        compiler_params=_cparams(("arbitrary",)),
        name="moe_dispatch",
    )(dest_flat, h, zeros)


def _expert_kernel(blk_e_ref, nact_ref, xs_ref, wg_ref, wu_ref, wd_ref, y_ref, wg_s, wu_s, wd_s):
    i = pl.program_id(0)
    active = i < nact_ref[0]
    changed = (i == 0) | (blk_e_ref[i] != blk_e_ref[jnp.maximum(i - 1, 0)])

    @pl.when(active & changed)
    def _():
        wg_s[...] = wg_ref[...].astype(BF16)
        wu_s[...] = wu_ref[...].astype(BF16)
        wd_s[...] = wd_ref[...].astype(BF16)

    @pl.when(active)
    def _():
        hi, lo = _unpack_bf16_pair(xs_ref[...])
        xb = jnp.concatenate([hi.astype(BF16), lo.astype(BF16)], axis=1)
        g = jnp.dot(xb, wg_s[...], preferred_element_type=F32)
        u = jnp.dot(xb, wu_s[...], preferred_element_type=F32)
        a = (g * jax.nn.sigmoid(g) * u).astype(BF16)
        y = jnp.dot(a, wd_s[...], preferred_element_type=F32)
        y_ref[...] = _pack_bf16_pair(y[:, :D_MODEL // 2], y[:, D_MODEL // 2:])

    @pl.when(jnp.logical_not(active))
    def _():
        y_ref[...] = jnp.zeros(y_ref.shape, y_ref.dtype)


def _experts(blk_e, nact, xs, w_gate, w_up, w_down, layer):
    cap = xs.shape[0]
    n_blk = cap // MOE_BLOCK

    def blk(i, be, na):
        return jnp.minimum(i, na[0] - 1)

    return pl.pallas_call(
        _expert_kernel,
        grid_spec=pltpu.PrefetchScalarGridSpec(
            num_scalar_prefetch=2,
            grid=(n_blk,),
            in_specs=[
                pl.BlockSpec((MOE_BLOCK, D_MODEL // 2), lambda i, be, na: (blk(i, be, na), 0)),
                pl.BlockSpec((None, None, D_MODEL, D_EXPERT), lambda i, be, na: (layer, be[blk(i, be, na)], 0, 0)),
                pl.BlockSpec((None, None, D_MODEL, D_EXPERT), lambda i, be, na: (layer, be[blk(i, be, na)], 0, 0)),
                pl.BlockSpec((None, None, D_EXPERT, D_MODEL), lambda i, be, na: (layer, be[blk(i, be, na)], 0, 0)),
            ],
            out_specs=pl.BlockSpec((MOE_BLOCK, D_MODEL // 2), lambda i, be, na: (i, 0)),
            scratch_shapes=[pltpu.VMEM((D_MODEL, D_EXPERT), BF16), pltpu.VMEM((D_MODEL, D_EXPERT), BF16),
                            pltpu.VMEM((D_EXPERT, D_MODEL), BF16)],
        ),
        out_shape=jax.ShapeDtypeStruct((cap, D_MODEL // 2), jnp.uint32),
        compiler_params=_cparams(("arbitrary",)),
        name="moe_experts",
    )(blk_e, nact, xs, w_gate, w_up, w_down)


def _combine_kernel(dest_ref, x1_ref, route_ref, y_hbm, o_ref, yb, sem, *, n, tc):
    t0 = pl.program_id(0) * tc

    def body(j, carry):
        pltpu.make_async_copy(y_hbm.at[pl.ds(dest_ref[t0 + j], 1)], yb.at[0, pl.ds(j, 1)],
                              sem.at[0]).start(priority=0)
        pltpu.make_async_copy(y_hbm.at[pl.ds(dest_ref[n + t0 + j], 1)], yb.at[1, pl.ds(j, 1)],
                              sem.at[0]).start(priority=1)
        return carry

    lax.fori_loop(0, tc, body, 0, unroll=8)
    for k in range(2):
        pltpu.make_async_copy(y_hbm.at[pl.ds(0, tc)], yb.at[k], sem.at[0]).wait()
    route = route_ref[...]
    hi0, lo0 = _unpack_bf16_pair(yb[0])
    hi1, lo1 = _unpack_bf16_pair(yb[1])
    half = D_MODEL // 2
    o_ref[:, :half] = x1_ref[:, :half] + route[:, 2:3] * hi0 + route[:, 3:4] * hi1
    o_ref[:, half:] = x1_ref[:, half:] + route[:, 2:3] * lo0 + route[:, 3:4] * lo1


def _combine(dest_flat, x1, route, y, tc=256):
    n = x1.shape[0]
    return pl.pallas_call(
        functools.partial(_combine_kernel, n=n, tc=tc),
        grid_spec=pltpu.PrefetchScalarGridSpec(
            num_scalar_prefetch=1,
            grid=(n // tc,),
            in_specs=[pl.BlockSpec((tc, D_MODEL), lambda i, d: (i, 0)),
                      pl.BlockSpec((tc, LANES), lambda i, d: (i, 0)),
                      pl.BlockSpec(memory_space=pl.ANY)],
            out_specs=pl.BlockSpec((tc, D_MODEL), lambda i, d: (i, 0)),
            scratch_shapes=[pltpu.VMEM((2, tc, D_MODEL // 2), jnp.uint32), pltpu.SemaphoreType.DMA((1,))],
        ),
        out_shape=jax.ShapeDtypeStruct((n, D_MODEL), F32),
        compiler_params=_cparams(("arbitrary",)),
        name="moe_combine",
    )(dest_flat, x1, route, y)


def _rope_tables(seq):
    n_freq = HEAD_DIM // 4
    t = np.arange(seq)
    inv_freq = jnp.asarray(ROPE_THETA, F32) ** (-jnp.arange(n_freq, dtype=F32) / n_freq)
    pos = jnp.stack([jnp.asarray(t // GRID_W, F32), jnp.asarray(t % GRID_W, F32)], axis=1)
    ang = pos[:, :, None] * inv_freq
    cos = jnp.broadcast_to(jnp.cos(ang)[:, :, None, :], (seq, 2, 2, n_freq)).reshape(seq, HEAD_DIM)
    sin = jnp.broadcast_to(jnp.sin(ang)[:, :, None, :], (seq, 2, 2, n_freq)).reshape(seq, HEAD_DIM)
    first_half = (np.arange(HEAD_DIM) % (2 * n_freq)) < n_freq
    sa = jnp.where(first_half, -sin, 0.0)
    sb = jnp.where(first_half, 0.0, sin)
    rep = CHUNK // HEAD_DIM
    return jnp.tile(cos, (1, rep)), jnp.tile(sa, (1, rep)), jnp.tile(sb, (1, rep))


def _prep_w_in(w):
    base = w[:, :2048]
    kc0, kc1 = w[:, 2048:2112], w[:, 2112:2176]
    vc0, vc1 = w[:, 2176:2240], w[:, 2240:2304]
    z = jnp.zeros_like(vc0)
    return jnp.concatenate([base, kc0, kc0, kc1, kc1, vc0, z, vc1, z], axis=1).astype(BF16)


def _prep_gain(qk_g):
    qs = SCALE * LOG2E
    ones = jnp.ones((CHUNK,), F32)
    t4 = lambda g: jnp.tile(g, CHUNK // HEAD_DIM)
    return jnp.concatenate([
        t4(qk_g[0, 0]) * qs, t4(qk_g[0, 1]), ones,
        t4(qk_g[1, 0]) * qs, t4(qk_g[1, 1]), ones,
        t4(qk_g[2, 0]) * qs, t4(qk_g[2, 0]) * qs, t4(qk_g[2, 1]), ones,
    ])[None, :]


def _split_hi_lo(w):
    hi = w.astype(BF16)
    lo = (w - hi.astype(F32)).astype(BF16)
    return hi, lo


def _mixer_layer(x2, bsz, seq, tables, norm1_g, w_in, qk_g, rpb, mix_g, w_out, norm2_g, w_rg, b_rg, w_re, b_re):
    n, d = x2.shape
    cos_t, sa_t, sb_t, bd = tables
    proj, pb4, pb16 = _inproj(x2, norm1_g[None, :], _prep_w_in(w_in), _prep_gain(qk_g), bd, cos_t, sa_t, sb_t, seq)
    proj3 = proj.reshape(bsz, seq, PROJ_W)
    oa = _na_attention(proj3, _na_bias_table(rpb)).reshape(n, CHUNK)
    ob_parts = [_dilated_branch(proj3[:, None], CH_QB, 1, 4), _dilated_branch(pb4, 0, 1, 4),
                _dilated_branch(pb16, 0, 2, 2)]
    oc = _gqa_attention(proj3).reshape(n, 2 * CHUNK)

    w_r = jnp.zeros((d, LANES), F32)
    w_r = w_r.at[:, :N_GROUPS].set(w_rg).at[:, N_GROUPS:N_GROUPS + N_EXPERTS].set(w_re)
    b_r = jnp.zeros((1, LANES), F32)
    b_r = b_r.at[0, :N_GROUPS].set(b_rg).at[0, N_GROUPS:N_GROUPS + N_EXPERTS].set(b_re)
    wr_hi, wr_lo = _split_hi_lo(w_r)
    return _outproj(oa, ob_parts, oc, x2, w_out.astype(BF16), mix_g[None, :], norm2_g[None, :], wr_hi, wr_lo, b_r)


def _moe_layer(x1, h, route, w_gate, w_up, w_down, layer):
    n = x1.shape[0]
    cap = 2 * n + N_EXPERTS * MOE_BLOCK
    n_blk = cap // MOE_BLOCK
    dest, meta = _rank(route)
    dest_flat = dest[:, :2].T.reshape(2 * n)
    xs = _dispatch(dest_flat, h, cap)
    y = _experts(meta[:n_blk, 0], meta[META_ROWS - 1:, 0], xs, w_gate, w_up, w_down, layer)
    return _combine(dest_flat, x1, route, y)


def _tables(seq):
    hd = np.arange(CHUNK) // HEAD_DIM
    return (*_rope_tables(seq), jnp.asarray(hd[:, None] == hd[None, :], BF16))


@jax.jit
def kernel(x, norm1_g, w_in, qk_norm_g, na_rpb, mix_norm_g, w_out, norm2_g, w_router_group, b_router_group,
           w_router_expert, b_router_expert, w_gate, w_up, w_down):
    bsz, seq, d = x.shape
    tables = _tables(seq)
    x2 = x.reshape(bsz * seq, d)
    for l in range(DEPTH):
        x1, h, route = _mixer_layer(x2, bsz, seq, tables, norm1_g[l], w_in[l], qk_norm_g[l], na_rpb[l],
                                    mix_norm_g[l], w_out[l], norm2_g[l], w_router_group[l], b_router_group[l],
                                    w_router_expert[l], b_router_expert[l])
        x2 = _moe_layer(x1, h, route, w_gate, w_up, w_down, l)
    return x2.reshape(bsz, seq, d)
```

```python
import functools
import math

import numpy as np
import jax
import jax.numpy as jnp
from jax import lax
from jax.experimental import pallas as pl
from jax.experimental.pallas import tpu as pltpu

F32 = jnp.float32
BF16 = jnp.bfloat16

D_MODEL = 1024
DEPTH = 4
GRID_W = 64
HEAD_DIM = 64
RMS_EPS = 1e-6
NEG = -1e30
LOG2E = math.log2(math.e)
SCALE = HEAD_DIM ** -0.5

NA_ROWS = 8
NA_COLS = 16
DILATIONS = (1, 4, 16)
DIL_HALF = 64
ROPE_THETA = 10000.0

N_GROUPS = 4
EXPERTS_PER_GROUP = 8
N_EXPERTS = 32
D_EXPERT = 512
MOE_BLOCK = 256

LANES = 128
CHUNK = 256
CH_QA, CH_KA, CH_VA, CH_QB, CH_KB, CH_VB, CH_QC, CH_KC, CH_VC = 0, 1, 2, 3, 4, 5, 6, 8, 9
N_CHUNKS = 10
PROJ_W = N_CHUNKS * CHUNK
NORM_CHUNKS = (CH_QA, CH_KA, CH_QB, CH_KB, CH_QC, CH_QC + 1, CH_KC)
ROPE_CHUNKS = (CH_QC, CH_QC + 1, CH_KC)

VMEM_LIMIT = 56 * 1024 * 1024

NT_DIMS = (((1,), (1,)), ((), ()))


def _cparams(sem):
    return pltpu.CompilerParams(dimension_semantics=sem, vmem_limit_bytes=VMEM_LIMIT)


INPROJ_PARTS = 2


def _inproj_kernel(x_ref, g1_ref, w_ref, gain_ref, bd_ref, cos_ref, sa_ref, sb_ref, o_ref, o4_ref, o16_ref, stage_ref):
    tm = x_ref.shape[0]
    tp = tm // INPROJ_PARTS
    part_rows = [slice(i * tp, (i + 1) * tp) for i in range(INPROJ_PARTS)]
    chunk_cols = [slice(c * CHUNK, (c + 1) * CHUNK) for c in range(N_CHUNKS)]
    n_stage = 3 * CHUNK // LANES

    normed = []
    for rows in part_rows:
        x = x_ref[rows, :]
        ms = jnp.mean(x * x, axis=-1, keepdims=True)
        normed.append((x * lax.rsqrt(ms + RMS_EPS) * g1_ref[...]).astype(BF16))
    projs = [jnp.dot(h, w_ref[...], preferred_element_type=F32) for h in normed]
    for part, (rows, proj) in enumerate(zip(part_rows, projs)):
        sumsq = {c: jnp.dot(jnp.square(proj[:, chunk_cols[c]]).astype(BF16), bd_ref[...],
                            preferred_element_type=F32) for c in NORM_CHUNKS}
        for c in range(N_CHUNKS):
            cols = chunk_cols[c]
            p = proj[:, cols]
            if c in NORM_CHUNKS:
                p = p * lax.rsqrt(sumsq[c] * (1.0 / HEAD_DIM) + RMS_EPS) * gain_ref[:, cols]
            if c in ROPE_CHUNKS:
                p = (p * cos_ref[rows, :] + pltpu.roll(p, CHUNK - 16, 1) * sa_ref[rows, :]
                     + pltpu.roll(p, 16, 1) * sb_ref[rows, :])
            if c == CH_VC:
                lane = lax.broadcasted_iota(jnp.int32, p.shape, 1)
                p = jnp.where(lane % LANES >= HEAD_DIM, 1.0, p)
            o_ref[rows, cols] = p.astype(BF16)
            if c in (CH_QB, CH_KB, CH_VB):
                for hf in range(CHUNK // LANES):
                    lane0 = ((c - CH_QB) * (CHUNK // LANES) + hf) * LANES
                    slot = part * n_stage + lane0 // LANES
                    stage_ref[slot] = p[:, hf * LANES:(hf + 1) * LANES]
                    for dil, od_ref in ((4, o4_ref), (16, o16_ref)):
                        nr = tp // dil
                        for r in range(dil):
                            picked = stage_ref[slot, pl.ds(r, nr, stride=dil), :]
                            od_ref[0, r, part * nr:(part + 1) * nr, lane0:lane0 + LANES] = picked.astype(BF16)


def _inproj(x2, g1, w, gain, bd, cos_t, sa_t, sb_t, seq, tm=512):
    n = x2.shape[0]
    nsb = seq // tm
    bsz = n // seq
    dil_spec = lambda dil: pl.BlockSpec((1, dil, tm // dil, 3 * CHUNK), lambda i: (i // nsb, 0, i % nsb, 0))
    dil_shape = lambda dil: jax.ShapeDtypeStruct((bsz, dil, seq // dil, 3 * CHUNK), BF16)
    return pl.pallas_call(
        _inproj_kernel,
        grid=(n // tm,),
        in_specs=[
            pl.BlockSpec((tm, D_MODEL), lambda i: (i, 0)),
            pl.BlockSpec((1, D_MODEL), lambda i: (0, 0)),
            pl.BlockSpec((D_MODEL, PROJ_W), lambda i: (0, 0)),
            pl.BlockSpec((1, PROJ_W), lambda i: (0, 0)),
            pl.BlockSpec((CHUNK, CHUNK), lambda i: (0, 0)),
            pl.BlockSpec((tm, CHUNK), lambda i: (i % nsb, 0)),
            pl.BlockSpec((tm, CHUNK), lambda i: (i % nsb, 0)),
            pl.BlockSpec((tm, CHUNK), lambda i: (i % nsb, 0)),
        ],
        out_specs=[pl.BlockSpec((tm, PROJ_W), lambda i: (i, 0)), dil_spec(4), dil_spec(16)],
        out_shape=[jax.ShapeDtypeStruct((n, PROJ_W), BF16), dil_shape(4), dil_shape(16)],
        scratch_shapes=[pltpu.VMEM((INPROJ_PARTS * 3 * CHUNK // LANES, tm // INPROJ_PARTS, LANES), F32)],
        compiler_params=_cparams(("arbitrary",)),
        name="inproj",
    )(x2, g1, w, gain, bd, cos_t, sa_t, sb_t)


def _stack_head_pair(qp, lo):
    zero = jnp.zeros_like(qp)
    return jnp.concatenate([jnp.where(lo, qp, zero), jnp.where(lo, zero, qp)], axis=0)


def _pack_bf16_pair(a, b):
    ua = lax.bitcast_convert_type(a.astype(BF16).astype(F32), jnp.uint32)
    ub = lax.bitcast_convert_type(b.astype(BF16).astype(F32), jnp.uint32)
    return ua | (ub >> 16)


def _unpack_bf16_pair(u):
    hi = lax.bitcast_convert_type(u & jnp.uint32(0xFFFF0000), F32)
    lo = lax.bitcast_convert_type(u << 16, F32)
    return hi, lo


def _lo_mask(rows):
    return lax.broadcasted_iota(jnp.int32, (rows, LANES), 1) < HEAD_DIM


NA_ROW_GROUP = 4


def _na_kernel(q_ref, k_ref, v_ref, bias_ref, o_ref, *, rows_per_step, n_rows):
    i = pl.program_id(1)
    lo = _lo_mask(GRID_W)
    win = NA_ROWS * GRID_W

    def window(rl):
        r = i * rows_per_step + rl
        r0 = jnp.clip(r - NA_ROWS // 2, 0, n_rows - NA_ROWS)
        return pl.multiple_of(r0 * GRID_W, GRID_W), r - r0

    for g0 in range(0, rows_per_step, NA_ROW_GROUP):
        chains = [(rl, p) for rl in range(g0, g0 + NA_ROW_GROUP) for p in range(2)]
        scores = []
        for rl, p in chains:
            start, rr = window(rl)
            cols = slice(p * LANES, (p + 1) * LANES)
            qs = _stack_head_pair(q_ref[0, rl * GRID_W:(rl + 1) * GRID_W, cols], lo)
            s = lax.dot_general(qs, k_ref[0, pl.ds(start, win), cols], NT_DIMS, preferred_element_type=F32)
            scores.append(s + bias_ref[p, rr])
        stats = []
        for s in scores:
            e = jnp.exp2(s - jnp.max(s, axis=-1, keepdims=True))
            stats.append((e, jnp.sum(e, axis=-1, keepdims=True)))
        for (rl, p), (e, l) in zip(chains, stats):
            start, _ = window(rl)
            cols = slice(p * LANES, (p + 1) * LANES)
            pv = jnp.dot(e.astype(BF16), v_ref[0, pl.ds(start, win), cols], preferred_element_type=F32)
            o = pv / l
            o_ref[0, rl * GRID_W:(rl + 1) * GRID_W, cols] = jnp.where(lo, o[:GRID_W], o[GRID_W:]).astype(BF16)


def _na_attention(proj3, bias, rows_per_step=8):
    b, s, _ = proj3.shape
    n_rows = s // GRID_W
    tq = rows_per_step * GRID_W
    return pl.pallas_call(
        functools.partial(_na_kernel, rows_per_step=rows_per_step, n_rows=n_rows),
        grid=(b, n_rows // rows_per_step),
        in_specs=[
            pl.BlockSpec((1, tq, CHUNK), lambda bi, i: (bi, i, CH_QA)),
            pl.BlockSpec((1, s, CHUNK), lambda bi, i: (bi, 0, CH_KA)),
            pl.BlockSpec((1, s, CHUNK), lambda bi, i: (bi, 0, CH_VA)),
            pl.BlockSpec((2, NA_ROWS, 2 * GRID_W, NA_ROWS * GRID_W), lambda bi, i: (0, 0, 0, 0)),
        ],
        out_specs=pl.BlockSpec((1, tq, CHUNK), lambda bi, i: (bi, i, 0)),
        out_shape=jax.ShapeDtypeStruct((b, s, CHUNK), BF16),
        compiler_params=_cparams(("arbitrary", "arbitrary")),
        name="na_attention",
    )(proj3, proj3, proj3, bias)


def _na_bias_table(rpb):
    c = np.arange(GRID_W)
    cs = np.clip(c - NA_COLS // 2, 0, GRID_W - NA_COLS)
    kc = np.arange(GRID_W)
    inwin = (kc[None, :] >= cs[:, None]) & (kc[None, :] < cs[:, None] + NA_COLS)
    dc = kc[None, :] - c[:, None] + NA_COLS - 1
    rr = np.arange(NA_ROWS)
    dr = rr[None, :] - rr[:, None] + NA_ROWS - 1
    sel_r = jnp.asarray(dr[:, :, None] == np.arange(2 * NA_ROWS - 1), F32)
    sel_c = jnp.asarray((dc[:, :, None] == np.arange(2 * NA_COLS - 1)) & inwin[:, :, None], F32)
    tab = jnp.einsum("rka,hab,cjb->hrckj", sel_r, rpb, sel_c, precision=lax.Precision.HIGHEST)
    tab = jnp.where(inwin[None, None, :, None, :], tab * LOG2E, NEG)
    tab = tab.reshape(2, 2, NA_ROWS, GRID_W, NA_ROWS * GRID_W)
    return tab.transpose(0, 2, 1, 3, 4).reshape(2, NA_ROWS, 2 * GRID_W, NA_ROWS * GRID_W)


DIL_TQ = 128
DIL_WIN = DIL_TQ + 2 * DIL_HALF


def _dil_kernel(q_ref, k_ref, v_ref, bias_ref, o_ref, lw_ref, *, length, rg, ug, nb):
    u0 = pl.program_id(2) * ug
    lo = _lo_mask(DIL_TQ)
    chains = [(r, ub, p) for r in range(rg) for ub in range(ug) for p in range(2)]

    def window(ub):
        gb = u0 + ub
        ws = pl.multiple_of(jnp.clip(gb * DIL_TQ - DIL_HALF, 0, length - DIL_WIN), DIL_HALF)
        case = jnp.where(gb == 0, 0, jnp.where(gb == nb - 1, 2, 1))
        return ws, case

    scores = []
    for r, ub, p in chains:
        ws, case = window(ub)
        cols = slice(p * LANES, (p + 1) * LANES)
        qs = _stack_head_pair(q_ref[r, ub * DIL_TQ:(ub + 1) * DIL_TQ, cols], lo)
        s = lax.dot_general(qs, k_ref[r, pl.ds(ws, DIL_WIN), cols], NT_DIMS, preferred_element_type=F32)
        scores.append(s + bias_ref[case, p])
    stats = []
    for s in scores:
        m = jnp.max(s, axis=-1, keepdims=True)
        e = jnp.exp2(s - m)
        stats.append((m, e, jnp.sum(e, axis=-1, keepdims=True)))
    for (r, ub, p), (m, e, l) in zip(chains, stats):
        ws, _ = window(ub)
        cols = slice(p * LANES, (p + 1) * LANES)
        pv = jnp.dot(e.astype(BF16), v_ref[r, pl.ds(ws, DIL_WIN), cols], preferred_element_type=F32)
        o = pv / l
        lw = jnp.broadcast_to(m + jnp.log2(l), (2 * DIL_TQ, LANES))
        rows = slice(ub * DIL_TQ, (ub + 1) * DIL_TQ)
        ocols = slice(r * CHUNK + p * LANES, r * CHUNK + (p + 1) * LANES)
        o_ref[0, rows, ocols] = jnp.where(lo, o[:DIL_TQ], o[DIL_TQ:]).astype(BF16)
        lw_ref[0, rows, ocols] = jnp.where(lo, lw[:DIL_TQ], lw[DIL_TQ:])


def _dil_bias_table(dil):
    slopes = np.exp2(-8.0 * np.arange(1, 5) / 4.0)
    qi = np.arange(DIL_TQ)[:, None]
    kj = np.arange(DIL_WIN)[None, :]
    tab = np.zeros((3, 4, DIL_TQ, DIL_WIN), np.float32)
    for case, shift in enumerate((0, -DIL_HALF, -2 * DIL_HALF)):
        rel = kj + shift - qi
        for h in range(4):
            tab[case, h] = np.where(np.abs(rel) <= DIL_HALF, -slopes[h] * dil * np.abs(rel) * LOG2E, NEG)
    return tab.reshape(3, 2, 2 * DIL_TQ, DIL_WIN)


def _dilated_branch(src, q_chunk, rg, ug):
    b, dil, length, _ = src.shape
    nb = length // DIL_TQ
    o, lw = pl.pallas_call(
        functools.partial(_dil_kernel, length=length, rg=rg, ug=ug, nb=nb),
        grid=(b, dil // rg, nb // ug),
        in_specs=[
            pl.BlockSpec((None, rg, ug * DIL_TQ, CHUNK), lambda bi, r, u: (bi, r, u, q_chunk)),
            pl.BlockSpec((None, rg, length, CHUNK), lambda bi, r, u: (bi, r, 0, q_chunk + 1)),
            pl.BlockSpec((None, rg, length, CHUNK), lambda bi, r, u: (bi, r, 0, q_chunk + 2)),
            pl.BlockSpec((3, 2, 2 * DIL_TQ, DIL_WIN), lambda bi, r, u: (0, 0, 0, 0)),
        ],
        out_specs=[
            pl.BlockSpec((1, ug * DIL_TQ, rg * CHUNK), lambda bi, r, u: (bi, u, r)),
            pl.BlockSpec((1, ug * DIL_TQ, rg * CHUNK), lambda bi, r, u: (bi, u, r)),
        ],
        out_shape=[
            jax.ShapeDtypeStruct((b, length, dil * CHUNK), BF16),
            jax.ShapeDtypeStruct((b, length, dil * CHUNK), F32),
        ],
        compiler_params=_cparams(("arbitrary", "arbitrary", "arbitrary")),
        name=f"dilated_d{dil}",
    )(src, src, src, jnp.asarray(_dil_bias_table(dil)))
    return o, lw


TN_DIMS = (((0,), (0,)), ((), ()))


def _gqa_kernel(q_ref, k_ref, v_ref, o_ref, m_s, acc_s, s_s, *, tq, tk, seq):
    lo = _lo_mask(tq)
    n_kt = seq // tk
    qs = jnp.concatenate([_stack_head_pair(q_ref[0, :, j * LANES:(j + 1) * LANES], lo) for j in range(2)],
                         axis=0)
    m_s[...] = jnp.full(m_s.shape, NEG, F32)
    acc_s[...] = jnp.zeros(acc_s.shape, F32)

    def scores(kt, slot):
        k0 = pl.multiple_of(kt * tk, tk)
        s_s[slot] = lax.dot_general(k_ref[0, pl.ds(k0, tk), :], qs, NT_DIMS, preferred_element_type=F32)

    def softmax_values(kt, slot):
        k0 = pl.multiple_of(kt * tk, tk)
        s = s_s[slot]
        m_old = m_s[...]
        m_new = jnp.maximum(m_old, jnp.max(s, axis=0, keepdims=True))
        alpha = jnp.exp2(m_old - m_new)
        e = jnp.exp2((s - m_new[0:1, :]).astype(BF16))
        m_s[...] = m_new
        pv = lax.dot_general(v_ref[0, pl.ds(k0, tk), :], e, TN_DIMS, preferred_element_type=F32)
        acc_s[...] = alpha[0:1, :] * acc_s[...] + pv

    scores(0, 0)

    def body(j, carry):
        scores(2 * j + 1, 1)
        softmax_values(2 * j, 0)
        scores(2 * j + 2, 0)
        softmax_values(2 * j + 1, 1)
        return carry

    lax.fori_loop(0, n_kt // 2 - 1, body, 0, unroll=True)
    scores(n_kt - 1, 1)
    softmax_values(n_kt - 2, 0)
    softmax_values(n_kt - 1, 1)

    acc = acc_s[...]
    o_t = acc[:HEAD_DIM] / acc[HEAD_DIM:HEAD_DIM + 1]
    for j in range(2):
        pair_t = jnp.concatenate([o_t[:, (2 * j) * tq:(2 * j + 1) * tq], o_t[:, (2 * j + 1) * tq:(2 * j + 2) * tq]],
                                 axis=0)
        o_ref[0, :, j * LANES:(j + 1) * LANES] = pair_t.T.astype(BF16)


def _gqa_attention(proj3, tq=256, tk=256):
    b, s, _ = proj3.shape
    assert (s // tk) % 2 == 0 and s // tk >= 2
    return pl.pallas_call(
        functools.partial(_gqa_kernel, tq=tq, tk=tk, seq=s),
        grid=(b, 2, s // tq),
        in_specs=[
            pl.BlockSpec((1, tq, CHUNK), lambda bi, g, i: (bi, i, CH_QC + g)),
            pl.BlockSpec((1, s, LANES), lambda bi, g, i: (bi, 0, 2 * CH_KC + g)),
            pl.BlockSpec((1, s, LANES), lambda bi, g, i: (bi, 0, 2 * CH_VC + g)),
        ],
        out_specs=pl.BlockSpec((1, tq, CHUNK), lambda bi, g, i: (bi, i, g)),
        out_shape=jax.ShapeDtypeStruct((b, s, 2 * CHUNK), BF16),
        scratch_shapes=[
            pltpu.VMEM((8, 4 * tq), F32),
            pltpu.VMEM((LANES, 4 * tq), F32),
            pltpu.VMEM((2, tk, 4 * tq), F32),
        ],
        compiler_params=_cparams(("arbitrary", "arbitrary", "arbitrary")),
        name="gqa_attention",
    )(proj3, proj3, proj3)


def _rms(v, g):
    return v * lax.rsqrt(jnp.mean(v * v, axis=-1, keepdims=True) + RMS_EPS) * g


def _outproj_kernel(oa_ref, o1_ref, o4_ref, o16_ref, lw1_ref, lw4_ref, lw16_ref, oc_ref, x_ref,
                    w_ref, mg_ref, g2_ref, wrh_ref, wrl_ref, br_ref, x1_ref, h_ref, route_ref, il_ref):
    tm = x_ref.shape[0]

    def interleaved(src_ref, dil, slot):
        for r in range(dil):
            for hf in range(CHUNK // LANES):
                piece = src_ref[0, :, r * CHUNK + hf * LANES:r * CHUNK + (hf + 1) * LANES].astype(F32)
                il_ref[slot * 2 + hf, pl.ds(r, tm // dil, stride=dil), :] = piece
        return jnp.concatenate([il_ref[slot * 2], il_ref[slot * 2 + 1]], axis=1)

    lw1 = lw1_ref[...]
    lw4, lw16 = interleaved(lw4_ref, 4, 0), interleaved(lw16_ref, 16, 1)
    o4, o16 = interleaved(o4_ref, 4, 2), interleaved(o16_ref, 16, 3)
    lwm = jnp.maximum(jnp.maximum(lw1, lw4), lw16)
    w1, w4, w16 = jnp.exp2(lw1 - lwm), jnp.exp2(lw4 - lwm), jnp.exp2(lw16 - lwm)
    ob = (w1 * o1_ref[...].astype(F32) + w4 * o4 + w16 * o16) / (w1 + w4 + w16)
    merged = jnp.concatenate([
        _rms(oa_ref[...].astype(F32), mg_ref[:, 0:CHUNK]).astype(BF16),
        _rms(ob, mg_ref[:, CHUNK:2 * CHUNK]).astype(BF16),
        _rms(oc_ref[...].astype(F32), mg_ref[:, 2 * CHUNK:]).astype(BF16),
    ], axis=-1)
    x1 = x_ref[...] + jnp.dot(merged, w_ref[...], preferred_element_type=F32)
    x1_ref[...] = x1
    h = _rms(x1, g2_ref[...])
    h_ref[...] = _pack_bf16_pair(h[:, :D_MODEL // 2], h[:, D_MODEL // 2:])

    hh = h.astype(BF16)
    hl = (h - hh.astype(F32)).astype(BF16)
    logits = (jnp.dot(hh, wrh_ref[...], preferred_element_type=F32)
              + jnp.dot(hl, wrh_ref[...], preferred_element_type=F32)
              + jnp.dot(hh, wrl_ref[...], preferred_element_type=F32)) + br_ref[...]
    lane = lax.broadcasted_iota(jnp.int32, logits.shape, 1)
    lane_f = lane.astype(F32)
    big = float(LANES)
    is_g = lane < N_GROUPS
    gl = jnp.where(is_g, logits, NEG)
    gmax = jnp.max(gl, axis=-1, keepdims=True)
    grp = jnp.min(jnp.where(gl == gmax, lane_f, big), axis=-1, keepdims=True)
    p_grp = 1.0 / jnp.sum(jnp.where(is_g, jnp.exp(logits - gmax), 0.0), axis=-1, keepdims=True)
    e_lo = N_GROUPS + EXPERTS_PER_GROUP * grp
    in_grp = (lane_f >= e_lo) & (lane_f < e_lo + EXPERTS_PER_GROUP)
    el = jnp.where(in_grp, logits, NEG)
    t1 = jnp.max(el, axis=-1, keepdims=True)
    i1 = jnp.min(jnp.where(el == t1, lane_f, big), axis=-1, keepdims=True)
    el2 = jnp.where(lane_f == i1, NEG, el)
    t2 = jnp.max(el2, axis=-1, keepdims=True)
    i2 = jnp.min(jnp.where(el2 == t2, lane_f, big), axis=-1, keepdims=True)
    e21 = jnp.exp(t2 - t1)
    gate1 = p_grp / (1.0 + e21)
    gate2 = p_grp * e21 / (1.0 + e21)
    route = jnp.where(lane == 0, i1 - N_GROUPS,
                      jnp.where(lane == 1, i2 - N_GROUPS,
                                jnp.where(lane == 2, gate1, jnp.where(lane == 3, gate2, 0.0))))
    route_ref[...] = route


def _outproj(oa, ob_parts, oc, x2, w_out, mix_g, g2, wr_hi, wr_lo, b_r, seq, tm=512):
    n = x2.shape[0]
    nsb = seq // tm
    (o1, lw1), (o4, lw4), (o16, lw16) = ob_parts
    o1, lw1 = o1.reshape(n, CHUNK), lw1.reshape(n, CHUNK)
    row = lambda w: pl.BlockSpec((tm, w), lambda i: (i, 0))
    full = lambda r, w: pl.BlockSpec((r, w), lambda i: (0, 0))
    dil = lambda d: pl.BlockSpec((1, tm // d, d * CHUNK), lambda i: (i // nsb, i % nsb, 0))
    return pl.pallas_call(
        _outproj_kernel,
        grid=(n // tm,),
        in_specs=[row(CHUNK), row(CHUNK), dil(4), dil(16), row(CHUNK), dil(4), dil(16),
                  row(2 * CHUNK), row(D_MODEL),
                  full(D_MODEL, D_MODEL), full(1, D_MODEL), full(1, D_MODEL),
                  full(D_MODEL, LANES), full(D_MODEL, LANES), full(1, LANES)],
        out_specs=[row(D_MODEL), row(D_MODEL // 2), row(LANES)],
        out_shape=[jax.ShapeDtypeStruct((n, D_MODEL), F32),
                   jax.ShapeDtypeStruct((n, D_MODEL // 2), jnp.uint32),
                   jax.ShapeDtypeStruct((n, LANES), F32)],
        scratch_shapes=[pltpu.VMEM((8, tm, LANES), F32)],
        compiler_params=_cparams(("arbitrary",)),
        name="outproj_router",
    )(oa, o1, o4, o16, lw1, lw4, lw16, oc, x2, w_out, mix_g, g2, wr_hi, wr_lo, b_r)


META_ROWS = 384


def _rank_kernel(route_ref, dest_ref, meta_ref, cnt_s, base_s, tri_s, *, tm):
    phase = pl.program_id(0)
    i = pl.program_id(1)
    lane = lax.broadcasted_iota(jnp.int32, (tm, LANES), 1)
    lane_f = lane.astype(F32)
    route = route_ref[...]
    oh0 = jnp.where(lane_f == route[:, 0:1], 1.0, 0.0)
    oh1 = jnp.where(lane_f == route[:, 1:2], 1.0, 0.0)
    oh = oh0 + oh1
    colsum = jnp.sum(oh, axis=0, keepdims=True)

    @pl.when((phase == 0) & (i == 0))
    def _():
        cnt_s[...] = jnp.zeros(cnt_s.shape, F32)
        r = lax.broadcasted_iota(jnp.int32, (tm, tm), 0)
        c = lax.broadcasted_iota(jnp.int32, (tm, tm), 1)
        tri_s[...] = jnp.where(c < r, 1.0, 0.0).astype(BF16)

    @pl.when(phase == 0)
    def _():
        cnt_s[...] = cnt_s[...] + colsum

    @pl.when((phase == 1) & (i == 0))
    def _():
        lane8 = lax.broadcasted_iota(jnp.int32, cnt_s.shape, 1)
        padded = jnp.floor((cnt_s[...] + (MOE_BLOCK - 1)) * (1.0 / MOE_BLOCK)) * MOE_BLOCK
        ends = padded
        for sh in (1, 2, 4, 8, 16):
            ends = ends + jnp.where(lane8 >= sh, pltpu.roll(ends, sh, 1), 0.0)
        base_s[...] = ends - padded
        pos = lax.broadcasted_iota(jnp.int32, (META_ROWS, LANES), 0).astype(F32) * MOE_BLOCK
        lane_m = lax.broadcasted_iota(jnp.int32, (META_ROWS, LANES), 1)
        ends_row = ends[0:1, :]
        hit = jnp.where((lane_m < N_EXPERTS) & (ends_row <= pos), 1.0, 0.0)
        blk_e = jnp.minimum(jnp.sum(hit, axis=-1, keepdims=True), N_EXPERTS - 1.0)
        total = jnp.sum(jnp.where(lane_m[0:1] == N_EXPERTS - 1, ends_row, 0.0), axis=-1, keepdims=True)
        row_m = lax.broadcasted_iota(jnp.int32, (META_ROWS, 1), 0)
        meta = jnp.where(row_m == META_ROWS - 1, total * (1.0 / MOE_BLOCK), blk_e)
        meta_ref[...] = meta.astype(jnp.int32)

    @pl.when(phase == 1)
    def _():
        prior = jnp.dot(tri_s[...], oh.astype(BF16), preferred_element_type=F32)
        val = base_s[0:1, :] + prior
        d0 = jnp.sum(oh0 * val, axis=-1, keepdims=True)
        d1 = jnp.sum(oh1 * val, axis=-1, keepdims=True)
        dest_ref[...] = jnp.where(lane == 0, d0, jnp.where(lane == 1, d1, 0.0)).astype(jnp.int32)
        base_s[...] = base_s[...] + colsum


def _rank(route, tm=512):
    n = route.shape[0]
    return pl.pallas_call(
        functools.partial(_rank_kernel, tm=tm),
        grid=(2, n // tm),
        in_specs=[pl.BlockSpec((tm, LANES), lambda p, i: (i, 0))],
        out_specs=[pl.BlockSpec((tm, LANES), lambda p, i: (i * p, 0)),
                   pl.BlockSpec((META_ROWS, 1), lambda p, i: (0, 0))],
        out_shape=[jax.ShapeDtypeStruct((n, LANES), jnp.int32),
                   jax.ShapeDtypeStruct((META_ROWS, 1), jnp.int32)],
        scratch_shapes=[pltpu.VMEM((8, LANES), F32), pltpu.VMEM((8, LANES), F32), pltpu.VMEM((tm, tm), BF16)],
        compiler_params=_cparams(("arbitrary", "arbitrary")),
        name="moe_rank",
    )(route)


def _dispatch_kernel(dest_ref, h_ref, xs_in_hbm, xs_hbm, sem, *, n, tc):
    del xs_in_hbm
    t0 = pl.program_id(0) * tc

    def body(j, carry):
        src = h_ref.at[pl.ds(j, 1)]
        pltpu.make_async_copy(src, xs_hbm.at[pl.ds(dest_ref[t0 + j], 1)], sem.at[0]).start(priority=0)
        pltpu.make_async_copy(src, xs_hbm.at[pl.ds(dest_ref[n + t0 + j], 1)], sem.at[0]).start(priority=1)
        return carry

    lax.fori_loop(0, tc, body, 0, unroll=8)
    for _ in range(2):
        pltpu.make_async_copy(h_ref, xs_hbm.at[pl.ds(0, tc)], sem.at[0]).wait()


def _dispatch(dest_flat, h, cap, tc=256):
    n, w = h.shape
    zeros = jnp.zeros((cap, w), h.dtype)
    return pl.pallas_call(
        functools.partial(_dispatch_kernel, n=n, tc=tc),
        grid_spec=pltpu.PrefetchScalarGridSpec(
            num_scalar_prefetch=1,
            grid=(n // tc,),
            in_specs=[pl.BlockSpec((tc, w), lambda i, d: (i, 0)), pl.BlockSpec(memory_space=pl.ANY)],
            out_specs=pl.BlockSpec(memory_space=pl.ANY),
            scratch_shapes=[pltpu.SemaphoreType.DMA((1,))],
        ),
        out_shape=jax.ShapeDtypeStruct((cap, w), h.dtype),
        input_output_aliases={2: 0},
        compiler_params=_cparams(("arbitrary",)),
        name="moe_dispatch",
    )(dest_flat, h, zeros)


def _expert_kernel(blk_e_ref, nact_ref, xs_ref, wg_ref, wu_ref, wd_ref, y_ref, wg_s, wu_s, wd_s):
    i = pl.program_id(0)
    active = i < nact_ref[0]
    changed = (i == 0) | (blk_e_ref[i] != blk_e_ref[jnp.maximum(i - 1, 0)])

    @pl.when(active & changed)
    def _():
        wg_s[...] = wg_ref[...].astype(BF16)
        wu_s[...] = wu_ref[...].astype(BF16)
        wd_s[...] = wd_ref[...].astype(BF16)

    @pl.when(active)
    def _():
        hi, lo = _unpack_bf16_pair(xs_ref[...])
        xb = jnp.concatenate([hi.astype(BF16), lo.astype(BF16)], axis=1)
        g = jnp.dot(xb, wg_s[...], preferred_element_type=F32)
        u = jnp.dot(xb, wu_s[...], preferred_element_type=F32)
        a = (g * jax.nn.sigmoid(g) * u).astype(BF16)
        y = jnp.dot(a, wd_s[...], preferred_element_type=F32)
        y_ref[...] = _pack_bf16_pair(y[:, :D_MODEL // 2], y[:, D_MODEL // 2:])

    @pl.when(jnp.logical_not(active))
    def _():
        y_ref[...] = jnp.zeros(y_ref.shape, y_ref.dtype)


def _experts(blk_e, nact, xs, w_gate, w_up, w_down, layer):
    cap = xs.shape[0]
    n_blk = cap // MOE_BLOCK

    def blk(i, be, na):
        return jnp.minimum(i, na[0] - 1)

    return pl.pallas_call(
        _expert_kernel,
        grid_spec=pltpu.PrefetchScalarGridSpec(
            num_scalar_prefetch=2,
            grid=(n_blk,),
            in_specs=[
                pl.BlockSpec((MOE_BLOCK, D_MODEL // 2), lambda i, be, na: (blk(i, be, na), 0)),
                pl.BlockSpec((None, None, D_MODEL, D_EXPERT), lambda i, be, na: (layer, be[blk(i, be, na)], 0, 0)),
                pl.BlockSpec((None, None, D_MODEL, D_EXPERT), lambda i, be, na: (layer, be[blk(i, be, na)], 0, 0)),
                pl.BlockSpec((None, None, D_EXPERT, D_MODEL), lambda i, be, na: (layer, be[blk(i, be, na)], 0, 0)),
            ],
            out_specs=pl.BlockSpec((MOE_BLOCK, D_MODEL // 2), lambda i, be, na: (i, 0)),
            scratch_shapes=[pltpu.VMEM((D_MODEL, D_EXPERT), BF16), pltpu.VMEM((D_MODEL, D_EXPERT), BF16),
                            pltpu.VMEM((D_EXPERT, D_MODEL), BF16)],
        ),
        out_shape=jax.ShapeDtypeStruct((cap, D_MODEL // 2), jnp.uint32),
        compiler_params=_cparams(("arbitrary",)),
        name="moe_experts",
    )(blk_e, nact, xs, w_gate, w_up, w_down)


def _combine_kernel(dest_ref, x1_ref, route_ref, y_hbm, o_ref, yb, sem, *, n, tc):
    t0 = pl.program_id(0) * tc

    def body(j, carry):
        pltpu.make_async_copy(y_hbm.at[pl.ds(dest_ref[t0 + j], 1)], yb.at[0, pl.ds(j, 1)],
                              sem.at[0]).start(priority=0)
        pltpu.make_async_copy(y_hbm.at[pl.ds(dest_ref[n + t0 + j], 1)], yb.at[1, pl.ds(j, 1)],
                              sem.at[0]).start(priority=1)
        return carry

    lax.fori_loop(0, tc, body, 0, unroll=8)
    for k in range(2):
        pltpu.make_async_copy(y_hbm.at[pl.ds(0, tc)], yb.at[k], sem.at[0]).wait()
    route = route_ref[...]
    hi0, lo0 = _unpack_bf16_pair(yb[0])
    hi1, lo1 = _unpack_bf16_pair(yb[1])
    half = D_MODEL // 2
    o_ref[:, :half] = x1_ref[:, :half] + route[:, 2:3] * hi0 + route[:, 3:4] * hi1
    o_ref[:, half:] = x1_ref[:, half:] + route[:, 2:3] * lo0 + route[:, 3:4] * lo1


def _combine(dest_flat, x1, route, y, tc=256):
    n = x1.shape[0]
    return pl.pallas_call(
        functools.partial(_combine_kernel, n=n, tc=tc),
        grid_spec=pltpu.PrefetchScalarGridSpec(
            num_scalar_prefetch=1,
            grid=(n // tc,),
            in_specs=[pl.BlockSpec((tc, D_MODEL), lambda i, d: (i, 0)),
                      pl.BlockSpec((tc, LANES), lambda i, d: (i, 0)),
                      pl.BlockSpec(memory_space=pl.ANY)],
            out_specs=pl.BlockSpec((tc, D_MODEL), lambda i, d: (i, 0)),
            scratch_shapes=[pltpu.VMEM((2, tc, D_MODEL // 2), jnp.uint32), pltpu.SemaphoreType.DMA((1,))],
        ),
        out_shape=jax.ShapeDtypeStruct((n, D_MODEL), F32),
        compiler_params=_cparams(("arbitrary",)),
        name="moe_combine",
    )(dest_flat, x1, route, y)


def _rope_tables(seq):
    n_freq = HEAD_DIM // 4
    t = np.arange(seq)
    inv_freq = jnp.asarray(ROPE_THETA, F32) ** (-jnp.arange(n_freq, dtype=F32) / n_freq)
    pos = jnp.stack([jnp.asarray(t // GRID_W, F32), jnp.asarray(t % GRID_W, F32)], axis=1)
    ang = pos[:, :, None] * inv_freq
    cos = jnp.broadcast_to(jnp.cos(ang)[:, :, None, :], (seq, 2, 2, n_freq)).reshape(seq, HEAD_DIM)
    sin = jnp.broadcast_to(jnp.sin(ang)[:, :, None, :], (seq, 2, 2, n_freq)).reshape(seq, HEAD_DIM)
    first_half = (np.arange(HEAD_DIM) % (2 * n_freq)) < n_freq
    sa = jnp.where(first_half, -sin, 0.0)
    sb = jnp.where(first_half, 0.0, sin)
    rep = CHUNK // HEAD_DIM
    return jnp.tile(cos, (1, rep)), jnp.tile(sa, (1, rep)), jnp.tile(sb, (1, rep))


def _prep_w_in(w):
    base = w[:, :2048]
    kc0, kc1 = w[:, 2048:2112], w[:, 2112:2176]
    vc0, vc1 = w[:, 2176:2240], w[:, 2240:2304]
    z = jnp.zeros_like(vc0)
    return jnp.concatenate([base, kc0, kc0, kc1, kc1, vc0, z, vc1, z], axis=1).astype(BF16)


def _prep_gain(qk_g):
    qs = SCALE * LOG2E
    ones = jnp.ones((CHUNK,), F32)
    t4 = lambda g: jnp.tile(g, CHUNK // HEAD_DIM)
    return jnp.concatenate([
        t4(qk_g[0, 0]) * qs, t4(qk_g[0, 1]), ones,
        t4(qk_g[1, 0]) * qs, t4(qk_g[1, 1]), ones,
        t4(qk_g[2, 0]) * qs, t4(qk_g[2, 0]) * qs, t4(qk_g[2, 1]), ones,
    ])[None, :]


def _split_hi_lo(w):
    hi = w.astype(BF16)
    lo = (w - hi.astype(F32)).astype(BF16)
    return hi, lo


def _mixer_layer(x2, bsz, seq, tables, norm1_g, w_in, qk_g, rpb, mix_g, w_out, norm2_g, w_rg, b_rg, w_re, b_re):
    n, d = x2.shape
    cos_t, sa_t, sb_t, bd = tables
    proj, pb4, pb16 = _inproj(x2, norm1_g[None, :], _prep_w_in(w_in), _prep_gain(qk_g), bd, cos_t, sa_t, sb_t, seq)
    proj3 = proj.reshape(bsz, seq, PROJ_W)
    oa = _na_attention(proj3, _na_bias_table(rpb)).reshape(n, CHUNK)
    ob_parts = [_dilated_branch(proj3[:, None], CH_QB, 1, 4), _dilated_branch(pb4, 0, 1, 4),
                _dilated_branch(pb16, 0, 2, 2)]
    oc = _gqa_attention(proj3).reshape(n, 2 * CHUNK)

    w_r = jnp.zeros((d, LANES), F32)
    w_r = w_r.at[:, :N_GROUPS].set(w_rg).at[:, N_GROUPS:N_GROUPS + N_EXPERTS].set(w_re)
    b_r = jnp.zeros((1, LANES), F32)
    b_r = b_r.at[0, :N_GROUPS].set(b_rg).at[0, N_GROUPS:N_GROUPS + N_EXPERTS].set(b_re)
    wr_hi, wr_lo = _split_hi_lo(w_r)
    return _outproj(oa, ob_parts, oc, x2, w_out.astype(BF16), mix_g[None, :], norm2_g[None, :], wr_hi, wr_lo, b_r,
                    seq)


def _moe_layer(x1, h, route, w_gate, w_up, w_down, layer):
    n = x1.shape[0]
    cap = 2 * n + N_EXPERTS * MOE_BLOCK
    n_blk = cap // MOE_BLOCK
    dest, meta = _rank(route)
    dest_flat = dest[:, :2].T.reshape(2 * n)
    xs = _dispatch(dest_flat, h, cap)
    y = _experts(meta[:n_blk, 0], meta[META_ROWS - 1:, 0], xs, w_gate, w_up, w_down, layer)
    return _combine(dest_flat, x1, route, y)


def _tables(seq):
    hd = np.arange(CHUNK) // HEAD_DIM
    return (*_rope_tables(seq), jnp.asarray(hd[:, None] == hd[None, :], BF16))


@jax.jit
def kernel(x, norm1_g, w_in, qk_norm_g, na_rpb, mix_norm_g, w_out, norm2_g, w_router_group, b_router_group,
           w_router_expert, b_router_expert, w_gate, w_up, w_down):
    bsz, seq, d = x.shape
    tables = _tables(seq)
    x2 = x.reshape(bsz * seq, d)
    for l in range(DEPTH):
        x1, h, route = _mixer_layer(x2, bsz, seq, tables, norm1_g[l], w_in[l], qk_norm_g[l], na_rpb[l],
                                    mix_norm_g[l], w_out[l], norm2_g[l], w_router_group[l], b_router_group[l],
                                    w_router_expert[l], b_router_expert[l])
        x2 = _moe_layer(x1, h, route, w_gate, w_up, w_down, l)
    return x2.reshape(bsz, seq, d)
```

```python
import functools
import math

import numpy as np
import jax
import jax.numpy as jnp
from jax import lax
from jax.experimental import pallas as pl
from jax.experimental.pallas import tpu as pltpu

F32 = jnp.float32
BF16 = jnp.bfloat16

D_MODEL = 1024
DEPTH = 4
GRID_W = 64
HEAD_DIM = 64
RMS_EPS = 1e-6
NEG = -1e30
LOG2E = math.log2(math.e)
SCALE = HEAD_DIM ** -0.5

NA_ROWS = 8
NA_COLS = 16
DILATIONS = (1, 4, 16)
DIL_HALF = 64
ROPE_THETA = 10000.0

N_GROUPS = 4
EXPERTS_PER_GROUP = 8
N_EXPERTS = 32
D_EXPERT = 512
MOE_BLOCK = 256

LANES = 128
SUBLANES = 8
CHUNK = 256
CH_QA, CH_KA, CH_VA, CH_QB, CH_KB, CH_VB, CH_QC, CH_KC, CH_VC = 0, 1, 2, 3, 4, 5, 6, 8, 9
N_CHUNKS = 10
PROJ_W = N_CHUNKS * CHUNK
NORM_CHUNKS = (CH_QA, CH_KA, CH_QB, CH_KB, CH_QC, CH_QC + 1, CH_KC)
ROPE_CHUNKS = (CH_QC, CH_QC + 1, CH_KC)

VMEM_LIMIT = 56 * 1024 * 1024

NT_DIMS = (((1,), (1,)), ((), ()))


def _cparams(sem):
    return pltpu.CompilerParams(dimension_semantics=sem, vmem_limit_bytes=VMEM_LIMIT)


INPROJ_PARTS = 2


def _inproj_kernel(x_ref, g1_ref, w_ref, gain_ref, bd_ref, cos_ref, sa_ref, sb_ref, o_ref, o4_ref, o16_ref, stage_ref):
    tm = x_ref.shape[0]
    tp = tm // INPROJ_PARTS
    part_rows = [slice(i * tp, (i + 1) * tp) for i in range(INPROJ_PARTS)]
    chunk_cols = [slice(c * CHUNK, (c + 1) * CHUNK) for c in range(N_CHUNKS)]
    n_stage = 3 * CHUNK // LANES

    normed = []
    for rows in part_rows:
        x = x_ref[rows, :]
        ms = jnp.mean(x * x, axis=-1, keepdims=True)
        normed.append((x * lax.rsqrt(ms + RMS_EPS) * g1_ref[...]).astype(BF16))
    projs = [jnp.dot(h, w_ref[...], preferred_element_type=F32) for h in normed]
    for part, (rows, proj) in enumerate(zip(part_rows, projs)):
        sumsq = {c: jnp.dot(jnp.square(proj[:, chunk_cols[c]]).astype(BF16), bd_ref[...],
                            preferred_element_type=F32) for c in NORM_CHUNKS}
        for c in range(N_CHUNKS):
            cols = chunk_cols[c]
            p = proj[:, cols]
            if c in NORM_CHUNKS:
                p = p * lax.rsqrt(sumsq[c] * (1.0 / HEAD_DIM) + RMS_EPS) * gain_ref[:, cols]
            if c in ROPE_CHUNKS:
                p = (p * cos_ref[rows, :] + pltpu.roll(p, CHUNK - 16, 1) * sa_ref[rows, :]
                     + pltpu.roll(p, 16, 1) * sb_ref[rows, :])
            if c == CH_VC:
                lane = lax.broadcasted_iota(jnp.int32, p.shape, 1)
                p = jnp.where(lane % LANES >= HEAD_DIM, 1.0, p)
            o_ref[rows, cols] = p.astype(BF16)
            if c in (CH_QB, CH_KB, CH_VB):
                for hf in range(CHUNK // LANES):
                    lane0 = ((c - CH_QB) * (CHUNK // LANES) + hf) * LANES
                    slot = part * n_stage + lane0 // LANES
                    stage_ref[slot] = p[:, hf * LANES:(hf + 1) * LANES]
                    for dil, od_ref in ((4, o4_ref), (16, o16_ref)):
                        nr = tp // dil
                        for r in range(dil):
                            picked = stage_ref[slot, pl.ds(r, nr, stride=dil), :]
                            od_ref[0, r, part * nr:(part + 1) * nr, lane0:lane0 + LANES] = picked.astype(BF16)


def _inproj(x2, g1, w, gain, bd, cos_t, sa_t, sb_t, seq, tm=512):
    n = x2.shape[0]
    nsb = seq // tm
    bsz = n // seq
    dil_spec = lambda dil: pl.BlockSpec((1, dil, tm // dil, 3 * CHUNK), lambda i: (i // nsb, 0, i % nsb, 0))
    dil_shape = lambda dil: jax.ShapeDtypeStruct((bsz, dil, seq // dil, 3 * CHUNK), BF16)
    return pl.pallas_call(
        _inproj_kernel,
        grid=(n // tm,),
        in_specs=[
            pl.BlockSpec((tm, D_MODEL), lambda i: (i, 0)),
            pl.BlockSpec((1, D_MODEL), lambda i: (0, 0)),
            pl.BlockSpec((D_MODEL, PROJ_W), lambda i: (0, 0)),
            pl.BlockSpec((1, PROJ_W), lambda i: (0, 0)),
            pl.BlockSpec((CHUNK, CHUNK), lambda i: (0, 0)),
            pl.BlockSpec((tm, CHUNK), lambda i: (i % nsb, 0)),
            pl.BlockSpec((tm, CHUNK), lambda i: (i % nsb, 0)),
            pl.BlockSpec((tm, CHUNK), lambda i: (i % nsb, 0)),
        ],
        out_specs=[pl.BlockSpec((tm, PROJ_W), lambda i: (i, 0)), dil_spec(4), dil_spec(16)],
        out_shape=[jax.ShapeDtypeStruct((n, PROJ_W), BF16), dil_shape(4), dil_shape(16)],
        scratch_shapes=[pltpu.VMEM((INPROJ_PARTS * 3 * CHUNK // LANES, tm // INPROJ_PARTS, LANES), F32)],
        compiler_params=_cparams(("arbitrary",)),
        name="inproj",
    )(x2, g1, w, gain, bd, cos_t, sa_t, sb_t)


def _stack_head_pair(qp, lo):
    zero = jnp.zeros_like(qp)
    return jnp.concatenate([jnp.where(lo, qp, zero), jnp.where(lo, zero, qp)], axis=0)


def _pack_bf16_pair(a, b):
    ua = lax.bitcast_convert_type(a.astype(BF16).astype(F32), jnp.uint32)
    ub = lax.bitcast_convert_type(b.astype(BF16).astype(F32), jnp.uint32)
    return ua | (ub >> 16)


def _unpack_bf16_pair(u):
    hi = lax.bitcast_convert_type(u & jnp.uint32(0xFFFF0000), F32)
    lo = lax.bitcast_convert_type(u << 16, F32)
    return hi, lo


def _lo_mask(rows):
    return lax.broadcasted_iota(jnp.int32, (rows, LANES), 1) < HEAD_DIM


NA_ROW_GROUP = 4


def _na_kernel(q_ref, k_ref, v_ref, bias_ref, o_ref, *, rows_per_step, n_rows):
    i = pl.program_id(1)
    lo = _lo_mask(GRID_W)
    win = NA_ROWS * GRID_W

    def window(rl):
        r = i * rows_per_step + rl
        r0 = jnp.clip(r - NA_ROWS // 2, 0, n_rows - NA_ROWS)
        return pl.multiple_of(r0 * GRID_W, GRID_W), r - r0

    for g0 in range(0, rows_per_step, NA_ROW_GROUP):
        chains = [(rl, p) for rl in range(g0, g0 + NA_ROW_GROUP) for p in range(2)]
        scores = []
        for rl, p in chains:
            start, rr = window(rl)
            cols = slice(p * LANES, (p + 1) * LANES)
            qs = _stack_head_pair(q_ref[0, rl * GRID_W:(rl + 1) * GRID_W, cols], lo)
            s = lax.dot_general(qs, k_ref[0, pl.ds(start, win), cols], NT_DIMS, preferred_element_type=F32)
            scores.append(s + bias_ref[p, rr])
        stats = []
        for s in scores:
            e = jnp.exp2(s - jnp.max(s, axis=-1, keepdims=True))
            stats.append((e, jnp.sum(e, axis=-1, keepdims=True)))
        for (rl, p), (e, l) in zip(chains, stats):
            start, _ = window(rl)
            cols = slice(p * LANES, (p + 1) * LANES)
            pv = jnp.dot(e.astype(BF16), v_ref[0, pl.ds(start, win), cols], preferred_element_type=F32)
            o = pv / l
            o_ref[0, rl * GRID_W:(rl + 1) * GRID_W, cols] = jnp.where(lo, o[:GRID_W], o[GRID_W:]).astype(BF16)


def _na_attention(proj3, bias, rows_per_step=8):
    b, s, _ = proj3.shape
    n_rows = s // GRID_W
    tq = rows_per_step * GRID_W
    return pl.pallas_call(
        functools.partial(_na_kernel, rows_per_step=rows_per_step, n_rows=n_rows),
        grid=(b, n_rows // rows_per_step),
        in_specs=[
            pl.BlockSpec((1, tq, CHUNK), lambda bi, i: (bi, i, CH_QA)),
            pl.BlockSpec((1, s, CHUNK), lambda bi, i: (bi, 0, CH_KA)),
            pl.BlockSpec((1, s, CHUNK), lambda bi, i: (bi, 0, CH_VA)),
            pl.BlockSpec((2, NA_ROWS, 2 * GRID_W, NA_ROWS * GRID_W), lambda bi, i: (0, 0, 0, 0)),
        ],
        out_specs=pl.BlockSpec((1, tq, CHUNK), lambda bi, i: (bi, i, 0)),
        out_shape=jax.ShapeDtypeStruct((b, s, CHUNK), BF16),
        compiler_params=_cparams(("arbitrary", "arbitrary")),
        name="na_attention",
    )(proj3, proj3, proj3, bias)


def _na_bias_table(rpb):
    c = np.arange(GRID_W)
    cs = np.clip(c - NA_COLS // 2, 0, GRID_W - NA_COLS)
    kc = np.arange(GRID_W)
    inwin = (kc[None, :] >= cs[:, None]) & (kc[None, :] < cs[:, None] + NA_COLS)
    dc = kc[None, :] - c[:, None] + NA_COLS - 1
    rr = np.arange(NA_ROWS)
    dr = rr[None, :] - rr[:, None] + NA_ROWS - 1
    sel_r = jnp.asarray(dr[:, :, None] == np.arange(2 * NA_ROWS - 1), F32)
    sel_c = jnp.asarray((dc[:, :, None] == np.arange(2 * NA_COLS - 1)) & inwin[:, :, None], F32)
    tab = jnp.einsum("rka,hab,cjb->hrckj", sel_r, rpb, sel_c, precision=lax.Precision.HIGHEST)
    tab = jnp.where(inwin[None, None, :, None, :], tab * LOG2E, NEG)
    tab = tab.reshape(2, 2, NA_ROWS, GRID_W, NA_ROWS * GRID_W)
    return tab.transpose(0, 2, 1, 3, 4).reshape(2, NA_ROWS, 2 * GRID_W, NA_ROWS * GRID_W)


DIL_TQ = 128
DIL_WIN = DIL_TQ + 2 * DIL_HALF


def _dil_kernel(q_ref, k_ref, v_ref, bias_ref, o_ref, lw_ref, *, length, rg, ug, nb):
    u0 = pl.program_id(2) * ug
    lo = _lo_mask(DIL_TQ)
    chains = [(r, ub, p) for r in range(rg) for ub in range(ug) for p in range(2)]

    def window(ub):
        gb = u0 + ub
        ws = pl.multiple_of(jnp.clip(gb * DIL_TQ - DIL_HALF, 0, length - DIL_WIN), DIL_HALF)
        case = jnp.where(gb == 0, 0, jnp.where(gb == nb - 1, 2, 1))
        return ws, case

    scores = []
    for r, ub, p in chains:
        ws, case = window(ub)
        cols = slice(p * LANES, (p + 1) * LANES)
        qs = _stack_head_pair(q_ref[r, ub * DIL_TQ:(ub + 1) * DIL_TQ, cols], lo)
        s = lax.dot_general(qs, k_ref[r, pl.ds(ws, DIL_WIN), cols], NT_DIMS, preferred_element_type=F32)
        scores.append(s + bias_ref[case, p])
    stats = []
    for s in scores:
        m = jnp.max(s, axis=-1, keepdims=True)
        e = jnp.exp2(s - m)
        stats.append((m, e, jnp.sum(e, axis=-1, keepdims=True)))
    for (r, ub, p), (m, e, l) in zip(chains, stats):
        ws, _ = window(ub)
        cols = slice(p * LANES, (p + 1) * LANES)
        pv = jnp.dot(e.astype(BF16), v_ref[r, pl.ds(ws, DIL_WIN), cols], preferred_element_type=F32)
        o = pv / l
        lw = jnp.broadcast_to(m + jnp.log2(l), (2 * DIL_TQ, LANES))
        rows = slice(ub * DIL_TQ, (ub + 1) * DIL_TQ)
        ocols = slice(r * CHUNK + p * LANES, r * CHUNK + (p + 1) * LANES)
        o_ref[0, rows, ocols] = jnp.where(lo, o[:DIL_TQ], o[DIL_TQ:]).astype(BF16)
        lw_ref[0, rows, ocols] = jnp.where(lo, lw[:DIL_TQ], lw[DIL_TQ:])


def _dil_bias_table(dil):
    slopes = np.exp2(-8.0 * np.arange(1, 5) / 4.0)
    qi = np.arange(DIL_TQ)[:, None]
    kj = np.arange(DIL_WIN)[None, :]
    tab = np.zeros((3, 4, DIL_TQ, DIL_WIN), np.float32)
    for case, shift in enumerate((0, -DIL_HALF, -2 * DIL_HALF)):
        rel = kj + shift - qi
        for h in range(4):
            tab[case, h] = np.where(np.abs(rel) <= DIL_HALF, -slopes[h] * dil * np.abs(rel) * LOG2E, NEG)
    return tab.reshape(3, 2, 2 * DIL_TQ, DIL_WIN)


def _dilated_branch(src, q_chunk, rg, ug):
    b, dil, length, _ = src.shape
    nb = length // DIL_TQ
    o, lw = pl.pallas_call(
        functools.partial(_dil_kernel, length=length, rg=rg, ug=ug, nb=nb),
        grid=(b, dil // rg, nb // ug),
        in_specs=[
            pl.BlockSpec((None, rg, ug * DIL_TQ, CHUNK), lambda bi, r, u: (bi, r, u, q_chunk)),
            pl.BlockSpec((None, rg, length, CHUNK), lambda bi, r, u: (bi, r, 0, q_chunk + 1)),
            pl.BlockSpec((None, rg, length, CHUNK), lambda bi, r, u: (bi, r, 0, q_chunk + 2)),
            pl.BlockSpec((3, 2, 2 * DIL_TQ, DIL_WIN), lambda bi, r, u: (0, 0, 0, 0)),
        ],
        out_specs=[
            pl.BlockSpec((1, ug * DIL_TQ, rg * CHUNK), lambda bi, r, u: (bi, u, r)),
            pl.BlockSpec((1, ug * DIL_TQ, rg * CHUNK), lambda bi, r, u: (bi, u, r)),
        ],
        out_shape=[
            jax.ShapeDtypeStruct((b, length, dil * CHUNK), BF16),
            jax.ShapeDtypeStruct((b, length, dil * CHUNK), F32),
        ],
        compiler_params=_cparams(("arbitrary", "arbitrary", "arbitrary")),
        name=f"dilated_d{dil}",
    )(src, src, src, jnp.asarray(_dil_bias_table(dil)))
    return o, lw


TN_DIMS = (((0,), (0,)), ((), ()))


def _gqa_kernel(q_ref, k_ref, v_ref, o_ref, m_s, acc_s, s_s, *, tq, tk, seq):
    lo = _lo_mask(tq)
    n_kt = seq // tk
    qs = jnp.concatenate([_stack_head_pair(q_ref[0, :, j * LANES:(j + 1) * LANES], lo) for j in range(2)],
                         axis=0)
    m_s[...] = jnp.full(m_s.shape, NEG, F32)
    acc_s[...] = jnp.zeros(acc_s.shape, F32)

    def scores(kt, slot):
        k0 = pl.multiple_of(kt * tk, tk)
        s_s[slot] = lax.dot_general(k_ref[0, pl.ds(k0, tk), :], qs, NT_DIMS, preferred_element_type=F32)

    def softmax_values(kt, slot):
        k0 = pl.multiple_of(kt * tk, tk)
        s = s_s[slot]
        m_old = m_s[...]
        m_new = jnp.maximum(m_old, jnp.max(s, axis=0, keepdims=True))
        alpha = jnp.exp2(m_old - m_new)
        e = jnp.exp2((s - m_new[0:1, :]).astype(BF16))
        m_s[...] = m_new
        pv = lax.dot_general(v_ref[0, pl.ds(k0, tk), :], e, TN_DIMS, preferred_element_type=F32)
        acc_s[...] = alpha[0:1, :] * acc_s[...] + pv

    scores(0, 0)

    def body(j, carry):
        scores(2 * j + 1, 1)
        softmax_values(2 * j, 0)
        scores(2 * j + 2, 0)
        softmax_values(2 * j + 1, 1)
        return carry

    lax.fori_loop(0, n_kt // 2 - 1, body, 0, unroll=True)
    scores(n_kt - 1, 1)
    softmax_values(n_kt - 2, 0)
    softmax_values(n_kt - 1, 1)

    acc = acc_s[...]
    o_t = acc[:HEAD_DIM] / acc[HEAD_DIM:HEAD_DIM + 1]
    for j in range(2):
        pair_t = jnp.concatenate([o_t[:, (2 * j) * tq:(2 * j + 1) * tq], o_t[:, (2 * j + 1) * tq:(2 * j + 2) * tq]],
                                 axis=0)
        o_ref[0, :, j * LANES:(j + 1) * LANES] = pair_t.T.astype(BF16)


def _gqa_attention(proj3, tq=256, tk=256):
    b, s, _ = proj3.shape
    assert (s // tk) % 2 == 0 and s // tk >= 2
    return pl.pallas_call(
        functools.partial(_gqa_kernel, tq=tq, tk=tk, seq=s),
        grid=(b, 2, s // tq),
        in_specs=[
            pl.BlockSpec((1, tq, CHUNK), lambda bi, g, i: (bi, i, CH_QC + g)),
            pl.BlockSpec((1, s, LANES), lambda bi, g, i: (bi, 0, 2 * CH_KC + g)),
            pl.BlockSpec((1, s, LANES), lambda bi, g, i: (bi, 0, 2 * CH_VC + g)),
        ],
        out_specs=pl.BlockSpec((1, tq, CHUNK), lambda bi, g, i: (bi, i, g)),
        out_shape=jax.ShapeDtypeStruct((b, s, 2 * CHUNK), BF16),
        scratch_shapes=[
            pltpu.VMEM((8, 4 * tq), F32),
            pltpu.VMEM((LANES, 4 * tq), F32),
            pltpu.VMEM((2, tk, 4 * tq), F32),
        ],
        compiler_params=_cparams(("arbitrary", "arbitrary", "arbitrary")),
        name="gqa_attention",
    )(proj3, proj3, proj3)


def _rms(v, g):
    return v * lax.rsqrt(jnp.mean(v * v, axis=-1, keepdims=True) + RMS_EPS) * g


def _outproj_kernel(oa_ref, o1_ref, o4_ref, o16_ref, lw1_ref, lw4_ref, lw16_ref, oc_ref, x_ref,
                    w_ref, mg_ref, g2_ref, wrh_ref, wrl_ref, br_ref, x1_ref, h_ref, route_ref, cnt_ref, il_ref):
    tm = x_ref.shape[0]

    def interleaved(src_ref, dil, slot):
        for r in range(dil):
            for hf in range(CHUNK // LANES):
                piece = src_ref[0, :, r * CHUNK + hf * LANES:r * CHUNK + (hf + 1) * LANES].astype(F32)
                il_ref[slot * 2 + hf, pl.ds(r, tm // dil, stride=dil), :] = piece
        return jnp.concatenate([il_ref[slot * 2], il_ref[slot * 2 + 1]], axis=1)

    lw1 = lw1_ref[...]
    lw4, lw16 = interleaved(lw4_ref, 4, 0), interleaved(lw16_ref, 16, 1)
    o4, o16 = interleaved(o4_ref, 4, 2), interleaved(o16_ref, 16, 3)
    lwm = jnp.maximum(jnp.maximum(lw1, lw4), lw16)
    w1, w4, w16 = jnp.exp2(lw1 - lwm), jnp.exp2(lw4 - lwm), jnp.exp2(lw16 - lwm)
    ob = (w1 * o1_ref[...].astype(F32) + w4 * o4 + w16 * o16) / (w1 + w4 + w16)
    merged = jnp.concatenate([
        _rms(oa_ref[...].astype(F32), mg_ref[:, 0:CHUNK]).astype(BF16),
        _rms(ob, mg_ref[:, CHUNK:2 * CHUNK]).astype(BF16),
        _rms(oc_ref[...].astype(F32), mg_ref[:, 2 * CHUNK:]).astype(BF16),
    ], axis=-1)
    x1 = x_ref[...] + jnp.dot(merged, w_ref[...], preferred_element_type=F32)
    x1_ref[...] = x1
    h = _rms(x1, g2_ref[...])
    h_ref[...] = _pack_bf16_pair(h[:, :D_MODEL // 2], h[:, D_MODEL // 2:])

    hh = h.astype(BF16)
    hl = (h - hh.astype(F32)).astype(BF16)
    logits = (jnp.dot(hh, wrh_ref[...], preferred_element_type=F32)
              + jnp.dot(hl, wrh_ref[...], preferred_element_type=F32)
              + jnp.dot(hh, wrl_ref[...], preferred_element_type=F32)) + br_ref[...]
    lane = lax.broadcasted_iota(jnp.int32, logits.shape, 1)
    lane_f = lane.astype(F32)
    big = float(LANES)
    is_g = lane < N_GROUPS
    gl = jnp.where(is_g, logits, NEG)
    gmax = jnp.max(gl, axis=-1, keepdims=True)
    grp = jnp.min(jnp.where(gl == gmax, lane_f, big), axis=-1, keepdims=True)
    p_grp = 1.0 / jnp.sum(jnp.where(is_g, jnp.exp(logits - gmax), 0.0), axis=-1, keepdims=True)
    e_lo = N_GROUPS + EXPERTS_PER_GROUP * grp
    in_grp = (lane_f >= e_lo) & (lane_f < e_lo + EXPERTS_PER_GROUP)
    el = jnp.where(in_grp, logits, NEG)
    t1 = jnp.max(el, axis=-1, keepdims=True)
    i1 = jnp.min(jnp.where(el == t1, lane_f, big), axis=-1, keepdims=True)
    el2 = jnp.where(lane_f == i1, NEG, el)
    t2 = jnp.max(el2, axis=-1, keepdims=True)
    i2 = jnp.min(jnp.where(el2 == t2, lane_f, big), axis=-1, keepdims=True)
    e21 = jnp.exp(t2 - t1)
    gate1 = p_grp / (1.0 + e21)
    gate2 = p_grp * e21 / (1.0 + e21)
    route = jnp.where(lane == 0, i1 - N_GROUPS,
                      jnp.where(lane == 1, i2 - N_GROUPS,
                                jnp.where(lane == 2, gate1, jnp.where(lane == 3, gate2, 0.0))))
    route_ref[...] = route

    picked = jnp.where((lane_f == i1 - N_GROUPS) | (lane_f == i2 - N_GROUPS), 1.0, 0.0)

    @pl.when(pl.program_id(0) == 0)
    def _():
        cnt_ref[...] = jnp.zeros(cnt_ref.shape, F32)

    cnt_ref[...] = cnt_ref[...] + jnp.sum(picked, axis=0, keepdims=True)


def _outproj(oa, ob_parts, oc, x2, w_out, mix_g, g2, wr_hi, wr_lo, b_r, seq, tm=512):
    n = x2.shape[0]
    nsb = seq // tm
    (o1, lw1), (o4, lw4), (o16, lw16) = ob_parts
    o1, lw1 = o1.reshape(n, CHUNK), lw1.reshape(n, CHUNK)
    row = lambda w: pl.BlockSpec((tm, w), lambda i: (i, 0))
    full = lambda r, w: pl.BlockSpec((r, w), lambda i: (0, 0))
    dil = lambda d: pl.BlockSpec((1, tm // d, d * CHUNK), lambda i: (i // nsb, i % nsb, 0))
    return pl.pallas_call(
        _outproj_kernel,
        grid=(n // tm,),
        in_specs=[row(CHUNK), row(CHUNK), dil(4), dil(16), row(CHUNK), dil(4), dil(16),
                  row(2 * CHUNK), row(D_MODEL),
                  full(D_MODEL, D_MODEL), full(1, D_MODEL), full(1, D_MODEL),
                  full(D_MODEL, LANES), full(D_MODEL, LANES), full(1, LANES)],
        out_specs=[row(D_MODEL), row(D_MODEL // 2), row(LANES), full(8, LANES)],
        out_shape=[jax.ShapeDtypeStruct((n, D_MODEL), F32),
                   jax.ShapeDtypeStruct((n, D_MODEL // 2), jnp.uint32),
                   jax.ShapeDtypeStruct((n, LANES), F32),
                   jax.ShapeDtypeStruct((8, LANES), F32)],
        scratch_shapes=[pltpu.VMEM((8, tm, LANES), F32)],
        compiler_params=_cparams(("arbitrary",)),
        name="outproj_router",
    )(oa, o1, o4, o16, lw1, lw4, lw16, oc, x2, w_out, mix_g, g2, wr_hi, wr_lo, b_r)


META_ROWS = 384


def _rank_kernel(route_ref, cnt_ref, dest_ref, meta_ref, base_s, tri_s, *, tm):
    i = pl.program_id(0)
    lane = lax.broadcasted_iota(jnp.int32, (tm, LANES), 1)
    lane_f = lane.astype(F32)
    route = route_ref[...]
    oh0 = jnp.where(lane_f == route[:, 0:1], 1.0, 0.0)
    oh1 = jnp.where(lane_f == route[:, 1:2], 1.0, 0.0)
    oh = oh0 + oh1

    @pl.when(i == 0)
    def _():
        r = lax.broadcasted_iota(jnp.int32, (tm, tm), 0)
        c = lax.broadcasted_iota(jnp.int32, (tm, tm), 1)
        tri_s[...] = jnp.where(c < r, 1.0, 0.0).astype(BF16)
        lane8 = lax.broadcasted_iota(jnp.int32, cnt_ref.shape, 1)
        padded = jnp.floor((cnt_ref[...] + (MOE_BLOCK - 1)) * (1.0 / MOE_BLOCK)) * MOE_BLOCK
        ends = padded
        for sh in (1, 2, 4, 8, 16):
            ends = ends + jnp.where(lane8 >= sh, pltpu.roll(ends, sh, 1), 0.0)
        base_s[...] = ends - padded
        pos = lax.broadcasted_iota(jnp.int32, (META_ROWS, LANES), 0).astype(F32) * MOE_BLOCK
        lane_m = lax.broadcasted_iota(jnp.int32, (META_ROWS, LANES), 1)
        ends_row = ends[0:1, :]
        hit = jnp.where((lane_m < N_EXPERTS) & (ends_row <= pos), 1.0, 0.0)
        blk_e = jnp.minimum(jnp.sum(hit, axis=-1, keepdims=True), N_EXPERTS - 1.0)
        total = jnp.sum(jnp.where(lane_m[0:1] == N_EXPERTS - 1, ends_row, 0.0), axis=-1, keepdims=True)
        row_m = lax.broadcasted_iota(jnp.int32, (META_ROWS, 1), 0)
        meta = jnp.where(row_m == META_ROWS - 1, total * (1.0 / MOE_BLOCK), blk_e)
        meta_ref[...] = meta.astype(jnp.int32)

    prior = jnp.dot(tri_s[...], oh.astype(BF16), preferred_element_type=F32)
    val = base_s[0:1, :] + prior
    d0 = jnp.sum(oh0 * val, axis=-1, keepdims=True)
    d1 = jnp.sum(oh1 * val, axis=-1, keepdims=True)
    dest_ref[...] = jnp.where(lane == 0, d0, jnp.where(lane == 1, d1, 0.0)).astype(jnp.int32)
    base_s[...] = base_s[...] + jnp.sum(oh, axis=0, keepdims=True)


def _rank(route, counts, tm=512):
    n = route.shape[0]
    return pl.pallas_call(
        functools.partial(_rank_kernel, tm=tm),
        grid=(n // tm,),
        in_specs=[pl.BlockSpec((tm, LANES), lambda i: (i, 0)), pl.BlockSpec((8, LANES), lambda i: (0, 0))],
        out_specs=[pl.BlockSpec((tm, LANES), lambda i: (i, 0)),
                   pl.BlockSpec((META_ROWS, 1), lambda i: (0, 0))],
        out_shape=[jax.ShapeDtypeStruct((n, LANES), jnp.int32),
                   jax.ShapeDtypeStruct((META_ROWS, 1), jnp.int32)],
        scratch_shapes=[pltpu.VMEM((8, LANES), F32), pltpu.VMEM((tm, tm), BF16)],
        compiler_params=_cparams(("arbitrary",)),
        name="moe_rank",
    )(route, counts)


def _dispatch_kernel(dest_ref, h_ref, xs_in_hbm, xs_hbm, sem, *, n, tc):
    del xs_in_hbm
    t0 = pl.program_id(0) * tc

    def group(g, carry):
        base = pl.multiple_of(g * SUBLANES, SUBLANES)
        for u in range(SUBLANES):
            src = h_ref.at[pl.ds(base + u, 1)]
            pltpu.make_async_copy(src, xs_hbm.at[pl.ds(dest_ref[t0 + base + u], 1)], sem.at[0]).start(priority=0)
            pltpu.make_async_copy(src, xs_hbm.at[pl.ds(dest_ref[n + t0 + base + u], 1)], sem.at[0]).start(priority=1)
        return carry

    lax.fori_loop(0, tc // SUBLANES, group, 0)
    for _ in range(2):
        pltpu.make_async_copy(h_ref, xs_hbm.at[pl.ds(0, tc)], sem.at[0]).wait()


def _dispatch(dest_flat, h, slots, tc=256):
    n, w = h.shape
    cap = slots.shape[0]
    return pl.pallas_call(
        functools.partial(_dispatch_kernel, n=n, tc=tc),
        grid_spec=pltpu.PrefetchScalarGridSpec(
            num_scalar_prefetch=1,
            grid=(n // tc,),
            in_specs=[pl.BlockSpec((tc, w), lambda i, d: (i, 0)), pl.BlockSpec(memory_space=pl.ANY)],
            out_specs=pl.BlockSpec(memory_space=pl.ANY),
            scratch_shapes=[pltpu.SemaphoreType.DMA((1,))],
        ),
        out_shape=jax.ShapeDtypeStruct((cap, w), h.dtype),
        input_output_aliases={2: 0},
        compiler_params=_cparams(("arbitrary",)),
        name="moe_dispatch",
    )(dest_flat, h, slots)


def _expert_kernel(blk_e_ref, nact_ref, xs_ref, wg_ref, wu_ref, wd_ref, y_ref, wg_s, wu_s, wd_s):
    i = pl.program_id(0)
    active = i < nact_ref[0]
    changed = (i == 0) | (blk_e_ref[i] != blk_e_ref[jnp.maximum(i - 1, 0)])

    @pl.when(active & changed)
    def _():
        wg_s[...] = wg_ref[...].astype(BF16)
        wu_s[...] = wu_ref[...].astype(BF16)
        wd_s[...] = wd_ref[...].astype(BF16)

    @pl.when(active)
    def _():
        hi, lo = _unpack_bf16_pair(xs_ref[...])
        xb = jnp.concatenate([hi.astype(BF16), lo.astype(BF16)], axis=1)
        g = jnp.dot(xb, wg_s[...], preferred_element_type=F32)
        u = jnp.dot(xb, wu_s[...], preferred_element_type=F32)
        a = (g * jax.nn.sigmoid(g) * u).astype(BF16)
        y = jnp.dot(a, wd_s[...], preferred_element_type=F32)
        y_ref[...] = _pack_bf16_pair(y[:, :D_MODEL // 2], y[:, D_MODEL // 2:])

    @pl.when(jnp.logical_not(active))
    def _():
        y_ref[...] = jnp.zeros(y_ref.shape, y_ref.dtype)


def _experts(blk_e, nact, xs, w_gate, w_up, w_down, layer):
    cap = xs.shape[0]
    n_blk = cap // MOE_BLOCK

    def blk(i, be, na):
        return jnp.minimum(i, na[0] - 1)

    return pl.pallas_call(
        _expert_kernel,
        grid_spec=pltpu.PrefetchScalarGridSpec(
            num_scalar_prefetch=2,
            grid=(n_blk,),
            in_specs=[
                pl.BlockSpec((MOE_BLOCK, D_MODEL // 2), lambda i, be, na: (blk(i, be, na), 0)),
                pl.BlockSpec((None, None, D_MODEL, D_EXPERT), lambda i, be, na: (layer, be[blk(i, be, na)], 0, 0)),
                pl.BlockSpec((None, None, D_MODEL, D_EXPERT), lambda i, be, na: (layer, be[blk(i, be, na)], 0, 0)),
                pl.BlockSpec((None, None, D_EXPERT, D_MODEL), lambda i, be, na: (layer, be[blk(i, be, na)], 0, 0)),
            ],
            out_specs=pl.BlockSpec((MOE_BLOCK, D_MODEL // 2), lambda i, be, na: (i, 0)),
            scratch_shapes=[pltpu.VMEM((D_MODEL, D_EXPERT), BF16), pltpu.VMEM((D_MODEL, D_EXPERT), BF16),
                            pltpu.VMEM((D_EXPERT, D_MODEL), BF16)],
        ),
        out_shape=jax.ShapeDtypeStruct((cap, D_MODEL // 2), jnp.uint32),
        compiler_params=_cparams(("arbitrary",)),
        name="moe_experts",
    )(blk_e, nact, xs, w_gate, w_up, w_down)


def _combine_kernel(dest_ref, x1_ref, route_ref, y_hbm, o_ref, yb, sem, *, n, tc):
    t0 = pl.program_id(0) * tc

    def body(j, carry):
        pltpu.make_async_copy(y_hbm.at[pl.ds(dest_ref[t0 + j], 1)], yb.at[0, pl.ds(j, 1)],
                              sem.at[0]).start(priority=0)
        pltpu.make_async_copy(y_hbm.at[pl.ds(dest_ref[n + t0 + j], 1)], yb.at[1, pl.ds(j, 1)],
                              sem.at[0]).start(priority=1)
        return carry

    lax.fori_loop(0, tc, body, 0, unroll=8)
    for k in range(2):
        pltpu.make_async_copy(y_hbm.at[pl.ds(0, tc)], yb.at[k], sem.at[0]).wait()
    route = route_ref[...]
    hi0, lo0 = _unpack_bf16_pair(yb[0])
    hi1, lo1 = _unpack_bf16_pair(yb[1])
    half = D_MODEL // 2
    o_ref[:, :half] = x1_ref[:, :half] + route[:, 2:3] * hi0 + route[:, 3:4] * hi1
    o_ref[:, half:] = x1_ref[:, half:] + route[:, 2:3] * lo0 + route[:, 3:4] * lo1


def _combine(dest_flat, x1, route, y, tc=256):
    n = x1.shape[0]
    return pl.pallas_call(
        functools.partial(_combine_kernel, n=n, tc=tc),
        grid_spec=pltpu.PrefetchScalarGridSpec(
            num_scalar_prefetch=1,
            grid=(n // tc,),
            in_specs=[pl.BlockSpec((tc, D_MODEL), lambda i, d: (i, 0)),
                      pl.BlockSpec((tc, LANES), lambda i, d: (i, 0)),
                      pl.BlockSpec(memory_space=pl.ANY)],
            out_specs=pl.BlockSpec((tc, D_MODEL), lambda i, d: (i, 0)),
            scratch_shapes=[pltpu.VMEM((2, tc, D_MODEL // 2), jnp.uint32), pltpu.SemaphoreType.DMA((1,))],
        ),
        out_shape=jax.ShapeDtypeStruct((n, D_MODEL), F32),
        compiler_params=_cparams(("arbitrary",)),
        name="moe_combine",
    )(dest_flat, x1, route, y)


def _rope_tables(seq):
    n_freq = HEAD_DIM // 4
    t = np.arange(seq)
    inv_freq = jnp.asarray(ROPE_THETA, F32) ** (-jnp.arange(n_freq, dtype=F32) / n_freq)
    pos = jnp.stack([jnp.asarray(t // GRID_W, F32), jnp.asarray(t % GRID_W, F32)], axis=1)
    ang = pos[:, :, None] * inv_freq
    cos = jnp.broadcast_to(jnp.cos(ang)[:, :, None, :], (seq, 2, 2, n_freq)).reshape(seq, HEAD_DIM)
    sin = jnp.broadcast_to(jnp.sin(ang)[:, :, None, :], (seq, 2, 2, n_freq)).reshape(seq, HEAD_DIM)
    first_half = (np.arange(HEAD_DIM) % (2 * n_freq)) < n_freq
    sa = jnp.where(first_half, -sin, 0.0)
    sb = jnp.where(first_half, 0.0, sin)
    rep = CHUNK // HEAD_DIM
    return jnp.tile(cos, (1, rep)), jnp.tile(sa, (1, rep)), jnp.tile(sb, (1, rep))


def _prep_w_in(w):
    base = w[:, :2048]
    kc0, kc1 = w[:, 2048:2112], w[:, 2112:2176]
    vc0, vc1 = w[:, 2176:2240], w[:, 2240:2304]
    z = jnp.zeros_like(vc0)
    return jnp.concatenate([base, kc0, kc0, kc1, kc1, vc0, z, vc1, z], axis=1).astype(BF16)


def _prep_gain(qk_g):
    qs = SCALE * LOG2E
    ones = jnp.ones((CHUNK,), F32)
    t4 = lambda g: jnp.tile(g, CHUNK // HEAD_DIM)
    return jnp.concatenate([
        t4(qk_g[0, 0]) * qs, t4(qk_g[0, 1]), ones,
        t4(qk_g[1, 0]) * qs, t4(qk_g[1, 1]), ones,
        t4(qk_g[2, 0]) * qs, t4(qk_g[2, 0]) * qs, t4(qk_g[2, 1]), ones,
    ])[None, :]


def _split_hi_lo(w):
    hi = w.astype(BF16)
    lo = (w - hi.astype(F32)).astype(BF16)
    return hi, lo


def _mixer_layer(x2, bsz, seq, tables, norm1_g, w_in, qk_g, rpb, mix_g, w_out, norm2_g, w_rg, b_rg, w_re, b_re):
    n, d = x2.shape
    cos_t, sa_t, sb_t, bd = tables
    proj, pb4, pb16 = _inproj(x2, norm1_g[None, :], _prep_w_in(w_in), _prep_gain(qk_g), bd, cos_t, sa_t, sb_t, seq)
    proj3 = proj.reshape(bsz, seq, PROJ_W)
    oa = _na_attention(proj3, _na_bias_table(rpb)).reshape(n, CHUNK)
    ob_parts = [_dilated_branch(proj3[:, None], CH_QB, 1, 4), _dilated_branch(pb4, 0, 1, 4),
                _dilated_branch(pb16, 0, 2, 2)]
    oc = _gqa_attention(proj3).reshape(n, 2 * CHUNK)

    w_r = jnp.zeros((d, LANES), F32)
    w_r = w_r.at[:, :N_GROUPS].set(w_rg).at[:, N_GROUPS:N_GROUPS + N_EXPERTS].set(w_re)
    b_r = jnp.zeros((1, LANES), F32)
    b_r = b_r.at[0, :N_GROUPS].set(b_rg).at[0, N_GROUPS:N_GROUPS + N_EXPERTS].set(b_re)
    wr_hi, wr_lo = _split_hi_lo(w_r)
    return _outproj(oa, ob_parts, oc, x2, w_out.astype(BF16), mix_g[None, :], norm2_g[None, :], wr_hi, wr_lo, b_r,
                    seq)


def _moe_layer(x1, h, route, counts, slots, w_gate, w_up, w_down, layer):
    n = x1.shape[0]
    n_blk = slots.shape[0] // MOE_BLOCK
    dest, meta = _rank(route, counts)
    dest_flat = dest[:, :2].T.reshape(2 * n)
    xs = _dispatch(dest_flat, h, slots)
    y = _experts(meta[:n_blk, 0], meta[META_ROWS - 1:, 0], xs, w_gate, w_up, w_down, layer)
    return _combine(dest_flat, x1, route, y), xs


def _tables(seq):
    hd = np.arange(CHUNK) // HEAD_DIM
    return (*_rope_tables(seq), jnp.asarray(hd[:, None] == hd[None, :], BF16))


@jax.jit
def kernel(x, norm1_g, w_in, qk_norm_g, na_rpb, mix_norm_g, w_out, norm2_g, w_router_group, b_router_group,
           w_router_expert, b_router_expert, w_gate, w_up, w_down):
    bsz, seq, d = x.shape
    tables = _tables(seq)
    x2 = x.reshape(bsz * seq, d)
    slots = jnp.zeros((2 * bsz * seq + N_EXPERTS * MOE_BLOCK, d // 2), jnp.uint32)
    for l in range(DEPTH):
        x1, h, route, counts = _mixer_layer(x2, bsz, seq, tables, norm1_g[l], w_in[l], qk_norm_g[l], na_rpb[l],
                                            mix_norm_g[l], w_out[l], norm2_g[l], w_router_group[l],
                                            b_router_group[l], w_router_expert[l], b_router_expert[l])
        x2, slots = _moe_layer(x1, h, route, counts, slots, w_gate, w_up, w_down, l)
    return x2.reshape(bsz, seq, d)
```

```python
import functools
import math

import numpy as np
import jax
import jax.numpy as jnp
from jax import lax
from jax.experimental import pallas as pl
from jax.experimental.pallas import tpu as pltpu

F32 = jnp.float32
BF16 = jnp.bfloat16

D_MODEL = 1024
DEPTH = 4
GRID_W = 64
HEAD_DIM = 64
RMS_EPS = 1e-6
NEG = -1e30
LOG2E = math.log2(math.e)
SCALE = HEAD_DIM ** -0.5

NA_ROWS = 8
NA_COLS = 16
DILATIONS = (1, 4, 16)
DIL_HALF = 64
ROPE_THETA = 10000.0

N_GROUPS = 4
EXPERTS_PER_GROUP = 8
N_EXPERTS = 32
D_EXPERT = 512
MOE_BLOCK = 256

LANES = 128
SUBLANES = 8
CHUNK = 256
CH_QA, CH_KA, CH_VA, CH_QB, CH_KB, CH_VB, CH_QC, CH_KC, CH_VC = 0, 1, 2, 3, 4, 5, 6, 8, 9
N_CHUNKS = 10
PROJ_W = N_CHUNKS * CHUNK
NORM_CHUNKS = (CH_QA, CH_KA, CH_QB, CH_KB, CH_QC, CH_QC + 1, CH_KC)
ROPE_CHUNKS = (CH_QC, CH_QC + 1, CH_KC)

VMEM_LIMIT = 56 * 1024 * 1024

NT_DIMS = (((1,), (1,)), ((), ()))


def _cparams(sem):
    return pltpu.CompilerParams(dimension_semantics=sem, vmem_limit_bytes=VMEM_LIMIT)


INPROJ_PARTS = 2


def _inproj_kernel(x_ref, g1_ref, w_ref, gain_ref, bd_ref, cos_ref, sa_ref, sb_ref, o_ref, o4_ref, o16_ref, stage_ref):
    tm = x_ref.shape[0]
    tp = tm // INPROJ_PARTS
    part_rows = [slice(i * tp, (i + 1) * tp) for i in range(INPROJ_PARTS)]
    chunk_cols = [slice(c * CHUNK, (c + 1) * CHUNK) for c in range(N_CHUNKS)]
    n_stage = 3 * CHUNK // LANES

    normed = []
    for rows in part_rows:
        x = x_ref[rows, :]
        ms = jnp.mean(x * x, axis=-1, keepdims=True)
        normed.append((x * lax.rsqrt(ms + RMS_EPS) * g1_ref[...]).astype(BF16))
    projs = [jnp.dot(h, w_ref[...], preferred_element_type=F32) for h in normed]
    for part, (rows, proj) in enumerate(zip(part_rows, projs)):
        sumsq = {c: jnp.dot(jnp.square(proj[:, chunk_cols[c]]).astype(BF16), bd_ref[...],
                            preferred_element_type=F32) for c in NORM_CHUNKS}
        for c in range(N_CHUNKS):
            cols = chunk_cols[c]
            p = proj[:, cols]
            if c in NORM_CHUNKS:
                p = p * lax.rsqrt(sumsq[c] * (1.0 / HEAD_DIM) + RMS_EPS) * gain_ref[:, cols]
            if c in ROPE_CHUNKS:
                p = (p * cos_ref[rows, :] + pltpu.roll(p, CHUNK - 16, 1) * sa_ref[rows, :]
                     + pltpu.roll(p, 16, 1) * sb_ref[rows, :])
            if c == CH_VC:
                lane = lax.broadcasted_iota(jnp.int32, p.shape, 1)
                p = jnp.where(lane % LANES >= HEAD_DIM, 1.0, p)
            o_ref[rows, cols] = p.astype(BF16)
            if c in (CH_QB, CH_KB, CH_VB):
                for hf in range(CHUNK // LANES):
                    lane0 = ((c - CH_QB) * (CHUNK // LANES) + hf) * LANES
                    slot = part * n_stage + lane0 // LANES
                    stage_ref[slot] = p[:, hf * LANES:(hf + 1) * LANES]
                    for dil, od_ref in ((4, o4_ref), (16, o16_ref)):
                        nr = tp // dil
                        for r in range(dil):
                            picked = stage_ref[slot, pl.ds(r, nr, stride=dil), :]
                            od_ref[0, r, part * nr:(part + 1) * nr, lane0:lane0 + LANES] = picked.astype(BF16)


def _inproj(x2, g1, w, gain, bd, cos_t, sa_t, sb_t, seq, tm=512):
    n = x2.shape[0]
    nsb = seq // tm
    bsz = n // seq
    dil_spec = lambda dil: pl.BlockSpec((1, dil, tm // dil, 3 * CHUNK), lambda i: (i // nsb, 0, i % nsb, 0))
    dil_shape = lambda dil: jax.ShapeDtypeStruct((bsz, dil, seq // dil, 3 * CHUNK), BF16)
    return pl.pallas_call(
        _inproj_kernel,
        grid=(n // tm,),
        in_specs=[
            pl.BlockSpec((tm, D_MODEL), lambda i: (i, 0)),
            pl.BlockSpec((1, D_MODEL), lambda i: (0, 0)),
            pl.BlockSpec((D_MODEL, PROJ_W), lambda i: (0, 0)),
            pl.BlockSpec((1, PROJ_W), lambda i: (0, 0)),
            pl.BlockSpec((CHUNK, CHUNK), lambda i: (0, 0)),
            pl.BlockSpec((tm, CHUNK), lambda i: (i % nsb, 0)),
            pl.BlockSpec((tm, CHUNK), lambda i: (i % nsb, 0)),
            pl.BlockSpec((tm, CHUNK), lambda i: (i % nsb, 0)),
        ],
        out_specs=[pl.BlockSpec((tm, PROJ_W), lambda i: (i, 0)), dil_spec(4), dil_spec(16)],
        out_shape=[jax.ShapeDtypeStruct((n, PROJ_W), BF16), dil_shape(4), dil_shape(16)],
        scratch_shapes=[pltpu.VMEM((INPROJ_PARTS * 3 * CHUNK // LANES, tm // INPROJ_PARTS, LANES), F32)],
        compiler_params=_cparams(("arbitrary",)),
        name="inproj",
    )(x2, g1, w, gain, bd, cos_t, sa_t, sb_t)


def _stack_head_pair(qp, lo):
    zero = jnp.zeros_like(qp)
    return jnp.concatenate([jnp.where(lo, qp, zero), jnp.where(lo, zero, qp)], axis=0)


def _pack_bf16_pair(a, b):
    ua = lax.bitcast_convert_type(a.astype(BF16).astype(F32), jnp.uint32)
    ub = lax.bitcast_convert_type(b.astype(BF16).astype(F32), jnp.uint32)
    return ua | (ub >> 16)


def _unpack_bf16_pair(u):
    hi = lax.bitcast_convert_type(u & jnp.uint32(0xFFFF0000), F32)
    lo = lax.bitcast_convert_type(u << 16, F32)
    return hi, lo


def _lo_mask(rows):
    return lax.broadcasted_iota(jnp.int32, (rows, LANES), 1) < HEAD_DIM


NA_ROW_GROUP = 4


def _na_kernel(q_ref, k_ref, v_ref, bias_ref, o_ref, *, rows_per_step, n_rows):
    i = pl.program_id(1)
    lo = _lo_mask(GRID_W)
    win = NA_ROWS * GRID_W

    def window(rl):
        r = i * rows_per_step + rl
        r0 = jnp.clip(r - NA_ROWS // 2, 0, n_rows - NA_ROWS)
        return pl.multiple_of(r0 * GRID_W, GRID_W), r - r0

    for g0 in range(0, rows_per_step, NA_ROW_GROUP):
        chains = [(rl, p) for rl in range(g0, g0 + NA_ROW_GROUP) for p in range(2)]
        scores = []
        for rl, p in chains:
            start, rr = window(rl)
            cols = slice(p * LANES, (p + 1) * LANES)
            qs = _stack_head_pair(q_ref[0, rl * GRID_W:(rl + 1) * GRID_W, cols], lo)
            s = lax.dot_general(qs, k_ref[0, pl.ds(start, win), cols], NT_DIMS, preferred_element_type=F32)
            scores.append(s + bias_ref[p, rr])
        stats = []
        for s in scores:
            e = jnp.exp2(s - jnp.max(s, axis=-1, keepdims=True))
            stats.append((e, jnp.sum(e, axis=-1, keepdims=True)))
        for (rl, p), (e, l) in zip(chains, stats):
            start, _ = window(rl)
            cols = slice(p * LANES, (p + 1) * LANES)
            pv = jnp.dot(e.astype(BF16), v_ref[0, pl.ds(start, win), cols], preferred_element_type=F32)
            o = pv / l
            o_ref[0, rl * GRID_W:(rl + 1) * GRID_W, cols] = jnp.where(lo, o[:GRID_W], o[GRID_W:]).astype(BF16)


def _na_attention(proj3, bias, rows_per_step=8):
    b, s, _ = proj3.shape
    n_rows = s // GRID_W
    tq = rows_per_step * GRID_W
    return pl.pallas_call(
        functools.partial(_na_kernel, rows_per_step=rows_per_step, n_rows=n_rows),
        grid=(b, n_rows // rows_per_step),
        in_specs=[
            pl.BlockSpec((1, tq, CHUNK), lambda bi, i: (bi, i, CH_QA)),
            pl.BlockSpec((1, s, CHUNK), lambda bi, i: (bi, 0, CH_KA)),
            pl.BlockSpec((1, s, CHUNK), lambda bi, i: (bi, 0, CH_VA)),
            pl.BlockSpec((2, NA_ROWS, 2 * GRID_W, NA_ROWS * GRID_W), lambda bi, i: (0, 0, 0, 0)),
        ],
        out_specs=pl.BlockSpec((1, tq, CHUNK), lambda bi, i: (bi, i, 0)),
        out_shape=jax.ShapeDtypeStruct((b, s, CHUNK), BF16),
        compiler_params=_cparams(("arbitrary", "arbitrary")),
        name="na_attention",
    )(proj3, proj3, proj3, bias)


def _na_bias_table(rpb):
    c = np.arange(GRID_W)
    cs = np.clip(c - NA_COLS // 2, 0, GRID_W - NA_COLS)
    kc = np.arange(GRID_W)
    inwin = (kc[None, :] >= cs[:, None]) & (kc[None, :] < cs[:, None] + NA_COLS)
    dc = kc[None, :] - c[:, None] + NA_COLS - 1
    rr = np.arange(NA_ROWS)
    dr = rr[None, :] - rr[:, None] + NA_ROWS - 1
    sel_r = jnp.asarray(dr[:, :, None] == np.arange(2 * NA_ROWS - 1), F32)
    sel_c = jnp.asarray((dc[:, :, None] == np.arange(2 * NA_COLS - 1)) & inwin[:, :, None], F32)
    tab = jnp.einsum("rka,hab,cjb->hrckj", sel_r, rpb, sel_c, precision=lax.Precision.HIGHEST)
    tab = jnp.where(inwin[None, None, :, None, :], tab * LOG2E, NEG)
    tab = tab.reshape(2, 2, NA_ROWS, GRID_W, NA_ROWS * GRID_W)
    return tab.transpose(0, 2, 1, 3, 4).reshape(2, NA_ROWS, 2 * GRID_W, NA_ROWS * GRID_W)


DIL_TQ = 128
DIL_WIN = DIL_TQ + 2 * DIL_HALF


def _dil_kernel(q_ref, k_ref, v_ref, bias_ref, o_ref, lw_ref, *, length, rg, ug, nb):
    u0 = pl.program_id(2) * ug
    lo = _lo_mask(DIL_TQ)
    chains = [(r, ub, p) for r in range(rg) for ub in range(ug) for p in range(2)]

    def window(ub):
        gb = u0 + ub
        ws = pl.multiple_of(jnp.clip(gb * DIL_TQ - DIL_HALF, 0, length - DIL_WIN), DIL_HALF)
        case = jnp.where(gb == 0, 0, jnp.where(gb == nb - 1, 2, 1))
        return ws, case

    scores = []
    for r, ub, p in chains:
        ws, case = window(ub)
        cols = slice(p * LANES, (p + 1) * LANES)
        qs = _stack_head_pair(q_ref[r, ub * DIL_TQ:(ub + 1) * DIL_TQ, cols], lo)
        s = lax.dot_general(qs, k_ref[r, pl.ds(ws, DIL_WIN), cols], NT_DIMS, preferred_element_type=F32)
        scores.append(s + bias_ref[case, p])
    stats = []
    for s in scores:
        m = jnp.max(s, axis=-1, keepdims=True)
        e = jnp.exp2(s - m)
        stats.append((m, e, jnp.sum(e, axis=-1, keepdims=True)))
    for (r, ub, p), (m, e, l) in zip(chains, stats):
        ws, _ = window(ub)
        cols = slice(p * LANES, (p + 1) * LANES)
        pv = jnp.dot(e.astype(BF16), v_ref[r, pl.ds(ws, DIL_WIN), cols], preferred_element_type=F32)
        o = pv / l
        lw = jnp.broadcast_to(m + jnp.log2(l), (2 * DIL_TQ, LANES))
        rows = slice(ub * DIL_TQ, (ub + 1) * DIL_TQ)
        ocols = slice(r * CHUNK + p * LANES, r * CHUNK + (p + 1) * LANES)
        o_ref[0, rows, ocols] = jnp.where(lo, o[:DIL_TQ], o[DIL_TQ:]).astype(BF16)
        lw_ref[0, rows, ocols] = jnp.where(lo, lw[:DIL_TQ], lw[DIL_TQ:])


def _dil_bias_table(dil):
    slopes = np.exp2(-8.0 * np.arange(1, 5) / 4.0)
    qi = np.arange(DIL_TQ)[:, None]
    kj = np.arange(DIL_WIN)[None, :]
    tab = np.zeros((3, 4, DIL_TQ, DIL_WIN), np.float32)
    for case, shift in enumerate((0, -DIL_HALF, -2 * DIL_HALF)):
        rel = kj + shift - qi
        for h in range(4):
            tab[case, h] = np.where(np.abs(rel) <= DIL_HALF, -slopes[h] * dil * np.abs(rel) * LOG2E, NEG)
    return tab.reshape(3, 2, 2 * DIL_TQ, DIL_WIN)


def _dilated_branch(src, q_chunk, rg, ug):
    b, dil, length, _ = src.shape
    nb = length // DIL_TQ
    o, lw = pl.pallas_call(
        functools.partial(_dil_kernel, length=length, rg=rg, ug=ug, nb=nb),
        grid=(b, dil // rg, nb // ug),
        in_specs=[
            pl.BlockSpec((None, rg, ug * DIL_TQ, CHUNK), lambda bi, r, u: (bi, r, u, q_chunk)),
            pl.BlockSpec((None, rg, length, CHUNK), lambda bi, r, u: (bi, r, 0, q_chunk + 1)),
            pl.BlockSpec((None, rg, length, CHUNK), lambda bi, r, u: (bi, r, 0, q_chunk + 2)),
            pl.BlockSpec((3, 2, 2 * DIL_TQ, DIL_WIN), lambda bi, r, u: (0, 0, 0, 0)),
        ],
        out_specs=[
            pl.BlockSpec((1, ug * DIL_TQ, rg * CHUNK), lambda bi, r, u: (bi, u, r)),
            pl.BlockSpec((1, ug * DIL_TQ, rg * CHUNK), lambda bi, r, u: (bi, u, r)),
        ],
        out_shape=[
            jax.ShapeDtypeStruct((b, length, dil * CHUNK), BF16),
            jax.ShapeDtypeStruct((b, length, dil * CHUNK), F32),
        ],
        compiler_params=_cparams(("arbitrary", "arbitrary", "arbitrary")),
        name=f"dilated_d{dil}",
    )(src, src, src, jnp.asarray(_dil_bias_table(dil)))
    return o, lw


TN_DIMS = (((0,), (0,)), ((), ()))


def _gqa_kernel(q_ref, k_ref, v_ref, o_ref, m_s, acc_s, s_s, *, tq, tk, seq):
    lo = _lo_mask(tq)
    n_kt = seq // tk
    qs = jnp.concatenate([_stack_head_pair(q_ref[0, :, j * LANES:(j + 1) * LANES], lo) for j in range(2)],
                         axis=0)
    m_s[...] = jnp.full(m_s.shape, NEG, F32)
    acc_s[...] = jnp.zeros(acc_s.shape, F32)

    def scores(kt, slot):
        k0 = pl.multiple_of(kt * tk, tk)
        s_s[slot] = lax.dot_general(k_ref[0, pl.ds(k0, tk), :], qs, NT_DIMS, preferred_element_type=F32)

    def softmax_values(kt, slot):
        k0 = pl.multiple_of(kt * tk, tk)
        s = s_s[slot]
        m_old = m_s[...]
        m_new = jnp.maximum(m_old, jnp.max(s, axis=0, keepdims=True))
        alpha = jnp.exp2(m_old - m_new)
        e = jnp.exp2((s - m_new[0:1, :]).astype(BF16))
        m_s[...] = m_new
        pv = lax.dot_general(v_ref[0, pl.ds(k0, tk), :], e, TN_DIMS, preferred_element_type=F32)
        acc_s[...] = alpha[0:1, :] * acc_s[...] + pv

    scores(0, 0)

    def body(j, carry):
        scores(2 * j + 1, 1)
        softmax_values(2 * j, 0)
        scores(2 * j + 2, 0)
        softmax_values(2 * j + 1, 1)
        return carry

    lax.fori_loop(0, n_kt // 2 - 1, body, 0, unroll=True)
    scores(n_kt - 1, 1)
    softmax_values(n_kt - 2, 0)
    softmax_values(n_kt - 1, 1)

    acc = acc_s[...]
    o_t = acc[:HEAD_DIM] / acc[HEAD_DIM:HEAD_DIM + 1]
    for j in range(2):
        pair_t = jnp.concatenate([o_t[:, (2 * j) * tq:(2 * j + 1) * tq], o_t[:, (2 * j + 1) * tq:(2 * j + 2) * tq]],
                                 axis=0)
        o_ref[0, :, j * LANES:(j + 1) * LANES] = pair_t.T.astype(BF16)


def _gqa_attention(proj3, tq=256, tk=256):
    b, s, _ = proj3.shape
    assert (s // tk) % 2 == 0 and s // tk >= 2
    return pl.pallas_call(
        functools.partial(_gqa_kernel, tq=tq, tk=tk, seq=s),
        grid=(b, 2, s // tq),
        in_specs=[
            pl.BlockSpec((1, tq, CHUNK), lambda bi, g, i: (bi, i, CH_QC + g)),
            pl.BlockSpec((1, s, LANES), lambda bi, g, i: (bi, 0, 2 * CH_KC + g)),
            pl.BlockSpec((1, s, LANES), lambda bi, g, i: (bi, 0, 2 * CH_VC + g)),
        ],
        out_specs=pl.BlockSpec((1, tq, CHUNK), lambda bi, g, i: (bi, i, g)),
        out_shape=jax.ShapeDtypeStruct((b, s, 2 * CHUNK), BF16),
        scratch_shapes=[
            pltpu.VMEM((8, 4 * tq), F32),
            pltpu.VMEM((LANES, 4 * tq), F32),
            pltpu.VMEM((2, tk, 4 * tq), F32),
        ],
        compiler_params=_cparams(("arbitrary", "arbitrary", "arbitrary")),
        name="gqa_attention",
    )(proj3, proj3, proj3)


def _rms(v, g):
    return v * lax.rsqrt(jnp.mean(v * v, axis=-1, keepdims=True) + RMS_EPS) * g


def _outproj_kernel(oa_ref, o1_ref, o4_ref, o16_ref, lw1_ref, lw4_ref, lw16_ref, oc_ref, x_ref,
                    w_ref, mg_ref, g2_ref, wrh_ref, wrl_ref, br_ref, x1_ref, h_ref, route_ref, cnt_ref, il_ref):
    tm = x_ref.shape[0]

    def interleaved(src_ref, dil, slot):
        for r in range(dil):
            for hf in range(CHUNK // LANES):
                piece = src_ref[0, :, r * CHUNK + hf * LANES:r * CHUNK + (hf + 1) * LANES].astype(F32)
                il_ref[slot * 2 + hf, pl.ds(r, tm // dil, stride=dil), :] = piece
        return jnp.concatenate([il_ref[slot * 2], il_ref[slot * 2 + 1]], axis=1)

    lw1 = lw1_ref[...]
    lw4, lw16 = interleaved(lw4_ref, 4, 0), interleaved(lw16_ref, 16, 1)
    o4, o16 = interleaved(o4_ref, 4, 2), interleaved(o16_ref, 16, 3)
    lwm = jnp.maximum(jnp.maximum(lw1, lw4), lw16)
    w1, w4, w16 = jnp.exp2(lw1 - lwm), jnp.exp2(lw4 - lwm), jnp.exp2(lw16 - lwm)
    ob = (w1 * o1_ref[...].astype(F32) + w4 * o4 + w16 * o16) / (w1 + w4 + w16)
    merged = jnp.concatenate([
        _rms(oa_ref[...].astype(F32), mg_ref[:, 0:CHUNK]).astype(BF16),
        _rms(ob, mg_ref[:, CHUNK:2 * CHUNK]).astype(BF16),
        _rms(oc_ref[...].astype(F32), mg_ref[:, 2 * CHUNK:]).astype(BF16),
    ], axis=-1)
    x1 = x_ref[...] + jnp.dot(merged, w_ref[...], preferred_element_type=F32)
    x1_ref[...] = x1
    h = _rms(x1, g2_ref[...])
    h_ref[...] = _pack_bf16_pair(h[:, :D_MODEL // 2], h[:, D_MODEL // 2:])

    hh = h.astype(BF16)
    hl = (h - hh.astype(F32)).astype(BF16)
    logits = (jnp.dot(hh, wrh_ref[...], preferred_element_type=F32)
              + jnp.dot(hl, wrh_ref[...], preferred_element_type=F32)
              + jnp.dot(hh, wrl_ref[...], preferred_element_type=F32)) + br_ref[...]
    lane = lax.broadcasted_iota(jnp.int32, logits.shape, 1)
    lane_f = lane.astype(F32)
    big = float(LANES)
    is_g = lane < N_GROUPS
    gl = jnp.where(is_g, logits, NEG)
    gmax = jnp.max(gl, axis=-1, keepdims=True)
    grp = jnp.min(jnp.where(gl == gmax, lane_f, big), axis=-1, keepdims=True)
    p_grp = 1.0 / jnp.sum(jnp.where(is_g, jnp.exp(logits - gmax), 0.0), axis=-1, keepdims=True)
    e_lo = N_GROUPS + EXPERTS_PER_GROUP * grp
    in_grp = (lane_f >= e_lo) & (lane_f < e_lo + EXPERTS_PER_GROUP)
    el = jnp.where(in_grp, logits, NEG)
    t1 = jnp.max(el, axis=-1, keepdims=True)
    i1 = jnp.min(jnp.where(el == t1, lane_f, big), axis=-1, keepdims=True)
    el2 = jnp.where(lane_f == i1, NEG, el)
    t2 = jnp.max(el2, axis=-1, keepdims=True)
    i2 = jnp.min(jnp.where(el2 == t2, lane_f, big), axis=-1, keepdims=True)
    e21 = jnp.exp(t2 - t1)
    gate1 = p_grp / (1.0 + e21)
    gate2 = p_grp * e21 / (1.0 + e21)
    route = jnp.where(lane == 0, i1 - N_GROUPS,
                      jnp.where(lane == 1, i2 - N_GROUPS,
                                jnp.where(lane == 2, gate1, jnp.where(lane == 3, gate2, 0.0))))
    route_ref[...] = route

    picked = jnp.where((lane_f == i1 - N_GROUPS) | (lane_f == i2 - N_GROUPS), 1.0, 0.0)

    @pl.when(pl.program_id(0) == 0)
    def _():
        cnt_ref[...] = jnp.zeros(cnt_ref.shape, F32)

    cnt_ref[...] = cnt_ref[...] + jnp.sum(picked, axis=0, keepdims=True)


def _outproj(oa, ob_parts, oc, x2, w_out, mix_g, g2, wr_hi, wr_lo, b_r, seq, tm=512):
    n = x2.shape[0]
    nsb = seq // tm
    (o1, lw1), (o4, lw4), (o16, lw16) = ob_parts
    o1, lw1 = o1.reshape(n, CHUNK), lw1.reshape(n, CHUNK)
    row = lambda w: pl.BlockSpec((tm, w), lambda i: (i, 0))
    full = lambda r, w: pl.BlockSpec((r, w), lambda i: (0, 0))
    dil = lambda d: pl.BlockSpec((1, tm // d, d * CHUNK), lambda i: (i // nsb, i % nsb, 0))
    return pl.pallas_call(
        _outproj_kernel,
        grid=(n // tm,),
        in_specs=[row(CHUNK), row(CHUNK), dil(4), dil(16), row(CHUNK), dil(4), dil(16),
                  row(2 * CHUNK), row(D_MODEL),
                  full(D_MODEL, D_MODEL), full(1, D_MODEL), full(1, D_MODEL),
                  full(D_MODEL, LANES), full(D_MODEL, LANES), full(1, LANES)],
        out_specs=[row(D_MODEL), row(D_MODEL // 2), row(LANES), full(8, LANES)],
        out_shape=[jax.ShapeDtypeStruct((n, D_MODEL), F32),
                   jax.ShapeDtypeStruct((n, D_MODEL // 2), jnp.uint32),
                   jax.ShapeDtypeStruct((n, LANES), F32),
                   jax.ShapeDtypeStruct((8, LANES), F32)],
        scratch_shapes=[pltpu.VMEM((8, tm, LANES), F32)],
        compiler_params=_cparams(("arbitrary",)),
        name="outproj_router",
    )(oa, o1, o4, o16, lw1, lw4, lw16, oc, x2, w_out, mix_g, g2, wr_hi, wr_lo, b_r)


META_ROWS = 384


def _rank_kernel(route_ref, cnt_ref, dest_ref, meta_ref, base_s, tri_s, *, tm):
    i = pl.program_id(0)
    lane = lax.broadcasted_iota(jnp.int32, (tm, LANES), 1)
    lane_f = lane.astype(F32)
    route = route_ref[...]
    oh0 = jnp.where(lane_f == route[:, 0:1], 1.0, 0.0)
    oh1 = jnp.where(lane_f == route[:, 1:2], 1.0, 0.0)
    oh = oh0 + oh1

    @pl.when(i == 0)
    def _():
        r = lax.broadcasted_iota(jnp.int32, (tm, tm), 0)
        c = lax.broadcasted_iota(jnp.int32, (tm, tm), 1)
        tri_s[...] = jnp.where(c < r, 1.0, 0.0).astype(BF16)
        lane8 = lax.broadcasted_iota(jnp.int32, cnt_ref.shape, 1)
        padded = jnp.floor((cnt_ref[...] + (MOE_BLOCK - 1)) * (1.0 / MOE_BLOCK)) * MOE_BLOCK
        ends = padded
        for sh in (1, 2, 4, 8, 16):
            ends = ends + jnp.where(lane8 >= sh, pltpu.roll(ends, sh, 1), 0.0)
        base_s[...] = ends - padded
        pos = lax.broadcasted_iota(jnp.int32, (META_ROWS, LANES), 0).astype(F32) * MOE_BLOCK
        lane_m = lax.broadcasted_iota(jnp.int32, (META_ROWS, LANES), 1)
        ends_row = ends[0:1, :]
        hit = jnp.where((lane_m < N_EXPERTS) & (ends_row <= pos), 1.0, 0.0)
        blk_e = jnp.minimum(jnp.sum(hit, axis=-1, keepdims=True), N_EXPERTS - 1.0)
        total = jnp.sum(jnp.where(lane_m[0:1] == N_EXPERTS - 1, ends_row, 0.0), axis=-1, keepdims=True)
        row_m = lax.broadcasted_iota(jnp.int32, (META_ROWS, 1), 0)
        meta = jnp.where(row_m == META_ROWS - 1, total * (1.0 / MOE_BLOCK), blk_e)
        meta_ref[...] = meta.astype(jnp.int32)

    prior = jnp.dot(tri_s[...], oh.astype(BF16), preferred_element_type=F32)
    val = base_s[0:1, :] + prior
    d0 = jnp.sum(oh0 * val, axis=-1, keepdims=True)
    d1 = jnp.sum(oh1 * val, axis=-1, keepdims=True)
    dest_ref[...] = jnp.where(lane == 0, d0, jnp.where(lane == 1, d1, 0.0)).astype(jnp.int32)
    base_s[...] = base_s[...] + jnp.sum(oh, axis=0, keepdims=True)


def _rank(route, counts, tm=512):
    n = route.shape[0]
    return pl.pallas_call(
        functools.partial(_rank_kernel, tm=tm),
        grid=(n // tm,),
        in_specs=[pl.BlockSpec((tm, LANES), lambda i: (i, 0)), pl.BlockSpec((8, LANES), lambda i: (0, 0))],
        out_specs=[pl.BlockSpec((tm, LANES), lambda i: (i, 0)),
                   pl.BlockSpec((META_ROWS, 1), lambda i: (0, 0))],
        out_shape=[jax.ShapeDtypeStruct((n, LANES), jnp.int32),
                   jax.ShapeDtypeStruct((META_ROWS, 1), jnp.int32)],
        scratch_shapes=[pltpu.VMEM((8, LANES), F32), pltpu.VMEM((tm, tm), BF16)],
        compiler_params=_cparams(("arbitrary",)),
        name="moe_rank",
    )(route, counts)


def _dispatch_kernel(dest_ref, h_ref, xs_in_hbm, xs_hbm, sem, *, n, tc):
    del xs_in_hbm
    t0 = pl.program_id(0) * tc

    def group(g, carry):
        base = pl.multiple_of(g * SUBLANES, SUBLANES)
        for u in range(SUBLANES):
            src = h_ref.at[pl.ds(base + u, 1)]
            pltpu.make_async_copy(src, xs_hbm.at[pl.ds(dest_ref[t0 + base + u], 1)], sem.at[0]).start(priority=0)
            pltpu.make_async_copy(src, xs_hbm.at[pl.ds(dest_ref[n + t0 + base + u], 1)], sem.at[0]).start(priority=1)
        return carry

    lax.fori_loop(0, tc // SUBLANES, group, 0)
    for _ in range(2):
        pltpu.make_async_copy(h_ref, xs_hbm.at[pl.ds(0, tc)], sem.at[0]).wait()


def _dispatch(dest_flat, h, slots, tc=512):
    n, w = h.shape
    cap = slots.shape[0]
    return pl.pallas_call(
        functools.partial(_dispatch_kernel, n=n, tc=tc),
        grid_spec=pltpu.PrefetchScalarGridSpec(
            num_scalar_prefetch=1,
            grid=(n // tc,),
            in_specs=[pl.BlockSpec((tc, w), lambda i, d: (i, 0)), pl.BlockSpec(memory_space=pl.ANY)],
            out_specs=pl.BlockSpec(memory_space=pl.ANY),
            scratch_shapes=[pltpu.SemaphoreType.DMA((1,))],
        ),
        out_shape=jax.ShapeDtypeStruct((cap, w), h.dtype),
        input_output_aliases={2: 0},
        compiler_params=_cparams(("arbitrary",)),
        name="moe_dispatch",
    )(dest_flat, h, slots)


def _expert_kernel(blk_e_ref, nact_ref, xs_ref, wg_ref, wu_ref, wd_ref, y_ref, wg_s, wu_s, wd_s):
    i = pl.program_id(0)
    active = i < nact_ref[0]
    changed = (i == 0) | (blk_e_ref[i] != blk_e_ref[jnp.maximum(i - 1, 0)])

    @pl.when(active & changed)
    def _():
        wg_s[...] = wg_ref[...].astype(BF16)
        wu_s[...] = wu_ref[...].astype(BF16)
        wd_s[...] = wd_ref[...].astype(BF16)

    @pl.when(active)
    def _():
        hi, lo = _unpack_bf16_pair(xs_ref[...])
        xb = jnp.concatenate([hi.astype(BF16), lo.astype(BF16)], axis=1)
        g = jnp.dot(xb, wg_s[...], preferred_element_type=F32)
        u = jnp.dot(xb, wu_s[...], preferred_element_type=F32)
        a = (g * jax.nn.sigmoid(g) * u).astype(BF16)
        y = jnp.dot(a, wd_s[...], preferred_element_type=F32)
        y_ref[...] = _pack_bf16_pair(y[:, :D_MODEL // 2], y[:, D_MODEL // 2:])

    @pl.when(jnp.logical_not(active))
    def _():
        y_ref[...] = jnp.zeros(y_ref.shape, y_ref.dtype)


def _experts(blk_e, nact, xs, w_gate, w_up, w_down, layer):
    cap = xs.shape[0]
    n_blk = cap // MOE_BLOCK

    def blk(i, be, na):
        return jnp.minimum(i, na[0] - 1)

    return pl.pallas_call(
        _expert_kernel,
        grid_spec=pltpu.PrefetchScalarGridSpec(
            num_scalar_prefetch=2,
            grid=(n_blk,),
            in_specs=[
                pl.BlockSpec((MOE_BLOCK, D_MODEL // 2), lambda i, be, na: (blk(i, be, na), 0)),
                pl.BlockSpec((None, None, D_MODEL, D_EXPERT), lambda i, be, na: (layer, be[blk(i, be, na)], 0, 0)),
                pl.BlockSpec((None, None, D_MODEL, D_EXPERT), lambda i, be, na: (layer, be[blk(i, be, na)], 0, 0)),
                pl.BlockSpec((None, None, D_EXPERT, D_MODEL), lambda i, be, na: (layer, be[blk(i, be, na)], 0, 0)),
            ],
            out_specs=pl.BlockSpec((MOE_BLOCK, D_MODEL // 2), lambda i, be, na: (i, 0)),
            scratch_shapes=[pltpu.VMEM((D_MODEL, D_EXPERT), BF16), pltpu.VMEM((D_MODEL, D_EXPERT), BF16),
                            pltpu.VMEM((D_EXPERT, D_MODEL), BF16)],
        ),
        out_shape=jax.ShapeDtypeStruct((cap, D_MODEL // 2), jnp.uint32),
        compiler_params=_cparams(("arbitrary",)),
        name="moe_experts",
    )(blk_e, nact, xs, w_gate, w_up, w_down)


def _combine_kernel(dest_ref, x1_ref, route_ref, y_hbm, o_ref, yb, sem, *, n, tc):
    i = pl.program_id(0)
    last = pl.num_programs(0) - 1
    slot = i % 2

    def gather(tile, sl):
        t0 = tile * tc

        def body(j, carry):
            for k in range(2):
                pltpu.make_async_copy(y_hbm.at[pl.ds(dest_ref[k * n + t0 + j], 1)], yb.at[sl, k, pl.ds(j, 1)],
                                      sem.at[sl]).start(priority=k)
            return carry

        lax.fori_loop(0, tc, body, 0, unroll=8)

    @pl.when(i == 0)
    def _():
        gather(0, 0)

    for k in range(2):
        pltpu.make_async_copy(y_hbm.at[pl.ds(0, tc)], yb.at[slot, k], sem.at[slot]).wait()

    @pl.when(i < last)
    def _():
        gather(i + 1, 1 - slot)

    route = route_ref[...]
    hi0, lo0 = _unpack_bf16_pair(yb[slot, 0])
    hi1, lo1 = _unpack_bf16_pair(yb[slot, 1])
    half = D_MODEL // 2
    o_ref[:, :half] = x1_ref[:, :half] + route[:, 2:3] * hi0 + route[:, 3:4] * hi1
    o_ref[:, half:] = x1_ref[:, half:] + route[:, 2:3] * lo0 + route[:, 3:4] * lo1


def _combine(dest_flat, x1, route, y, tc=512):
    n = x1.shape[0]
    return pl.pallas_call(
        functools.partial(_combine_kernel, n=n, tc=tc),
        grid_spec=pltpu.PrefetchScalarGridSpec(
            num_scalar_prefetch=1,
            grid=(n // tc,),
            in_specs=[pl.BlockSpec((tc, D_MODEL), lambda i, d: (i, 0)),
                      pl.BlockSpec((tc, LANES), lambda i, d: (i, 0)),
                      pl.BlockSpec(memory_space=pl.ANY)],
            out_specs=pl.BlockSpec((tc, D_MODEL), lambda i, d: (i, 0)),
            scratch_shapes=[pltpu.VMEM((2, 2, tc, D_MODEL // 2), jnp.uint32), pltpu.SemaphoreType.DMA((2,))],
        ),
        out_shape=jax.ShapeDtypeStruct((n, D_MODEL), F32),
        compiler_params=_cparams(("arbitrary",)),
        name="moe_combine",
    )(dest_flat, x1, route, y)


def _rope_tables(seq):
    n_freq = HEAD_DIM // 4
    t = np.arange(seq)
    inv_freq = jnp.asarray(ROPE_THETA, F32) ** (-jnp.arange(n_freq, dtype=F32) / n_freq)
    pos = jnp.stack([jnp.asarray(t // GRID_W, F32), jnp.asarray(t % GRID_W, F32)], axis=1)
    ang = pos[:, :, None] * inv_freq
    cos = jnp.broadcast_to(jnp.cos(ang)[:, :, None, :], (seq, 2, 2, n_freq)).reshape(seq, HEAD_DIM)
    sin = jnp.broadcast_to(jnp.sin(ang)[:, :, None, :], (seq, 2, 2, n_freq)).reshape(seq, HEAD_DIM)
    first_half = (np.arange(HEAD_DIM) % (2 * n_freq)) < n_freq
    sa = jnp.where(first_half, -sin, 0.0)
    sb = jnp.where(first_half, 0.0, sin)
    rep = CHUNK // HEAD_DIM
    return jnp.tile(cos, (1, rep)), jnp.tile(sa, (1, rep)), jnp.tile(sb, (1, rep))


def _prep_w_in(w):
    base = w[:, :2048]
    kc0, kc1 = w[:, 2048:2112], w[:, 2112:2176]
    vc0, vc1 = w[:, 2176:2240], w[:, 2240:2304]
    z = jnp.zeros_like(vc0)
    return jnp.concatenate([base, kc0, kc0, kc1, kc1, vc0, z, vc1, z], axis=1).astype(BF16)


def _prep_gain(qk_g):
    qs = SCALE * LOG2E
    ones = jnp.ones((CHUNK,), F32)
    t4 = lambda g: jnp.tile(g, CHUNK // HEAD_DIM)
    return jnp.concatenate([
        t4(qk_g[0, 0]) * qs, t4(qk_g[0, 1]), ones,
        t4(qk_g[1, 0]) * qs, t4(qk_g[1, 1]), ones,
        t4(qk_g[2, 0]) * qs, t4(qk_g[2, 0]) * qs, t4(qk_g[2, 1]), ones,
    ])[None, :]


def _split_hi_lo(w):
    hi = w.astype(BF16)
    lo = (w - hi.astype(F32)).astype(BF16)
    return hi, lo


def _mixer_layer(x2, bsz, seq, tables, norm1_g, w_in, qk_g, rpb, mix_g, w_out, norm2_g, w_rg, b_rg, w_re, b_re):
    n, d = x2.shape
    cos_t, sa_t, sb_t, bd = tables
    proj, pb4, pb16 = _inproj(x2, norm1_g[None, :], _prep_w_in(w_in), _prep_gain(qk_g), bd, cos_t, sa_t, sb_t, seq)
    proj3 = proj.reshape(bsz, seq, PROJ_W)
    oa = _na_attention(proj3, _na_bias_table(rpb)).reshape(n, CHUNK)
    ob_parts = [_dilated_branch(proj3[:, None], CH_QB, 1, 4), _dilated_branch(pb4, 0, 1, 4),
                _dilated_branch(pb16, 0, 2, 2)]
    oc = _gqa_attention(proj3).reshape(n, 2 * CHUNK)

    w_r = jnp.zeros((d, LANES), F32)
    w_r = w_r.at[:, :N_GROUPS].set(w_rg).at[:, N_GROUPS:N_GROUPS + N_EXPERTS].set(w_re)
    b_r = jnp.zeros((1, LANES), F32)
    b_r = b_r.at[0, :N_GROUPS].set(b_rg).at[0, N_GROUPS:N_GROUPS + N_EXPERTS].set(b_re)
    wr_hi, wr_lo = _split_hi_lo(w_r)
    return _outproj(oa, ob_parts, oc, x2, w_out.astype(BF16), mix_g[None, :], norm2_g[None, :], wr_hi, wr_lo, b_r,
                    seq)


def _moe_layer(x1, h, route, counts, slots, w_gate, w_up, w_down, layer):
    n = x1.shape[0]
    n_blk = slots.shape[0] // MOE_BLOCK
    dest, meta = _rank(route, counts)
    dest_flat = dest[:, :2].T.reshape(2 * n)
    xs = _dispatch(dest_flat, h, slots)
    y = _experts(meta[:n_blk, 0], meta[META_ROWS - 1:, 0], xs, w_gate, w_up, w_down, layer)
    return _combine(dest_flat, x1, route, y), xs


def _tables(seq):
    hd = np.arange(CHUNK) // HEAD_DIM
    return (*_rope_tables(seq), jnp.asarray(hd[:, None] == hd[None, :], BF16))


@jax.jit
def kernel(x, norm1_g, w_in, qk_norm_g, na_rpb, mix_norm_g, w_out, norm2_g, w_router_group, b_router_group,
           w_router_expert, b_router_expert, w_gate, w_up, w_down):
    bsz, seq, d = x.shape
    tables = _tables(seq)
    x2 = x.reshape(bsz * seq, d)
    slots = jnp.zeros((2 * bsz * seq + N_EXPERTS * MOE_BLOCK, d // 2), jnp.uint32)
    for l in range(DEPTH):
        x1, h, route, counts = _mixer_layer(x2, bsz, seq, tables, norm1_g[l], w_in[l], qk_norm_g[l], na_rpb[l],
                                            mix_norm_g[l], w_out[l], norm2_g[l], w_router_group[l],
                                            b_router_group[l], w_router_expert[l], b_router_expert[l])
        x2, slots = _moe_layer(x1, h, route, counts, slots, w_gate, w_up, w_down, l)
    return x2.reshape(bsz, seq, d)
```

```python
import functools
import math

import numpy as np
import jax
import jax.numpy as jnp
from jax import lax
from jax.experimental import pallas as pl
from jax.experimental.pallas import tpu as pltpu

F32 = jnp.float32
BF16 = jnp.bfloat16

D_MODEL = 1024
DEPTH = 4
GRID_W = 64
HEAD_DIM = 64
RMS_EPS = 1e-6
NEG = -1e30
LOG2E = math.log2(math.e)
SCALE = HEAD_DIM ** -0.5

NA_ROWS = 8
NA_COLS = 16
DILATIONS = (1, 4, 16)
DIL_HALF = 64
ROPE_THETA = 10000.0

N_GROUPS = 4
EXPERTS_PER_GROUP = 8
N_EXPERTS = 32
D_EXPERT = 512
MOE_BLOCK = 512

LANES = 128
SUBLANES = 8
CHUNK = 256
CH_QA, CH_KA, CH_VA, CH_QB, CH_KB, CH_VB, CH_QC, CH_KC, CH_VC = 0, 1, 2, 3, 4, 5, 6, 8, 9
N_CHUNKS = 10
PROJ_W = N_CHUNKS * CHUNK
NORM_CHUNKS = (CH_QA, CH_KA, CH_QB, CH_KB, CH_QC, CH_QC + 1, CH_KC)
ROPE_CHUNKS = (CH_QC, CH_QC + 1, CH_KC)

VMEM_LIMIT = 56 * 1024 * 1024

NT_DIMS = (((1,), (1,)), ((), ()))


def _cparams(sem):
    return pltpu.CompilerParams(dimension_semantics=sem, vmem_limit_bytes=VMEM_LIMIT)


INPROJ_PARTS = 2


def _inproj_kernel(x_ref, g1_ref, w_ref, gain_ref, bd_ref, cos_ref, sa_ref, sb_ref, o_ref, o4_ref, o16_ref, stage_ref):
    tm = x_ref.shape[0]
    tp = tm // INPROJ_PARTS
    part_rows = [slice(i * tp, (i + 1) * tp) for i in range(INPROJ_PARTS)]
    chunk_cols = [slice(c * CHUNK, (c + 1) * CHUNK) for c in range(N_CHUNKS)]
    n_stage = 3 * CHUNK // LANES

    normed = []
    for rows in part_rows:
        x = x_ref[rows, :]
        ms = jnp.mean(x * x, axis=-1, keepdims=True)
        normed.append((x * lax.rsqrt(ms + RMS_EPS) * g1_ref[...]).astype(BF16))
    projs = [jnp.dot(h, w_ref[...], preferred_element_type=F32) for h in normed]
    for part, (rows, proj) in enumerate(zip(part_rows, projs)):
        meansq = {c: jnp.dot(jnp.square(proj[:, chunk_cols[c]]).astype(BF16), bd_ref[...],
                             preferred_element_type=F32) for c in NORM_CHUNKS}
        for c in range(N_CHUNKS):
            cols = chunk_cols[c]
            p = proj[:, cols]
            if c in NORM_CHUNKS:
                p = p * lax.rsqrt(meansq[c] + RMS_EPS) * gain_ref[:, cols]
            if c in ROPE_CHUNKS:
                p = (p * cos_ref[rows, :] + pltpu.roll(p, CHUNK - 16, 1) * sa_ref[rows, :]
                     + pltpu.roll(p, 16, 1) * sb_ref[rows, :])
            if c == CH_VC:
                lane = lax.broadcasted_iota(jnp.int32, p.shape, 1)
                p = jnp.where(lane % LANES >= HEAD_DIM, 1.0, p)
            o_ref[rows, cols] = p.astype(BF16)
            if c in (CH_QB, CH_KB, CH_VB):
                for hf in range(CHUNK // LANES):
                    lane0 = ((c - CH_QB) * (CHUNK // LANES) + hf) * LANES
                    slot = part * n_stage + lane0 // LANES
                    stage_ref[slot] = p[:, hf * LANES:(hf + 1) * LANES]
                    for dil, od_ref in ((4, o4_ref), (16, o16_ref)):
                        nr = tp // dil
                        for r in range(dil):
                            picked = stage_ref[slot, pl.ds(r, nr, stride=dil), :]
                            od_ref[0, r, part * nr:(part + 1) * nr, lane0:lane0 + LANES] = picked.astype(BF16)


def _inproj(x2, g1, w, gain, bd, cos_t, sa_t, sb_t, seq, tm=512):
    n = x2.shape[0]
    nsb = seq // tm
    bsz = n // seq
    dil_spec = lambda dil: pl.BlockSpec((1, dil, tm // dil, 3 * CHUNK), lambda i: (i // nsb, 0, i % nsb, 0))
    dil_shape = lambda dil: jax.ShapeDtypeStruct((bsz, dil, seq // dil, 3 * CHUNK), BF16)
    return pl.pallas_call(
        _inproj_kernel,
        grid=(n // tm,),
        in_specs=[
            pl.BlockSpec((tm, D_MODEL), lambda i: (i, 0)),
            pl.BlockSpec((1, D_MODEL), lambda i: (0, 0)),
            pl.BlockSpec((D_MODEL, PROJ_W), lambda i: (0, 0)),
            pl.BlockSpec((1, PROJ_W), lambda i: (0, 0)),
            pl.BlockSpec((CHUNK, CHUNK), lambda i: (0, 0)),
            pl.BlockSpec((tm, CHUNK), lambda i: (i % nsb, 0)),
            pl.BlockSpec((tm, CHUNK), lambda i: (i % nsb, 0)),
            pl.BlockSpec((tm, CHUNK), lambda i: (i % nsb, 0)),
        ],
        out_specs=[pl.BlockSpec((tm, PROJ_W), lambda i: (i, 0)), dil_spec(4), dil_spec(16)],
        out_shape=[jax.ShapeDtypeStruct((n, PROJ_W), BF16), dil_shape(4), dil_shape(16)],
        scratch_shapes=[pltpu.VMEM((INPROJ_PARTS * 3 * CHUNK // LANES, tm // INPROJ_PARTS, LANES), F32)],
        compiler_params=_cparams(("arbitrary",)),
        name="inproj",
    )(x2, g1, w, gain, bd, cos_t, sa_t, sb_t)


def _stack_head_pair(qp, lo):
    zero = jnp.zeros_like(qp)
    return jnp.concatenate([jnp.where(lo, qp, zero), jnp.where(lo, zero, qp)], axis=0)


def _pack_bf16_pair(a, b):
    ua = lax.bitcast_convert_type(a.astype(BF16).astype(F32), jnp.uint32)
    ub = lax.bitcast_convert_type(b.astype(BF16).astype(F32), jnp.uint32)
    return ua | (ub >> 16)


def _unpack_bf16_pair(u):
    hi = lax.bitcast_convert_type(u & jnp.uint32(0xFFFF0000), F32)
    lo = lax.bitcast_convert_type(u << 16, F32)
    return hi, lo


def _lo_mask(rows):
    return lax.broadcasted_iota(jnp.int32, (rows, LANES), 1) < HEAD_DIM


NA_ROW_GROUP = 4


def _na_kernel(q_ref, k_ref, v_ref, bias_ref, o_ref, *, rows_per_step, n_rows):
    i = pl.program_id(1)
    lo = _lo_mask(GRID_W)
    win = NA_ROWS * GRID_W

    def window(rl):
        r = i * rows_per_step + rl
        r0 = jnp.clip(r - NA_ROWS // 2, 0, n_rows - NA_ROWS)
        return pl.multiple_of(r0 * GRID_W, GRID_W), r - r0

    for g0 in range(0, rows_per_step, NA_ROW_GROUP):
        chains = [(rl, p) for rl in range(g0, g0 + NA_ROW_GROUP) for p in range(2)]
        scores = []
        for rl, p in chains:
            start, rr = window(rl)
            cols = slice(p * LANES, (p + 1) * LANES)
            qs = _stack_head_pair(q_ref[0, rl * GRID_W:(rl + 1) * GRID_W, cols], lo)
            s = lax.dot_general(qs, k_ref[0, pl.ds(start, win), cols], NT_DIMS, preferred_element_type=F32)
            scores.append(s + bias_ref[p, rr])
        stats = []
        for s in scores:
            e = jnp.exp2(s - jnp.max(s, axis=-1, keepdims=True))
            stats.append((e, jnp.sum(e, axis=-1, keepdims=True)))
        for (rl, p), (e, l) in zip(chains, stats):
            start, _ = window(rl)
            cols = slice(p * LANES, (p + 1) * LANES)
            pv = jnp.dot(e.astype(BF16), v_ref[0, pl.ds(start, win), cols], preferred_element_type=F32)
            o = pv / l
            o_ref[0, rl * GRID_W:(rl + 1) * GRID_W, cols] = jnp.where(lo, o[:GRID_W], o[GRID_W:]).astype(BF16)


def _na_attention(proj3, bias, rows_per_step=8):
    b, s, _ = proj3.shape
    n_rows = s // GRID_W
    tq = rows_per_step * GRID_W
    return pl.pallas_call(
        functools.partial(_na_kernel, rows_per_step=rows_per_step, n_rows=n_rows),
        grid=(b, n_rows // rows_per_step),
        in_specs=[
            pl.BlockSpec((1, tq, CHUNK), lambda bi, i: (bi, i, CH_QA)),
            pl.BlockSpec((1, s, CHUNK), lambda bi, i: (bi, 0, CH_KA)),
            pl.BlockSpec((1, s, CHUNK), lambda bi, i: (bi, 0, CH_VA)),
            pl.BlockSpec((2, NA_ROWS, 2 * GRID_W, NA_ROWS * GRID_W), lambda bi, i: (0, 0, 0, 0)),
        ],
        out_specs=pl.BlockSpec((1, tq, CHUNK), lambda bi, i: (bi, i, 0)),
        out_shape=jax.ShapeDtypeStruct((b, s, CHUNK), BF16),
        compiler_params=_cparams(("arbitrary", "arbitrary")),
        name="na_attention",
    )(proj3, proj3, proj3, bias)


def _na_bias_table(rpb):
    c = np.arange(GRID_W)
    cs = np.clip(c - NA_COLS // 2, 0, GRID_W - NA_COLS)
    kc = np.arange(GRID_W)
    inwin = (kc[None, :] >= cs[:, None]) & (kc[None, :] < cs[:, None] + NA_COLS)
    dc = kc[None, :] - c[:, None] + NA_COLS - 1
    rr = np.arange(NA_ROWS)
    dr = rr[None, :] - rr[:, None] + NA_ROWS - 1
    sel_r = jnp.asarray(dr[:, :, None] == np.arange(2 * NA_ROWS - 1), F32)
    sel_c = jnp.asarray((dc[:, :, None] == np.arange(2 * NA_COLS - 1)) & inwin[:, :, None], F32)
    tab = jnp.einsum("rka,hab,cjb->hrckj", sel_r, rpb, sel_c, precision=lax.Precision.HIGHEST)
    tab = jnp.where(inwin[None, None, :, None, :], tab * LOG2E, NEG)
    tab = tab.reshape(2, 2, NA_ROWS, GRID_W, NA_ROWS * GRID_W)
    return tab.transpose(0, 2, 1, 3, 4).reshape(2, NA_ROWS, 2 * GRID_W, NA_ROWS * GRID_W)


DIL_TQ = 128
DIL_WIN = DIL_TQ + 2 * DIL_HALF


def _dil_kernel(q_ref, k_ref, v_ref, bias_ref, o_ref, lw_ref, *, length, rg, ug, nb):
    u0 = pl.program_id(2) * ug
    lo = _lo_mask(DIL_TQ)
    chains = [(r, ub, p) for r in range(rg) for ub in range(ug) for p in range(2)]

    def window(ub):
        gb = u0 + ub
        ws = pl.multiple_of(jnp.clip(gb * DIL_TQ - DIL_HALF, 0, length - DIL_WIN), DIL_HALF)
        case = jnp.where(gb == 0, 0, jnp.where(gb == nb - 1, 2, 1))
        return ws, case

    scores = []
    for r, ub, p in chains:
        ws, case = window(ub)
        cols = slice(p * LANES, (p + 1) * LANES)
        qs = _stack_head_pair(q_ref[r, ub * DIL_TQ:(ub + 1) * DIL_TQ, cols], lo)
        s = lax.dot_general(qs, k_ref[r, pl.ds(ws, DIL_WIN), cols], NT_DIMS, preferred_element_type=F32)
        scores.append(s + bias_ref[case, p])
    stats = []
    for s in scores:
        m = jnp.max(s, axis=-1, keepdims=True)
        e = jnp.exp2(s - m)
        stats.append((m, e, jnp.sum(e, axis=-1, keepdims=True)))
    for (r, ub, p), (m, e, l) in zip(chains, stats):
        ws, _ = window(ub)
        cols = slice(p * LANES, (p + 1) * LANES)
        pv = jnp.dot(e.astype(BF16), v_ref[r, pl.ds(ws, DIL_WIN), cols], preferred_element_type=F32)
        o = pv / l
        lw = jnp.broadcast_to(m + jnp.log2(l), (2 * DIL_TQ, LANES))
        rows = slice(ub * DIL_TQ, (ub + 1) * DIL_TQ)
        ocols = slice(r * CHUNK + p * LANES, r * CHUNK + (p + 1) * LANES)
        o_ref[0, rows, ocols] = jnp.where(lo, o[:DIL_TQ], o[DIL_TQ:]).astype(BF16)
        lw_ref[0, rows, ocols] = jnp.where(lo, lw[:DIL_TQ], lw[DIL_TQ:])


def _dil_bias_table(dil):
    slopes = np.exp2(-8.0 * np.arange(1, 5) / 4.0)
    qi = np.arange(DIL_TQ)[:, None]
    kj = np.arange(DIL_WIN)[None, :]
    tab = np.zeros((3, 4, DIL_TQ, DIL_WIN), np.float32)
    for case, shift in enumerate((0, -DIL_HALF, -2 * DIL_HALF)):
        rel = kj + shift - qi
        for h in range(4):
            tab[case, h] = np.where(np.abs(rel) <= DIL_HALF, -slopes[h] * dil * np.abs(rel) * LOG2E, NEG)
    return tab.reshape(3, 2, 2 * DIL_TQ, DIL_WIN)


def _dilated_branch(src, q_chunk, rg, ug):
    b, dil, length, _ = src.shape
    nb = length // DIL_TQ
    o, lw = pl.pallas_call(
        functools.partial(_dil_kernel, length=length, rg=rg, ug=ug, nb=nb),
        grid=(b, dil // rg, nb // ug),
        in_specs=[
            pl.BlockSpec((None, rg, ug * DIL_TQ, CHUNK), lambda bi, r, u: (bi, r, u, q_chunk)),
            pl.BlockSpec((None, rg, length, CHUNK), lambda bi, r, u: (bi, r, 0, q_chunk + 1)),
            pl.BlockSpec((None, rg, length, CHUNK), lambda bi, r, u: (bi, r, 0, q_chunk + 2)),
            pl.BlockSpec((3, 2, 2 * DIL_TQ, DIL_WIN), lambda bi, r, u: (0, 0, 0, 0)),
        ],
        out_specs=[
            pl.BlockSpec((1, ug * DIL_TQ, rg * CHUNK), lambda bi, r, u: (bi, u, r)),
            pl.BlockSpec((1, ug * DIL_TQ, rg * CHUNK), lambda bi, r, u: (bi, u, r)),
        ],
        out_shape=[
            jax.ShapeDtypeStruct((b, length, dil * CHUNK), BF16),
            jax.ShapeDtypeStruct((b, length, dil * CHUNK), F32),
        ],
        compiler_params=_cparams(("arbitrary", "arbitrary", "arbitrary")),
        name=f"dilated_d{dil}",
    )(src, src, src, jnp.asarray(_dil_bias_table(dil)))
    return o, lw


TN_DIMS = (((0,), (0,)), ((), ()))


def _gqa_kernel(q_ref, k_ref, v_ref, o_ref, m_s, acc_s, *, tq, tk, seq):
    lo = _lo_mask(tq)
    n_kt = seq // tk
    qs = jnp.concatenate([_stack_head_pair(q_ref[0, :, j * LANES:(j + 1) * LANES], lo) for j in range(2)],
                         axis=0)
    m_s[...] = jnp.full(m_s.shape, NEG, F32)
    acc_s[...] = jnp.zeros(acc_s.shape, F32)

    def scores(kt):
        return lax.dot_general(k_ref[0, kt * tk:(kt + 1) * tk, :], qs, NT_DIMS, preferred_element_type=F32)

    def softmax_values(kt, s):
        k0 = kt * tk
        m_old = m_s[...]
        m_new = jnp.maximum(m_old, jnp.max(s, axis=0, keepdims=True))
        alpha = jnp.exp2(m_old - m_new)
        e = jnp.exp2((s - m_new[0:1, :]).astype(BF16))
        m_s[...] = m_new
        pv = lax.dot_general(v_ref[0, k0:k0 + tk, :], e, TN_DIMS, preferred_element_type=F32)
        acc_s[...] = alpha[0:1, :] * acc_s[...] + pv

    s_cur = scores(0)
    for kt in range(n_kt):
        s_next = scores(kt + 1) if kt + 1 < n_kt else None
        softmax_values(kt, s_cur)
        s_cur = s_next

    acc = acc_s[...]
    o_t = acc[:HEAD_DIM] / acc[HEAD_DIM:HEAD_DIM + 1]
    for j in range(2):
        pair_t = jnp.concatenate([o_t[:, (2 * j) * tq:(2 * j + 1) * tq], o_t[:, (2 * j + 1) * tq:(2 * j + 2) * tq]],
                                 axis=0)
        o_ref[0, :, j * LANES:(j + 1) * LANES] = pair_t.T.astype(BF16)


def _gqa_attention(proj3, tq=256, tk=256):
    b, s, _ = proj3.shape
    assert (s // tk) % 2 == 0 and s // tk >= 2
    return pl.pallas_call(
        functools.partial(_gqa_kernel, tq=tq, tk=tk, seq=s),
        grid=(b, 2, s // tq),
        in_specs=[
            pl.BlockSpec((1, tq, CHUNK), lambda bi, g, i: (bi, i, CH_QC + g)),
            pl.BlockSpec((1, s, LANES), lambda bi, g, i: (bi, 0, 2 * CH_KC + g)),
            pl.BlockSpec((1, s, LANES), lambda bi, g, i: (bi, 0, 2 * CH_VC + g)),
        ],
        out_specs=pl.BlockSpec((1, tq, CHUNK), lambda bi, g, i: (bi, i, g)),
        out_shape=jax.ShapeDtypeStruct((b, s, 2 * CHUNK), BF16),
        scratch_shapes=[
            pltpu.VMEM((8, 4 * tq), F32),
            pltpu.VMEM((LANES, 4 * tq), F32),
        ],
        compiler_params=_cparams(("arbitrary", "arbitrary", "arbitrary")),
        name="gqa_attention",
    )(proj3, proj3, proj3)


def _rms(v, g):
    return v * lax.rsqrt(jnp.mean(v * v, axis=-1, keepdims=True) + RMS_EPS) * g


def _outproj_kernel(oa_ref, o1_ref, o4_ref, o16_ref, lw1_ref, lw4_ref, lw16_ref, oc_ref, x_ref,
                    w_ref, mg_ref, g2_ref, wrh_ref, wrl_ref, br_ref, x1_ref, h_ref, route_ref, cnt_ref, il_ref):
    tm = x_ref.shape[0]

    def interleaved(src_ref, dil, slot):
        for r in range(dil):
            for hf in range(CHUNK // LANES):
                piece = src_ref[0, :, r * CHUNK + hf * LANES:r * CHUNK + (hf + 1) * LANES].astype(F32)
                il_ref[slot * 2 + hf, pl.ds(r, tm // dil, stride=dil), :] = piece
        return jnp.concatenate([il_ref[slot * 2], il_ref[slot * 2 + 1]], axis=1)

    lw1 = lw1_ref[...]
    lw4, lw16 = interleaved(lw4_ref, 4, 0), interleaved(lw16_ref, 16, 1)
    o4, o16 = interleaved(o4_ref, 4, 2), interleaved(o16_ref, 16, 3)
    lwm = jnp.maximum(jnp.maximum(lw1, lw4), lw16)
    w1, w4, w16 = jnp.exp2(lw1 - lwm), jnp.exp2(lw4 - lwm), jnp.exp2(lw16 - lwm)
    ob = (w1 * o1_ref[...].astype(F32) + w4 * o4 + w16 * o16) / (w1 + w4 + w16)
    merged = jnp.concatenate([
        _rms(oa_ref[...].astype(F32), mg_ref[:, 0:CHUNK]).astype(BF16),
        _rms(ob, mg_ref[:, CHUNK:2 * CHUNK]).astype(BF16),
        _rms(oc_ref[...].astype(F32), mg_ref[:, 2 * CHUNK:]).astype(BF16),
    ], axis=-1)
    x1 = x_ref[...] + jnp.dot(merged, w_ref[...], preferred_element_type=F32)
    x1_ref[...] = x1
    h = _rms(x1, g2_ref[...])
    h_ref[...] = _pack_bf16_pair(h[:, :D_MODEL // 2], h[:, D_MODEL // 2:])

    hh = h.astype(BF16)
    hl = (h - hh.astype(F32)).astype(BF16)
    logits = (jnp.dot(hh, wrh_ref[...], preferred_element_type=F32)
              + jnp.dot(hl, wrh_ref[...], preferred_element_type=F32)
              + jnp.dot(hh, wrl_ref[...], preferred_element_type=F32)) + br_ref[...]
    lane = lax.broadcasted_iota(jnp.int32, logits.shape, 1)
    lane_f = lane.astype(F32)
    big = float(LANES)
    is_g = lane < N_GROUPS
    gl = jnp.where(is_g, logits, NEG)
    gmax = jnp.max(gl, axis=-1, keepdims=True)
    grp = jnp.min(jnp.where(gl == gmax, lane_f, big), axis=-1, keepdims=True)
    p_grp = 1.0 / jnp.sum(jnp.where(is_g, jnp.exp(logits - gmax), 0.0), axis=-1, keepdims=True)
    e_lo = N_GROUPS + EXPERTS_PER_GROUP * grp
    in_grp = (lane_f >= e_lo) & (lane_f < e_lo + EXPERTS_PER_GROUP)
    el = jnp.where(in_grp, logits, NEG)
    t1 = jnp.max(el, axis=-1, keepdims=True)
    i1 = jnp.min(jnp.where(el == t1, lane_f, big), axis=-1, keepdims=True)
    el2 = jnp.where(lane_f == i1, NEG, el)
    t2 = jnp.max(el2, axis=-1, keepdims=True)
    i2 = jnp.min(jnp.where(el2 == t2, lane_f, big), axis=-1, keepdims=True)
    e21 = jnp.exp(t2 - t1)
    gate1 = p_grp / (1.0 + e21)
    gate2 = p_grp * e21 / (1.0 + e21)
    route = jnp.where(lane == 0, i1 - N_GROUPS,
                      jnp.where(lane == 1, i2 - N_GROUPS,
                                jnp.where(lane == 2, gate1, jnp.where(lane == 3, gate2, 0.0))))
    route_ref[...] = route

    picked = jnp.where((lane_f == i1 - N_GROUPS) | (lane_f == i2 - N_GROUPS), 1.0, 0.0)

    @pl.when(pl.program_id(0) == 0)
    def _():
        cnt_ref[...] = jnp.zeros(cnt_ref.shape, F32)

    cnt_ref[...] = cnt_ref[...] + jnp.sum(picked, axis=0, keepdims=True)


def _outproj(oa, ob_parts, oc, x2, w_out, mix_g, g2, wr_hi, wr_lo, b_r, seq, tm=512):
    n = x2.shape[0]
    nsb = seq // tm
    (o1, lw1), (o4, lw4), (o16, lw16) = ob_parts
    o1, lw1 = o1.reshape(n, CHUNK), lw1.reshape(n, CHUNK)
    row = lambda w: pl.BlockSpec((tm, w), lambda i: (i, 0))
    full = lambda r, w: pl.BlockSpec((r, w), lambda i: (0, 0))
    dil = lambda d: pl.BlockSpec((1, tm // d, d * CHUNK), lambda i: (i // nsb, i % nsb, 0))
    return pl.pallas_call(
        _outproj_kernel,
        grid=(n // tm,),
        in_specs=[row(CHUNK), row(CHUNK), dil(4), dil(16), row(CHUNK), dil(4), dil(16),
                  row(2 * CHUNK), row(D_MODEL),
                  full(D_MODEL, D_MODEL), full(1, D_MODEL), full(1, D_MODEL),
                  full(D_MODEL, LANES), full(D_MODEL, LANES), full(1, LANES)],
        out_specs=[row(D_MODEL), row(D_MODEL // 2), row(LANES), full(8, LANES)],
        out_shape=[jax.ShapeDtypeStruct((n, D_MODEL), F32),
                   jax.ShapeDtypeStruct((n, D_MODEL // 2), jnp.uint32),
                   jax.ShapeDtypeStruct((n, LANES), F32),
                   jax.ShapeDtypeStruct((8, LANES), F32)],
        scratch_shapes=[pltpu.VMEM((8, tm, LANES), F32)],
        compiler_params=_cparams(("arbitrary",)),
        name="outproj_router",
    )(oa, o1, o4, o16, lw1, lw4, lw16, oc, x2, w_out, mix_g, g2, wr_hi, wr_lo, b_r)


META_ROWS = 384


def _rank_kernel(route_ref, cnt_ref, dest_ref, meta_ref, base_s, tri_s, *, tm):
    i = pl.program_id(0)
    lane = lax.broadcasted_iota(jnp.int32, (tm, LANES), 1)
    lane_f = lane.astype(F32)
    route = route_ref[...]
    oh0 = jnp.where(lane_f == route[:, 0:1], 1.0, 0.0)
    oh1 = jnp.where(lane_f == route[:, 1:2], 1.0, 0.0)
    oh = oh0 + oh1

    @pl.when(i == 0)
    def _():
        r = lax.broadcasted_iota(jnp.int32, (tm, tm), 0)
        c = lax.broadcasted_iota(jnp.int32, (tm, tm), 1)
        tri_s[...] = jnp.where(c < r, 1.0, 0.0).astype(BF16)
        lane8 = lax.broadcasted_iota(jnp.int32, cnt_ref.shape, 1)
        padded = jnp.floor((cnt_ref[...] + (MOE_BLOCK - 1)) * (1.0 / MOE_BLOCK)) * MOE_BLOCK
        ends = padded
        for sh in (1, 2, 4, 8, 16):
            ends = ends + jnp.where(lane8 >= sh, pltpu.roll(ends, sh, 1), 0.0)
        base_s[...] = ends - padded
        pos = lax.broadcasted_iota(jnp.int32, (META_ROWS, LANES), 0).astype(F32) * MOE_BLOCK
        lane_m = lax.broadcasted_iota(jnp.int32, (META_ROWS, LANES), 1)
        ends_row = ends[0:1, :]
        hit = jnp.where((lane_m < N_EXPERTS) & (ends_row <= pos), 1.0, 0.0)
        blk_e = jnp.minimum(jnp.sum(hit, axis=-1, keepdims=True), N_EXPERTS - 1.0)
        total = jnp.sum(jnp.where(lane_m[0:1] == N_EXPERTS - 1, ends_row, 0.0), axis=-1, keepdims=True)
        row_m = lax.broadcasted_iota(jnp.int32, (META_ROWS, 1), 0)
        meta = jnp.where(row_m == META_ROWS - 1, total * (1.0 / MOE_BLOCK), blk_e)
        meta_ref[...] = meta.astype(jnp.int32)

    prior = jnp.dot(tri_s[...], oh.astype(BF16), preferred_element_type=F32)
    val = base_s[0:1, :] + prior
    d0 = jnp.sum(oh0 * val, axis=-1, keepdims=True)
    d1 = jnp.sum(oh1 * val, axis=-1, keepdims=True)
    dest_ref[...] = jnp.where(lane == 0, d0, jnp.where(lane == 1, d1, 0.0)).astype(jnp.int32)
    base_s[...] = base_s[...] + jnp.sum(oh, axis=0, keepdims=True)


def _rank(route, counts, tm=512):
    n = route.shape[0]
    return pl.pallas_call(
        functools.partial(_rank_kernel, tm=tm),
        grid=(n // tm,),
        in_specs=[pl.BlockSpec((tm, LANES), lambda i: (i, 0)), pl.BlockSpec((8, LANES), lambda i: (0, 0))],
        out_specs=[pl.BlockSpec((tm, LANES), lambda i: (i, 0)),
                   pl.BlockSpec((META_ROWS, 1), lambda i: (0, 0))],
        out_shape=[jax.ShapeDtypeStruct((n, LANES), jnp.int32),
                   jax.ShapeDtypeStruct((META_ROWS, 1), jnp.int32)],
        scratch_shapes=[pltpu.VMEM((8, LANES), F32), pltpu.VMEM((tm, tm), BF16)],
        compiler_params=_cparams(("arbitrary",)),
        name="moe_rank",
    )(route, counts)


def _dispatch_kernel(dest_ref, h_ref, xs_in_hbm, xs_hbm, sem, *, n, tc):
    del xs_in_hbm
    t0 = pl.program_id(0) * tc

    def group(g, carry):
        base = pl.multiple_of(g * SUBLANES, SUBLANES)
        for u in range(SUBLANES):
            src = h_ref.at[pl.ds(base + u, 1)]
            pltpu.make_async_copy(src, xs_hbm.at[pl.ds(dest_ref[t0 + base + u], 1)], sem.at[0]).start(priority=0)
            pltpu.make_async_copy(src, xs_hbm.at[pl.ds(dest_ref[n + t0 + base + u], 1)], sem.at[0]).start(priority=1)
        return carry

    lax.fori_loop(0, tc // SUBLANES, group, 0)
    for _ in range(2):
        pltpu.make_async_copy(h_ref, xs_hbm.at[pl.ds(0, tc)], sem.at[0]).wait()


def _dispatch(dest_flat, h, slots, tc=512):
    n, w = h.shape
    cap = slots.shape[0]
    return pl.pallas_call(
        functools.partial(_dispatch_kernel, n=n, tc=tc),
        grid_spec=pltpu.PrefetchScalarGridSpec(
            num_scalar_prefetch=1,
            grid=(n // tc,),
            in_specs=[pl.BlockSpec((tc, w), lambda i, d: (i, 0)), pl.BlockSpec(memory_space=pl.ANY)],
            out_specs=pl.BlockSpec(memory_space=pl.ANY),
            scratch_shapes=[pltpu.SemaphoreType.DMA((1,))],
        ),
        out_shape=jax.ShapeDtypeStruct((cap, w), h.dtype),
        input_output_aliases={2: 0},
        compiler_params=_cparams(("arbitrary",)),
        name="moe_dispatch",
    )(dest_flat, h, slots)


def _expert_kernel(blk_e_ref, nact_ref, xs_ref, wg_ref, wu_ref, wd_ref, y_ref, wg_s, wu_s, wd_s):
    i = pl.program_id(0)
    active = i < nact_ref[0]
    changed = (i == 0) | (blk_e_ref[i] != blk_e_ref[jnp.maximum(i - 1, 0)])

    @pl.when(active & changed)
    def _():
        wg_s[...] = wg_ref[...].astype(BF16)
        wu_s[...] = wu_ref[...].astype(BF16)
        wd_s[...] = wd_ref[...].astype(BF16)

    @pl.when(active)
    def _():
        hi, lo = _unpack_bf16_pair(xs_ref[...])
        xb = jnp.concatenate([hi.astype(BF16), lo.astype(BF16)], axis=1)
        g = jnp.dot(xb, wg_s[...], preferred_element_type=F32)
        u = jnp.dot(xb, wu_s[...], preferred_element_type=F32)
        a = (g * jax.nn.sigmoid(g) * u).astype(BF16)
        y = jnp.dot(a, wd_s[...], preferred_element_type=F32)
        y_ref[...] = _pack_bf16_pair(y[:, :D_MODEL // 2], y[:, D_MODEL // 2:])

    @pl.when(jnp.logical_not(active))
    def _():
        y_ref[...] = jnp.zeros(y_ref.shape, y_ref.dtype)


def _experts(blk_e, nact, xs, w_gate, w_up, w_down, layer):
    cap = xs.shape[0]
    n_blk = cap // MOE_BLOCK

    def blk(i, be, na):
        return jnp.minimum(i, na[0] - 1)

    return pl.pallas_call(
        _expert_kernel,
        grid_spec=pltpu.PrefetchScalarGridSpec(
            num_scalar_prefetch=2,
            grid=(n_blk,),
            in_specs=[
                pl.BlockSpec((MOE_BLOCK, D_MODEL // 2), lambda i, be, na: (blk(i, be, na), 0)),
                pl.BlockSpec((None, None, D_MODEL, D_EXPERT), lambda i, be, na: (layer, be[blk(i, be, na)], 0, 0)),
                pl.BlockSpec((None, None, D_MODEL, D_EXPERT), lambda i, be, na: (layer, be[blk(i, be, na)], 0, 0)),
                pl.BlockSpec((None, None, D_EXPERT, D_MODEL), lambda i, be, na: (layer, be[blk(i, be, na)], 0, 0)),
            ],
            out_specs=pl.BlockSpec((MOE_BLOCK, D_MODEL // 2), lambda i, be, na: (i, 0)),
            scratch_shapes=[pltpu.VMEM((D_MODEL, D_EXPERT), BF16), pltpu.VMEM((D_MODEL, D_EXPERT), BF16),
                            pltpu.VMEM((D_EXPERT, D_MODEL), BF16)],
        ),
        out_shape=jax.ShapeDtypeStruct((cap, D_MODEL // 2), jnp.uint32),
        compiler_params=_cparams(("arbitrary",)),
        name="moe_experts",
    )(blk_e, nact, xs, w_gate, w_up, w_down)


def _combine_kernel(dest_ref, x1_ref, route_ref, y_hbm, o_ref, yb, sem, *, n, tc):
    i = pl.program_id(0)
    last = pl.num_programs(0) - 1
    slot = i % 2

    def gather(tile, sl):
        t0 = tile * tc

        def body(j, carry):
            for k in range(2):
                pltpu.make_async_copy(y_hbm.at[pl.ds(dest_ref[k * n + t0 + j], 1)], yb.at[sl, k, pl.ds(j, 1)],
                                      sem.at[sl]).start(priority=k)
            return carry

        lax.fori_loop(0, tc, body, 0, unroll=8)

    @pl.when(i == 0)
    def _():
        gather(0, 0)

    for k in range(2):
        pltpu.make_async_copy(y_hbm.at[pl.ds(0, tc)], yb.at[slot, k], sem.at[slot]).wait()

    @pl.when(i < last)
    def _():
        gather(i + 1, 1 - slot)

    route = route_ref[...]
    hi0, lo0 = _unpack_bf16_pair(yb[slot, 0])
    hi1, lo1 = _unpack_bf16_pair(yb[slot, 1])
    half = D_MODEL // 2
    o_ref[:, :half] = x1_ref[:, :half] + route[:, 2:3] * hi0 + route[:, 3:4] * hi1
    o_ref[:, half:] = x1_ref[:, half:] + route[:, 2:3] * lo0 + route[:, 3:4] * lo1


def _combine(dest_flat, x1, route, y, tc=512):
    n = x1.shape[0]
    return pl.pallas_call(
        functools.partial(_combine_kernel, n=n, tc=tc),
        grid_spec=pltpu.PrefetchScalarGridSpec(
            num_scalar_prefetch=1,
            grid=(n // tc,),
            in_specs=[pl.BlockSpec((tc, D_MODEL), lambda i, d: (i, 0)),
                      pl.BlockSpec((tc, LANES), lambda i, d: (i, 0)),
                      pl.BlockSpec(memory_space=pl.ANY)],
            out_specs=pl.BlockSpec((tc, D_MODEL), lambda i, d: (i, 0)),
            scratch_shapes=[pltpu.VMEM((2, 2, tc, D_MODEL // 2), jnp.uint32), pltpu.SemaphoreType.DMA((2,))],
        ),
        out_shape=jax.ShapeDtypeStruct((n, D_MODEL), F32),
        compiler_params=_cparams(("arbitrary",)),
        name="moe_combine",
    )(dest_flat, x1, route, y)


def _rope_tables(seq):
    n_freq = HEAD_DIM // 4
    t = np.arange(seq)
    inv_freq = jnp.asarray(ROPE_THETA, F32) ** (-jnp.arange(n_freq, dtype=F32) / n_freq)
    pos = jnp.stack([jnp.asarray(t // GRID_W, F32), jnp.asarray(t % GRID_W, F32)], axis=1)
    ang = pos[:, :, None] * inv_freq
    cos = jnp.broadcast_to(jnp.cos(ang)[:, :, None, :], (seq, 2, 2, n_freq)).reshape(seq, HEAD_DIM)
    sin = jnp.broadcast_to(jnp.sin(ang)[:, :, None, :], (seq, 2, 2, n_freq)).reshape(seq, HEAD_DIM)
    first_half = (np.arange(HEAD_DIM) % (2 * n_freq)) < n_freq
    sa = jnp.where(first_half, -sin, 0.0)
    sb = jnp.where(first_half, 0.0, sin)
    rep = CHUNK // HEAD_DIM
    return jnp.tile(cos, (1, rep)), jnp.tile(sa, (1, rep)), jnp.tile(sb, (1, rep))


def _prep_w_in(w):
    base = w[:, :2048]
    kc0, kc1 = w[:, 2048:2112], w[:, 2112:2176]
    vc0, vc1 = w[:, 2176:2240], w[:, 2240:2304]
    z = jnp.zeros_like(vc0)
    return jnp.concatenate([base, kc0, kc0, kc1, kc1, vc0, z, vc1, z], axis=1).astype(BF16)


def _prep_gain(qk_g):
    qs = SCALE * LOG2E
    ones = jnp.ones((CHUNK,), F32)
    t4 = lambda g: jnp.tile(g, CHUNK // HEAD_DIM)
    return jnp.concatenate([
        t4(qk_g[0, 0]) * qs, t4(qk_g[0, 1]), ones,
        t4(qk_g[1, 0]) * qs, t4(qk_g[1, 1]), ones,
        t4(qk_g[2, 0]) * qs, t4(qk_g[2, 0]) * qs, t4(qk_g[2, 1]), ones,
    ])[None, :]


def _split_hi_lo(w):
    hi = w.astype(BF16)
    lo = (w - hi.astype(F32)).astype(BF16)
    return hi, lo


def _mixer_layer(x2, bsz, seq, tables, norm1_g, w_in, qk_g, rpb, mix_g, w_out, norm2_g, w_rg, b_rg, w_re, b_re):
    n, d = x2.shape
    cos_t, sa_t, sb_t, bd = tables
    proj, pb4, pb16 = _inproj(x2, norm1_g[None, :], _prep_w_in(w_in), _prep_gain(qk_g), bd, cos_t, sa_t, sb_t, seq)
    proj3 = proj.reshape(bsz, seq, PROJ_W)
    oa = _na_attention(proj3, _na_bias_table(rpb)).reshape(n, CHUNK)
    ob_parts = [_dilated_branch(proj3[:, None], CH_QB, 1, 4), _dilated_branch(pb4, 0, 1, 4),
                _dilated_branch(pb16, 0, 2, 2)]
    oc = _gqa_attention(proj3).reshape(n, 2 * CHUNK)

    w_r = jnp.zeros((d, LANES), F32)
    w_r = w_r.at[:, :N_GROUPS].set(w_rg).at[:, N_GROUPS:N_GROUPS + N_EXPERTS].set(w_re)
    b_r = jnp.zeros((1, LANES), F32)
    b_r = b_r.at[0, :N_GROUPS].set(b_rg).at[0, N_GROUPS:N_GROUPS + N_EXPERTS].set(b_re)
    wr_hi, wr_lo = _split_hi_lo(w_r)
    return _outproj(oa, ob_parts, oc, x2, w_out.astype(BF16), mix_g[None, :], norm2_g[None, :], wr_hi, wr_lo, b_r,
                    seq)


def _moe_layer(x1, h, route, counts, slots, w_gate, w_up, w_down, layer):
    n = x1.shape[0]
    n_blk = slots.shape[0] // MOE_BLOCK
    dest, meta = _rank(route, counts)
    dest_flat = dest[:, :2].T.reshape(2 * n)
    xs = _dispatch(dest_flat, h, slots)
    y = _experts(meta[:n_blk, 0], meta[META_ROWS - 1:, 0], xs, w_gate, w_up, w_down, layer)
    return _combine(dest_flat, x1, route, y), xs


def _tables(seq):
    hd = np.arange(CHUNK) // HEAD_DIM
    return (*_rope_tables(seq), jnp.asarray((hd[:, None] == hd[None, :]) * (1.0 / HEAD_DIM), BF16))


@jax.jit
def kernel(x, norm1_g, w_in, qk_norm_g, na_rpb, mix_norm_g, w_out, norm2_g, w_router_group, b_router_group,
           w_router_expert, b_router_expert, w_gate, w_up, w_down):
    bsz, seq, d = x.shape
    tables = _tables(seq)
    x2 = x.reshape(bsz * seq, d)
    slots = jnp.zeros((2 * bsz * seq + N_EXPERTS * MOE_BLOCK, d // 2), jnp.uint32)
    for l in range(DEPTH):
        x1, h, route, counts = _mixer_layer(x2, bsz, seq, tables, norm1_g[l], w_in[l], qk_norm_g[l], na_rpb[l],
                                            mix_norm_g[l], w_out[l], norm2_g[l], w_router_group[l],
                                            b_router_group[l], w_router_expert[l], b_router_expert[l])
        x2, slots = _moe_layer(x1, h, route, counts, slots, w_gate, w_up, w_down, l)
    return x2.reshape(bsz, seq, d)
```

```python
import functools
import math

import numpy as np
import jax
import jax.numpy as jnp
from jax import lax
from jax.experimental import pallas as pl
from jax.experimental.pallas import tpu as pltpu

F32 = jnp.float32
BF16 = jnp.bfloat16

D_MODEL = 1024
DEPTH = 4
GRID_W = 64
HEAD_DIM = 64
RMS_EPS = 1e-6
NEG = -1e30
LOG2E = math.log2(math.e)
SCALE = HEAD_DIM ** -0.5

NA_ROWS = 8
NA_COLS = 16
DILATIONS = (1, 4, 16)
DIL_HALF = 64
ROPE_THETA = 10000.0

N_GROUPS = 4
EXPERTS_PER_GROUP = 8
N_EXPERTS = 32
D_EXPERT = 512
MOE_BLOCK = 512

LANES = 128
SUBLANES = 8
CHUNK = 256
CH_QA, CH_KA, CH_VA, CH_QB, CH_KB, CH_VB, CH_QC, CH_KC, CH_VC = 0, 1, 2, 3, 4, 5, 6, 8, 9
N_CHUNKS = 10
PROJ_W = N_CHUNKS * CHUNK
NORM_CHUNKS = (CH_QA, CH_KA, CH_QB, CH_KB, CH_QC, CH_QC + 1, CH_KC)
ROPE_CHUNKS = (CH_QC, CH_QC + 1, CH_KC)

VMEM_LIMIT = 56 * 1024 * 1024

NT_DIMS = (((1,), (1,)), ((), ()))


def _cparams(sem):
    return pltpu.CompilerParams(dimension_semantics=sem, vmem_limit_bytes=VMEM_LIMIT)


INPROJ_PARTS = 2


def _inproj_kernel(x_ref, g1_ref, w_ref, gain_ref, bd_ref, cos_ref, sa_ref, sb_ref, o_ref, o4_ref, o16_ref, stage_ref):
    tm = x_ref.shape[0]
    tp = tm // INPROJ_PARTS
    part_rows = [slice(i * tp, (i + 1) * tp) for i in range(INPROJ_PARTS)]
    chunk_cols = [slice(c * CHUNK, (c + 1) * CHUNK) for c in range(N_CHUNKS)]
    n_stage = 3 * CHUNK // LANES

    normed = []
    for rows in part_rows:
        x = x_ref[rows, :]
        ms = jnp.mean(x * x, axis=-1, keepdims=True)
        normed.append((x * lax.rsqrt(ms + RMS_EPS) * g1_ref[...]).astype(BF16))
    projs = [jnp.dot(h, w_ref[...], preferred_element_type=F32) for h in normed]
    for part, (rows, proj) in enumerate(zip(part_rows, projs)):
        meansq = {c: jnp.dot(jnp.square(proj[:, chunk_cols[c]]).astype(BF16), bd_ref[...],
                             preferred_element_type=F32) for c in NORM_CHUNKS}
        for c in range(N_CHUNKS):
            cols = chunk_cols[c]
            p = proj[:, cols]
            if c in NORM_CHUNKS:
                p = p * lax.rsqrt(meansq[c] + RMS_EPS) * gain_ref[:, cols]
            if c in ROPE_CHUNKS:
                p = (p * cos_ref[rows, :] + pltpu.roll(p, CHUNK - 16, 1) * sa_ref[rows, :]
                     + pltpu.roll(p, 16, 1) * sb_ref[rows, :])
            if c == CH_VC:
                lane = lax.broadcasted_iota(jnp.int32, p.shape, 1)
                p = jnp.where(lane % LANES >= HEAD_DIM, 1.0, p)
            o_ref[rows, cols] = p.astype(BF16)
            if c in (CH_QB, CH_KB, CH_VB):
                for hf in range(CHUNK // LANES):
                    lane0 = ((c - CH_QB) * (CHUNK // LANES) + hf) * LANES
                    slot = part * n_stage + lane0 // LANES
                    stage_ref[slot] = p[:, hf * LANES:(hf + 1) * LANES]
                    for dil, od_ref in ((4, o4_ref), (16, o16_ref)):
                        nr = tp // dil
                        for r in range(dil):
                            picked = stage_ref[slot, pl.ds(r, nr, stride=dil), :]
                            od_ref[0, r, part * nr:(part + 1) * nr, lane0:lane0 + LANES] = picked.astype(BF16)


def _inproj(x2, g1, w, gain, bd, cos_t, sa_t, sb_t, seq, tm=512):
    n = x2.shape[0]
    nsb = seq // tm
    bsz = n // seq
    dil_spec = lambda dil: pl.BlockSpec((1, dil, tm // dil, 3 * CHUNK), lambda i: (i // nsb, 0, i % nsb, 0))
    dil_shape = lambda dil: jax.ShapeDtypeStruct((bsz, dil, seq // dil, 3 * CHUNK), BF16)
    return pl.pallas_call(
        _inproj_kernel,
        grid=(n // tm,),
        in_specs=[
            pl.BlockSpec((tm, D_MODEL), lambda i: (i, 0)),
            pl.BlockSpec((1, D_MODEL), lambda i: (0, 0)),
            pl.BlockSpec((D_MODEL, PROJ_W), lambda i: (0, 0)),
            pl.BlockSpec((1, PROJ_W), lambda i: (0, 0)),
            pl.BlockSpec((CHUNK, CHUNK), lambda i: (0, 0)),
            pl.BlockSpec((tm, CHUNK), lambda i: (i % nsb, 0)),
            pl.BlockSpec((tm, CHUNK), lambda i: (i % nsb, 0)),
            pl.BlockSpec((tm, CHUNK), lambda i: (i % nsb, 0)),
        ],
        out_specs=[pl.BlockSpec((tm, PROJ_W), lambda i: (i, 0)), dil_spec(4), dil_spec(16)],
        out_shape=[jax.ShapeDtypeStruct((n, PROJ_W), BF16), dil_shape(4), dil_shape(16)],
        scratch_shapes=[pltpu.VMEM((INPROJ_PARTS * 3 * CHUNK // LANES, tm // INPROJ_PARTS, LANES), F32)],
        compiler_params=_cparams(("arbitrary",)),
        name="inproj",
    )(x2, g1, w, gain, bd, cos_t, sa_t, sb_t)


def _stack_head_pair(qp, lo):
    zero = jnp.zeros_like(qp)
    return jnp.concatenate([jnp.where(lo, qp, zero), jnp.where(lo, zero, qp)], axis=0)


PACKED = jnp.uint32


def _pack_bf16_pair(a, b):
    ua = lax.bitcast_convert_type(a.astype(BF16).astype(F32), PACKED)
    ub = lax.bitcast_convert_type(b.astype(BF16).astype(F32), PACKED)
    return ua | (ub >> 16)


def _unpack_bf16_pair(u):
    hi = lax.bitcast_convert_type(u & PACKED(0xFFFF0000), F32)
    lo = lax.bitcast_convert_type(u << 16, F32)
    return hi, lo


def _lo_mask(rows):
    return lax.broadcasted_iota(jnp.int32, (rows, LANES), 1) < HEAD_DIM


NA_ROW_GROUP = 4


def _na_kernel(q_ref, k_ref, v_ref, bias_ref, o_ref, *, rows_per_step, n_rows):
    i = pl.program_id(1)
    lo = _lo_mask(GRID_W)
    win = NA_ROWS * GRID_W

    def window(rl):
        r = i * rows_per_step + rl
        r0 = jnp.clip(r - NA_ROWS // 2, 0, n_rows - NA_ROWS)
        return pl.multiple_of(r0 * GRID_W, GRID_W), r - r0

    for g0 in range(0, rows_per_step, NA_ROW_GROUP):
        chains = [(rl, p) for rl in range(g0, g0 + NA_ROW_GROUP) for p in range(2)]
        scores = []
        for rl, p in chains:
            start, rr = window(rl)
            cols = slice(p * LANES, (p + 1) * LANES)
            qs = _stack_head_pair(q_ref[0, rl * GRID_W:(rl + 1) * GRID_W, cols], lo)
            s = lax.dot_general(qs, k_ref[0, pl.ds(start, win), cols], NT_DIMS, preferred_element_type=F32)
            scores.append(s + bias_ref[p, rr])
        stats = []
        for s in scores:
            e = jnp.exp2(s - jnp.max(s, axis=-1, keepdims=True))
            stats.append((e, jnp.sum(e, axis=-1, keepdims=True)))
        for (rl, p), (e, l) in zip(chains, stats):
            start, _ = window(rl)
            cols = slice(p * LANES, (p + 1) * LANES)
            pv = jnp.dot(e.astype(BF16), v_ref[0, pl.ds(start, win), cols], preferred_element_type=F32)
            o = pv / l
            o_ref[0, rl * GRID_W:(rl + 1) * GRID_W, cols] = jnp.where(lo, o[:GRID_W], o[GRID_W:]).astype(BF16)


def _na_attention(proj3, bias, rows_per_step=8):
    b, s, _ = proj3.shape
    n_rows = s // GRID_W
    tq = rows_per_step * GRID_W
    return pl.pallas_call(
        functools.partial(_na_kernel, rows_per_step=rows_per_step, n_rows=n_rows),
        grid=(b, n_rows // rows_per_step),
        in_specs=[
            pl.BlockSpec((1, tq, CHUNK), lambda bi, i: (bi, i, CH_QA)),
            pl.BlockSpec((1, s, CHUNK), lambda bi, i: (bi, 0, CH_KA)),
            pl.BlockSpec((1, s, CHUNK), lambda bi, i: (bi, 0, CH_VA)),
            pl.BlockSpec((2, NA_ROWS, 2 * GRID_W, NA_ROWS * GRID_W), lambda bi, i: (0, 0, 0, 0)),
        ],
        out_specs=pl.BlockSpec((1, tq, CHUNK), lambda bi, i: (bi, i, 0)),
        out_shape=jax.ShapeDtypeStruct((b, s, CHUNK), BF16),
        compiler_params=_cparams(("arbitrary", "arbitrary")),
        name="na_attention",
    )(proj3, proj3, proj3, bias)


def _na_bias_table(rpb):
    c = np.arange(GRID_W)
    cs = np.clip(c - NA_COLS // 2, 0, GRID_W - NA_COLS)
    kc = np.arange(GRID_W)
    inwin = (kc[None, :] >= cs[:, None]) & (kc[None, :] < cs[:, None] + NA_COLS)
    dc = kc[None, :] - c[:, None] + NA_COLS - 1
    rr = np.arange(NA_ROWS)
    dr = rr[None, :] - rr[:, None] + NA_ROWS - 1
    sel_r = jnp.asarray(dr[:, :, None] == np.arange(2 * NA_ROWS - 1), F32)
    sel_c = jnp.asarray((dc[:, :, None] == np.arange(2 * NA_COLS - 1)) & inwin[:, :, None], F32)
    tab = jnp.einsum("rka,hab,cjb->hrckj", sel_r, rpb, sel_c, precision=lax.Precision.HIGHEST)
    tab = jnp.where(inwin[None, None, :, None, :], tab * LOG2E, NEG)
    tab = tab.reshape(2, 2, NA_ROWS, GRID_W, NA_ROWS * GRID_W)
    return tab.transpose(0, 2, 1, 3, 4).reshape(2, NA_ROWS, 2 * GRID_W, NA_ROWS * GRID_W)


DIL_TQ = 128
DIL_WIN = DIL_TQ + 2 * DIL_HALF


def _dil_kernel(q_ref, k_ref, v_ref, bias_ref, o_ref, lw_ref, *, length, rg, ug, nb):
    u0 = pl.program_id(2) * ug
    lo = _lo_mask(DIL_TQ)
    chains = [(r, ub, p) for r in range(rg) for ub in range(ug) for p in range(2)]

    def window(ub):
        gb = u0 + ub
        ws = pl.multiple_of(jnp.clip(gb * DIL_TQ - DIL_HALF, 0, length - DIL_WIN), DIL_HALF)
        case = jnp.where(gb == 0, 0, jnp.where(gb == nb - 1, 2, 1))
        return ws, case

    scores = []
    for r, ub, p in chains:
        ws, case = window(ub)
        cols = slice(p * LANES, (p + 1) * LANES)
        qs = _stack_head_pair(q_ref[r, ub * DIL_TQ:(ub + 1) * DIL_TQ, cols], lo)
        s = lax.dot_general(qs, k_ref[r, pl.ds(ws, DIL_WIN), cols], NT_DIMS, preferred_element_type=F32)
        scores.append(s + bias_ref[case, p])
    stats = []
    for s in scores:
        m = jnp.max(s, axis=-1, keepdims=True)
        e = jnp.exp2(s - m)
        stats.append((m, e, jnp.sum(e, axis=-1, keepdims=True)))
    for (r, ub, p), (m, e, l) in zip(chains, stats):
        ws, _ = window(ub)
        cols = slice(p * LANES, (p + 1) * LANES)
        pv = jnp.dot(e.astype(BF16), v_ref[r, pl.ds(ws, DIL_WIN), cols], preferred_element_type=F32)
        o = pv / l
        lw = jnp.broadcast_to(m + jnp.log2(l), (2 * DIL_TQ, LANES))
        rows = slice(ub * DIL_TQ, (ub + 1) * DIL_TQ)
        ocols = slice(r * CHUNK + p * LANES, r * CHUNK + (p + 1) * LANES)
        o_ref[0, rows, ocols] = jnp.where(lo, o[:DIL_TQ], o[DIL_TQ:]).astype(BF16)
        lw_ref[0, rows, ocols] = jnp.where(lo, lw[:DIL_TQ], lw[DIL_TQ:])


def _dil_bias_table(dil):
    slopes = np.exp2(-8.0 * np.arange(1, 5) / 4.0)
    qi = np.arange(DIL_TQ)[:, None]
    kj = np.arange(DIL_WIN)[None, :]
    tab = np.zeros((3, 4, DIL_TQ, DIL_WIN), np.float32)
    for case, shift in enumerate((0, -DIL_HALF, -2 * DIL_HALF)):
        rel = kj + shift - qi
        for h in range(4):
            tab[case, h] = np.where(np.abs(rel) <= DIL_HALF, -slopes[h] * dil * np.abs(rel) * LOG2E, NEG)
    return tab.reshape(3, 2, 2 * DIL_TQ, DIL_WIN)


def _dilated_branch(src, q_chunk, rg, ug):
    b, dil, length, _ = src.shape
    nb = length // DIL_TQ
    o, lw = pl.pallas_call(
        functools.partial(_dil_kernel, length=length, rg=rg, ug=ug, nb=nb),
        grid=(b, dil // rg, nb // ug),
        in_specs=[
            pl.BlockSpec((None, rg, ug * DIL_TQ, CHUNK), lambda bi, r, u: (bi, r, u, q_chunk)),
            pl.BlockSpec((None, rg, length, CHUNK), lambda bi, r, u: (bi, r, 0, q_chunk + 1)),
            pl.BlockSpec((None, rg, length, CHUNK), lambda bi, r, u: (bi, r, 0, q_chunk + 2)),
            pl.BlockSpec((3, 2, 2 * DIL_TQ, DIL_WIN), lambda bi, r, u: (0, 0, 0, 0)),
        ],
        out_specs=[
            pl.BlockSpec((1, ug * DIL_TQ, rg * CHUNK), lambda bi, r, u: (bi, u, r)),
            pl.BlockSpec((1, ug * DIL_TQ, rg * CHUNK), lambda bi, r, u: (bi, u, r)),
        ],
        out_shape=[
            jax.ShapeDtypeStruct((b, length, dil * CHUNK), BF16),
            jax.ShapeDtypeStruct((b, length, dil * CHUNK), F32),
        ],
        compiler_params=_cparams(("arbitrary", "arbitrary", "arbitrary")),
        name=f"dilated_d{dil}",
    )(src, src, src, jnp.asarray(_dil_bias_table(dil)))
    return o, lw


TN_DIMS = (((0,), (0,)), ((), ()))


def _gqa_kernel(q_ref, k_ref, v_ref, o_ref, m_s, acc_s, *, tq, tk, seq):
    lo = _lo_mask(tq)
    n_kt = seq // tk
    qs = jnp.concatenate([_stack_head_pair(q_ref[0, :, j * LANES:(j + 1) * LANES], lo) for j in range(2)],
                         axis=0)
    m_s[...] = jnp.full(m_s.shape, NEG, F32)
    acc_s[...] = jnp.zeros(acc_s.shape, F32)

    def scores(kt):
        return lax.dot_general(k_ref[0, kt * tk:(kt + 1) * tk, :], qs, NT_DIMS, preferred_element_type=F32)

    def softmax_values(kt, s):
        k0 = kt * tk
        m_old = m_s[...]
        m_new = jnp.maximum(m_old, jnp.max(s, axis=0, keepdims=True))
        alpha = jnp.exp2(m_old - m_new)
        e = jnp.exp2((s - m_new[0:1, :]).astype(BF16))
        m_s[...] = m_new
        pv = lax.dot_general(v_ref[0, k0:k0 + tk, :], e, TN_DIMS, preferred_element_type=F32)
        acc_s[...] = alpha[0:1, :] * acc_s[...] + pv

    s_cur = scores(0)
    for kt in range(n_kt):
        s_next = scores(kt + 1) if kt + 1 < n_kt else None
        softmax_values(kt, s_cur)
        s_cur = s_next

    acc = acc_s[...]
    o_t = acc[:HEAD_DIM] / acc[HEAD_DIM:HEAD_DIM + 1]
    for j in range(2):
        pair_t = jnp.concatenate([o_t[:, (2 * j) * tq:(2 * j + 1) * tq], o_t[:, (2 * j + 1) * tq:(2 * j + 2) * tq]],
                                 axis=0)
        o_ref[0, :, j * LANES:(j + 1) * LANES] = pair_t.T.astype(BF16)


def _gqa_attention(proj3, tq=256, tk=256):
    b, s, _ = proj3.shape
    assert (s // tk) % 2 == 0 and s // tk >= 2
    return pl.pallas_call(
        functools.partial(_gqa_kernel, tq=tq, tk=tk, seq=s),
        grid=(b, 2, s // tq),
        in_specs=[
            pl.BlockSpec((1, tq, CHUNK), lambda bi, g, i: (bi, i, CH_QC + g)),
            pl.BlockSpec((1, s, LANES), lambda bi, g, i: (bi, 0, 2 * CH_KC + g)),
            pl.BlockSpec((1, s, LANES), lambda bi, g, i: (bi, 0, 2 * CH_VC + g)),
        ],
        out_specs=pl.BlockSpec((1, tq, CHUNK), lambda bi, g, i: (bi, i, g)),
        out_shape=jax.ShapeDtypeStruct((b, s, 2 * CHUNK), BF16),
        scratch_shapes=[
            pltpu.VMEM((8, 4 * tq), F32),
            pltpu.VMEM((LANES, 4 * tq), F32),
        ],
        compiler_params=_cparams(("arbitrary", "arbitrary", "arbitrary")),
        name="gqa_attention",
    )(proj3, proj3, proj3)


def _rms(v, g):
    return v * lax.rsqrt(jnp.mean(v * v, axis=-1, keepdims=True) + RMS_EPS) * g


def _outproj_kernel(oa_ref, o1_ref, o4_ref, o16_ref, lw1_ref, lw4_ref, lw16_ref, oc_ref, x_ref,
                    w_ref, mg_ref, g2_ref, wrh_ref, wrl_ref, br_ref, x1_ref, h_ref, route_ref, cnt_ref, il_ref):
    tm = x_ref.shape[0]

    def interleaved(src_ref, dil, slot):
        for r in range(dil):
            for hf in range(CHUNK // LANES):
                piece = src_ref[0, :, r * CHUNK + hf * LANES:r * CHUNK + (hf + 1) * LANES].astype(F32)
                il_ref[slot * 2 + hf, pl.ds(r, tm // dil, stride=dil), :] = piece
        return jnp.concatenate([il_ref[slot * 2], il_ref[slot * 2 + 1]], axis=1)

    lw1 = lw1_ref[...]
    lw4, lw16 = interleaved(lw4_ref, 4, 0), interleaved(lw16_ref, 16, 1)
    o4, o16 = interleaved(o4_ref, 4, 2), interleaved(o16_ref, 16, 3)
    lwm = jnp.maximum(jnp.maximum(lw1, lw4), lw16)
    w1, w4, w16 = jnp.exp2(lw1 - lwm), jnp.exp2(lw4 - lwm), jnp.exp2(lw16 - lwm)
    ob = (w1 * o1_ref[...].astype(F32) + w4 * o4 + w16 * o16) / (w1 + w4 + w16)
    merged = jnp.concatenate([
        _rms(oa_ref[...].astype(F32), mg_ref[:, 0:CHUNK]).astype(BF16),
        _rms(ob, mg_ref[:, CHUNK:2 * CHUNK]).astype(BF16),
        _rms(oc_ref[...].astype(F32), mg_ref[:, 2 * CHUNK:]).astype(BF16),
    ], axis=-1)
    x1 = x_ref[...] + jnp.dot(merged, w_ref[...], preferred_element_type=F32)
    x1_ref[...] = x1
    h = _rms(x1, g2_ref[...])
    h_ref[...] = _pack_bf16_pair(h[:, :D_MODEL // 2], h[:, D_MODEL // 2:])

    hh = h.astype(BF16)
    hl = (h - hh.astype(F32)).astype(BF16)
    logits = (jnp.dot(hh, wrh_ref[...], preferred_element_type=F32)
              + jnp.dot(hl, wrh_ref[...], preferred_element_type=F32)
              + jnp.dot(hh, wrl_ref[...], preferred_element_type=F32)) + br_ref[...]
    lane = lax.broadcasted_iota(jnp.int32, logits.shape, 1)
    lane_f = lane.astype(F32)
    big = float(LANES)
    is_g = lane < N_GROUPS
    gl = jnp.where(is_g, logits, NEG)
    gmax = jnp.max(gl, axis=-1, keepdims=True)
    grp = jnp.min(jnp.where(gl == gmax, lane_f, big), axis=-1, keepdims=True)
    p_grp = 1.0 / jnp.sum(jnp.where(is_g, jnp.exp(logits - gmax), 0.0), axis=-1, keepdims=True)
    e_lo = N_GROUPS + EXPERTS_PER_GROUP * grp
    in_grp = (lane_f >= e_lo) & (lane_f < e_lo + EXPERTS_PER_GROUP)
    el = jnp.where(in_grp, logits, NEG)
    t1 = jnp.max(el, axis=-1, keepdims=True)
    i1 = jnp.min(jnp.where(el == t1, lane_f, big), axis=-1, keepdims=True)
    el2 = jnp.where(lane_f == i1, NEG, el)
    t2 = jnp.max(el2, axis=-1, keepdims=True)
    i2 = jnp.min(jnp.where(el2 == t2, lane_f, big), axis=-1, keepdims=True)
    e21 = jnp.exp(t2 - t1)
    gate1 = p_grp / (1.0 + e21)
    gate2 = p_grp * e21 / (1.0 + e21)
    route = jnp.where(lane == 0, i1 - N_GROUPS,
                      jnp.where(lane == 1, i2 - N_GROUPS,
                                jnp.where(lane == 2, gate1, jnp.where(lane == 3, gate2, 0.0))))
    route_ref[...] = route

    picked = jnp.where((lane_f == i1 - N_GROUPS) | (lane_f == i2 - N_GROUPS), 1.0, 0.0)

    @pl.when(pl.program_id(0) == 0)
    def _():
        cnt_ref[...] = jnp.zeros(cnt_ref.shape, F32)

    cnt_ref[...] = cnt_ref[...] + jnp.sum(picked, axis=0, keepdims=True)


def _outproj(oa, ob_parts, oc, x2, w_out, mix_g, g2, wr_hi, wr_lo, b_r, seq, tm=512):
    n = x2.shape[0]
    nsb = seq // tm
    (o1, lw1), (o4, lw4), (o16, lw16) = ob_parts
    o1, lw1 = o1.reshape(n, CHUNK), lw1.reshape(n, CHUNK)
    row = lambda w: pl.BlockSpec((tm, w), lambda i: (i, 0))
    full = lambda r, w: pl.BlockSpec((r, w), lambda i: (0, 0))
    dil = lambda d: pl.BlockSpec((1, tm // d, d * CHUNK), lambda i: (i // nsb, i % nsb, 0))
    return pl.pallas_call(
        _outproj_kernel,
        grid=(n // tm,),
        in_specs=[row(CHUNK), row(CHUNK), dil(4), dil(16), row(CHUNK), dil(4), dil(16),
                  row(2 * CHUNK), row(D_MODEL),
                  full(D_MODEL, D_MODEL), full(1, D_MODEL), full(1, D_MODEL),
                  full(D_MODEL, LANES), full(D_MODEL, LANES), full(1, LANES)],
        out_specs=[row(D_MODEL), row(D_MODEL // 2), row(LANES), full(8, LANES)],
        out_shape=[jax.ShapeDtypeStruct((n, D_MODEL), F32),
                   jax.ShapeDtypeStruct((n, D_MODEL // 2), PACKED),
                   jax.ShapeDtypeStruct((n, LANES), F32),
                   jax.ShapeDtypeStruct((8, LANES), F32)],
        scratch_shapes=[pltpu.VMEM((8, tm, LANES), F32)],
        compiler_params=_cparams(("arbitrary",)),
        name="outproj_router",
    )(oa, o1, o4, o16, lw1, lw4, lw16, oc, x2, w_out, mix_g, g2, wr_hi, wr_lo, b_r)


META_ROWS = 384


def _rank_kernel(route_ref, cnt_ref, dest_ref, meta_ref, base_s, tri_s, *, tm):
    i = pl.program_id(0)
    lane = lax.broadcasted_iota(jnp.int32, (tm, LANES), 1)
    lane_f = lane.astype(F32)
    route = route_ref[...]
    oh0 = jnp.where(lane_f == route[:, 0:1], 1.0, 0.0)
    oh1 = jnp.where(lane_f == route[:, 1:2], 1.0, 0.0)
    oh = oh0 + oh1

    @pl.when(i == 0)
    def _():
        r = lax.broadcasted_iota(jnp.int32, (tm, tm), 0)
        c = lax.broadcasted_iota(jnp.int32, (tm, tm), 1)
        tri_s[...] = jnp.where(c < r, 1.0, 0.0).astype(BF16)
        lane8 = lax.broadcasted_iota(jnp.int32, cnt_ref.shape, 1)
        padded = jnp.floor((cnt_ref[...] + (MOE_BLOCK - 1)) * (1.0 / MOE_BLOCK)) * MOE_BLOCK
        ends = padded
        for sh in (1, 2, 4, 8, 16):
            ends = ends + jnp.where(lane8 >= sh, pltpu.roll(ends, sh, 1), 0.0)
        base_s[...] = ends - padded
        pos = lax.broadcasted_iota(jnp.int32, (META_ROWS, LANES), 0).astype(F32) * MOE_BLOCK
        lane_m = lax.broadcasted_iota(jnp.int32, (META_ROWS, LANES), 1)
        ends_row = ends[0:1, :]
        hit = jnp.where((lane_m < N_EXPERTS) & (ends_row <= pos), 1.0, 0.0)
        blk_e = jnp.minimum(jnp.sum(hit, axis=-1, keepdims=True), N_EXPERTS - 1.0)
        total = jnp.sum(jnp.where(lane_m[0:1] == N_EXPERTS - 1, ends_row, 0.0), axis=-1, keepdims=True)
        row_m = lax.broadcasted_iota(jnp.int32, (META_ROWS, 1), 0)
        meta = jnp.where(row_m == META_ROWS - 1, total * (1.0 / MOE_BLOCK), blk_e)
        meta_ref[...] = meta.astype(jnp.int32)

    prior = jnp.dot(tri_s[...], oh.astype(BF16), preferred_element_type=F32)
    val = base_s[0:1, :] + prior
    d0 = jnp.sum(oh0 * val, axis=-1, keepdims=True)
    d1 = jnp.sum(oh1 * val, axis=-1, keepdims=True)
    dest_ref[...] = jnp.where(lane == 0, d0, jnp.where(lane == 1, d1, 0.0)).astype(jnp.int32)
    base_s[...] = base_s[...] + jnp.sum(oh, axis=0, keepdims=True)


def _rank(route, counts, tm=512):
    n = route.shape[0]
    return pl.pallas_call(
        functools.partial(_rank_kernel, tm=tm),
        grid=(n // tm,),
        in_specs=[pl.BlockSpec((tm, LANES), lambda i: (i, 0)), pl.BlockSpec((8, LANES), lambda i: (0, 0))],
        out_specs=[pl.BlockSpec((tm, LANES), lambda i: (i, 0)),
                   pl.BlockSpec((META_ROWS, 1), lambda i: (0, 0))],
        out_shape=[jax.ShapeDtypeStruct((n, LANES), jnp.int32),
                   jax.ShapeDtypeStruct((META_ROWS, 1), jnp.int32)],
        scratch_shapes=[pltpu.VMEM((8, LANES), F32), pltpu.VMEM((tm, tm), BF16)],
        compiler_params=_cparams(("arbitrary",)),
        name="moe_rank",
    )(route, counts)


DISPATCH_BUFS = 3


def _dispatch_kernel(dest_ref, h_hbm, xs_in_hbm, xs_hbm, hbuf, in_sem, out_sem, *, n, tc):
    del xs_in_hbm
    i = pl.program_id(0)
    nt = pl.num_programs(0)
    buf = i % DISPATCH_BUFS
    prev = (i + DISPATCH_BUFS - 1) % DISPATCH_BUFS
    t0 = i * tc

    def load(tile, b):
        return pltpu.make_async_copy(h_hbm.at[pl.ds(tile * tc, tc)], hbuf.at[b], in_sem.at[b])

    def drain_rows(b):
        for _ in range(2):
            pltpu.make_async_copy(hbuf.at[b], xs_hbm.at[pl.ds(0, tc)], out_sem.at[b]).wait()

    @pl.when(i == 0)
    def _():
        load(0, 0).start()

        @pl.when(nt > 1)
        def _():
            load(1, 1).start()

    load(i, buf).wait()

    def group(g, carry):
        base = pl.multiple_of(g * SUBLANES, SUBLANES)
        for u in range(SUBLANES):
            src = hbuf.at[buf, pl.ds(base + u, 1)]
            for k in range(2):
                pltpu.make_async_copy(src, xs_hbm.at[pl.ds(dest_ref[k * n + t0 + base + u], 1)],
                                      out_sem.at[buf]).start(priority=k)
        return carry

    lax.fori_loop(0, tc // SUBLANES, group, 0)

    @pl.when(i > 0)
    def _():
        drain_rows(prev)

    @pl.when(i + 2 < nt)
    def _():
        load(i + 2, prev).start()

    @pl.when(i == nt - 1)
    def _():
        drain_rows(buf)


def _dispatch(dest_flat, h, slots, tc=512):
    n, w = h.shape
    cap = slots.shape[0]
    return pl.pallas_call(
        functools.partial(_dispatch_kernel, n=n, tc=tc),
        grid_spec=pltpu.PrefetchScalarGridSpec(
            num_scalar_prefetch=1,
            grid=(n // tc,),
            in_specs=[pl.BlockSpec(memory_space=pl.ANY), pl.BlockSpec(memory_space=pl.ANY)],
            out_specs=pl.BlockSpec(memory_space=pl.ANY),
            scratch_shapes=[pltpu.VMEM((DISPATCH_BUFS, tc, w), h.dtype),
                            pltpu.SemaphoreType.DMA((DISPATCH_BUFS,)), pltpu.SemaphoreType.DMA((DISPATCH_BUFS,))],
        ),
        out_shape=jax.ShapeDtypeStruct((cap, w), h.dtype),
        input_output_aliases={2: 0},
        compiler_params=_cparams(("arbitrary",)),
        name="moe_dispatch",
    )(dest_flat, h, slots)


def _expert_kernel(blk_e_ref, nact_ref, xs_ref, wg_ref, wu_ref, wd_ref, y_ref, wg_s, wu_s, wd_s):
    i = pl.program_id(0)
    active = i < nact_ref[0]
    changed = (i == 0) | (blk_e_ref[i] != blk_e_ref[jnp.maximum(i - 1, 0)])

    @pl.when(active & changed)
    def _():
        wg_s[...] = wg_ref[...].astype(BF16)
        wu_s[...] = wu_ref[...].astype(BF16)
        wd_s[...] = wd_ref[...].astype(BF16)

    @pl.when(active)
    def _():
        hi, lo = _unpack_bf16_pair(xs_ref[...])
        xb = jnp.concatenate([hi.astype(BF16), lo.astype(BF16)], axis=1)
        g = jnp.dot(xb, wg_s[...], preferred_element_type=F32)
        u = jnp.dot(xb, wu_s[...], preferred_element_type=F32)
        a = (g * jax.nn.sigmoid(g) * u).astype(BF16)
        y = jnp.dot(a, wd_s[...], preferred_element_type=F32)
        y_ref[...] = _pack_bf16_pair(y[:, :D_MODEL // 2], y[:, D_MODEL // 2:])

    @pl.when(jnp.logical_not(active))
    def _():
        y_ref[...] = jnp.zeros(y_ref.shape, y_ref.dtype)


def _experts(blk_e, nact, xs, w_gate, w_up, w_down, layer):
    cap = xs.shape[0]
    n_blk = cap // MOE_BLOCK

    def blk(i, be, na):
        return jnp.minimum(i, na[0] - 1)

    return pl.pallas_call(
        _expert_kernel,
        grid_spec=pltpu.PrefetchScalarGridSpec(
            num_scalar_prefetch=2,
            grid=(n_blk,),
            in_specs=[
                pl.BlockSpec((MOE_BLOCK, D_MODEL // 2), lambda i, be, na: (blk(i, be, na), 0)),
                pl.BlockSpec((None, None, D_MODEL, D_EXPERT), lambda i, be, na: (layer, be[blk(i, be, na)], 0, 0)),
                pl.BlockSpec((None, None, D_MODEL, D_EXPERT), lambda i, be, na: (layer, be[blk(i, be, na)], 0, 0)),
                pl.BlockSpec((None, None, D_EXPERT, D_MODEL), lambda i, be, na: (layer, be[blk(i, be, na)], 0, 0)),
            ],
            out_specs=pl.BlockSpec((MOE_BLOCK, D_MODEL // 2), lambda i, be, na: (i, 0)),
            scratch_shapes=[pltpu.VMEM((D_MODEL, D_EXPERT), BF16), pltpu.VMEM((D_MODEL, D_EXPERT), BF16),
                            pltpu.VMEM((D_EXPERT, D_MODEL), BF16)],
        ),
        out_shape=jax.ShapeDtypeStruct((cap, D_MODEL // 2), PACKED),
        compiler_params=_cparams(("arbitrary",)),
        name="moe_experts",
    )(blk_e, nact, xs, w_gate, w_up, w_down)


def _combine_kernel(dest_ref, x1_ref, route_ref, y_hbm, o_ref, yb, sem, *, n, tc):
    i = pl.program_id(0)
    last = pl.num_programs(0) - 1
    slot = i % 2

    def gather(tile, sl):
        t0 = tile * tc

        def body(j, carry):
            for k in range(2):
                pltpu.make_async_copy(y_hbm.at[pl.ds(dest_ref[k * n + t0 + j], 1)], yb.at[sl, k, pl.ds(j, 1)],
                                      sem.at[sl]).start(priority=k)
            return carry

        lax.fori_loop(0, tc, body, 0, unroll=8)

    @pl.when(i == 0)
    def _():
        gather(0, 0)

    for k in range(2):
        pltpu.make_async_copy(y_hbm.at[pl.ds(0, tc)], yb.at[slot, k], sem.at[slot]).wait()

    @pl.when(i < last)
    def _():
        gather(i + 1, 1 - slot)

    route = route_ref[...]
    hi0, lo0 = _unpack_bf16_pair(yb[slot, 0])
    hi1, lo1 = _unpack_bf16_pair(yb[slot, 1])
    half = D_MODEL // 2
    o_ref[:, :half] = x1_ref[:, :half] + route[:, 2:3] * hi0 + route[:, 3:4] * hi1
    o_ref[:, half:] = x1_ref[:, half:] + route[:, 2:3] * lo0 + route[:, 3:4] * lo1


def _combine(dest_flat, x1, route, y, tc=512):
    n = x1.shape[0]
    return pl.pallas_call(
        functools.partial(_combine_kernel, n=n, tc=tc),
        grid_spec=pltpu.PrefetchScalarGridSpec(
            num_scalar_prefetch=1,
            grid=(n // tc,),
            in_specs=[pl.BlockSpec((tc, D_MODEL), lambda i, d: (i, 0)),
                      pl.BlockSpec((tc, LANES), lambda i, d: (i, 0)),
                      pl.BlockSpec(memory_space=pl.ANY)],
            out_specs=pl.BlockSpec((tc, D_MODEL), lambda i, d: (i, 0)),
            scratch_shapes=[pltpu.VMEM((2, 2, tc, D_MODEL // 2), PACKED), pltpu.SemaphoreType.DMA((2,))],
        ),
        out_shape=jax.ShapeDtypeStruct((n, D_MODEL), F32),
        compiler_params=_cparams(("arbitrary",)),
        name="moe_combine",
    )(dest_flat, x1, route, y)


def _rope_tables(seq):
    n_freq = HEAD_DIM // 4
    t = np.arange(seq)
    inv_freq = jnp.asarray(ROPE_THETA, F32) ** (-jnp.arange(n_freq, dtype=F32) / n_freq)
    pos = jnp.stack([jnp.asarray(t // GRID_W, F32), jnp.asarray(t % GRID_W, F32)], axis=1)
    ang = pos[:, :, None] * inv_freq
    cos = jnp.broadcast_to(jnp.cos(ang)[:, :, None, :], (seq, 2, 2, n_freq)).reshape(seq, HEAD_DIM)
    sin = jnp.broadcast_to(jnp.sin(ang)[:, :, None, :], (seq, 2, 2, n_freq)).reshape(seq, HEAD_DIM)
    first_half = (np.arange(HEAD_DIM) % (2 * n_freq)) < n_freq
    sa = jnp.where(first_half, -sin, 0.0)
    sb = jnp.where(first_half, 0.0, sin)
    rep = CHUNK // HEAD_DIM
    return jnp.tile(cos, (1, rep)), jnp.tile(sa, (1, rep)), jnp.tile(sb, (1, rep))


def _prep_w_in(w):
    base = w[:, :2048]
    kc0, kc1 = w[:, 2048:2112], w[:, 2112:2176]
    vc0, vc1 = w[:, 2176:2240], w[:, 2240:2304]
    z = jnp.zeros_like(vc0)
    return jnp.concatenate([base, kc0, kc0, kc1, kc1, vc0, z, vc1, z], axis=1).astype(BF16)


def _prep_gain(qk_g):
    qs = SCALE * LOG2E
    ones = jnp.ones((CHUNK,), F32)
    t4 = lambda g: jnp.tile(g, CHUNK // HEAD_DIM)
    return jnp.concatenate([
        t4(qk_g[0, 0]) * qs, t4(qk_g[0, 1]), ones,
        t4(qk_g[1, 0]) * qs, t4(qk_g[1, 1]), ones,
        t4(qk_g[2, 0]) * qs, t4(qk_g[2, 0]) * qs, t4(qk_g[2, 1]), ones,
    ])[None, :]


def _split_hi_lo(w):
    hi = w.astype(BF16)
    lo = (w - hi.astype(F32)).astype(BF16)
    return hi, lo


def _mixer_layer(x2, bsz, seq, tables, norm1_g, w_in, qk_g, rpb, mix_g, w_out, norm2_g, w_rg, b_rg, w_re, b_re):
    n, d = x2.shape
    cos_t, sa_t, sb_t, bd = tables
    proj, pb4, pb16 = _inproj(x2, norm1_g[None, :], _prep_w_in(w_in), _prep_gain(qk_g), bd, cos_t, sa_t, sb_t, seq)
    proj3 = proj.reshape(bsz, seq, PROJ_W)
    oa = _na_attention(proj3, _na_bias_table(rpb)).reshape(n, CHUNK)
    ob_parts = [_dilated_branch(proj3[:, None], CH_QB, 1, 4), _dilated_branch(pb4, 0, 1, 4),
                _dilated_branch(pb16, 0, 2, 2)]
    oc = _gqa_attention(proj3).reshape(n, 2 * CHUNK)

    w_r = jnp.zeros((d, LANES), F32)
    w_r = w_r.at[:, :N_GROUPS].set(w_rg).at[:, N_GROUPS:N_GROUPS + N_EXPERTS].set(w_re)
    b_r = jnp.zeros((1, LANES), F32)
    b_r = b_r.at[0, :N_GROUPS].set(b_rg).at[0, N_GROUPS:N_GROUPS + N_EXPERTS].set(b_re)
    wr_hi, wr_lo = _split_hi_lo(w_r)
    return _outproj(oa, ob_parts, oc, x2, w_out.astype(BF16), mix_g[None, :], norm2_g[None, :], wr_hi, wr_lo, b_r,
                    seq)


def _moe_layer(x1, h, route, counts, slots, w_gate, w_up, w_down, layer):
    n = x1.shape[0]
    n_blk = slots.shape[0] // MOE_BLOCK
    dest, meta = _rank(route, counts)
    dest_flat = dest[:, :2].T.reshape(2 * n)
    xs = _dispatch(dest_flat, h, slots)
    y = _experts(meta[:n_blk, 0], meta[META_ROWS - 1:, 0], xs, w_gate, w_up, w_down, layer)
    return _combine(dest_flat, x1, route, y), xs


def _tables(seq):
    hd = np.arange(CHUNK) // HEAD_DIM
    return (*_rope_tables(seq), jnp.asarray((hd[:, None] == hd[None, :]) * (1.0 / HEAD_DIM), BF16))


@jax.jit
def kernel(x, norm1_g, w_in, qk_norm_g, na_rpb, mix_norm_g, w_out, norm2_g, w_router_group, b_router_group,
           w_router_expert, b_router_expert, w_gate, w_up, w_down):
    bsz, seq, d = x.shape
    tables = _tables(seq)
    x2 = x.reshape(bsz * seq, d)
    slots = jnp.zeros((2 * bsz * seq + N_EXPERTS * MOE_BLOCK, d // 2), PACKED)
    for l in range(DEPTH):
        x1, h, route, counts = _mixer_layer(x2, bsz, seq, tables, norm1_g[l], w_in[l], qk_norm_g[l], na_rpb[l],
                                            mix_norm_g[l], w_out[l], norm2_g[l], w_router_group[l],
                                            b_router_group[l], w_router_expert[l], b_router_expert[l])
        x2, slots = _moe_layer(x1, h, route, counts, slots, w_gate, w_up, w_down, l)
    return x2.reshape(bsz, seq, d)
```

```python
import functools
import math

import numpy as np
import jax
import jax.numpy as jnp
from jax import lax
from jax.experimental import pallas as pl
from jax.experimental.pallas import tpu as pltpu

F32 = jnp.float32
BF16 = jnp.bfloat16

D_MODEL = 1024
DEPTH = 4
GRID_W = 64
HEAD_DIM = 64
RMS_EPS = 1e-6
NEG = -1e30
LOG2E = math.log2(math.e)
SCALE = HEAD_DIM ** -0.5

NA_ROWS = 8
NA_COLS = 16
DILATIONS = (1, 4, 16)
DIL_HALF = 64
ROPE_THETA = 10000.0

N_GROUPS = 4
EXPERTS_PER_GROUP = 8
N_EXPERTS = 32
D_EXPERT = 512
MOE_BLOCK = 512

LANES = 128
SUBLANES = 8
CHUNK = 256
CH_QA, CH_KA, CH_VA, CH_QB, CH_KB, CH_VB, CH_QC, CH_KC, CH_VC = 0, 1, 2, 3, 4, 5, 6, 8, 9
N_CHUNKS = 10
PROJ_W = N_CHUNKS * CHUNK
NORM_CHUNKS = (CH_QA, CH_KA, CH_QB, CH_KB, CH_QC, CH_QC + 1, CH_KC)
ROPE_CHUNKS = (CH_QC, CH_QC + 1, CH_KC)

VMEM_LIMIT = 56 * 1024 * 1024

NT_DIMS = (((1,), (1,)), ((), ()))


def _cparams(sem):
    return pltpu.CompilerParams(dimension_semantics=sem, vmem_limit_bytes=VMEM_LIMIT)


INPROJ_PARTS = 2


def _inproj_kernel(x_ref, g1_ref, w_ref, gain_ref, bd_ref, cos_ref, sa_ref, sb_ref, o_ref, o4_ref, o16_ref, stage_ref):
    tm = x_ref.shape[0]
    tp = tm // INPROJ_PARTS
    part_rows = [slice(i * tp, (i + 1) * tp) for i in range(INPROJ_PARTS)]
    chunk_cols = [slice(c * CHUNK, (c + 1) * CHUNK) for c in range(N_CHUNKS)]
    n_stage = 3 * CHUNK // LANES

    normed = []
    for rows in part_rows:
        x = x_ref[rows, :]
        ms = jnp.mean(x * x, axis=-1, keepdims=True)
        normed.append((x * lax.rsqrt(ms + RMS_EPS) * g1_ref[...]).astype(BF16))
    projs = [jnp.dot(h, w_ref[...], preferred_element_type=F32) for h in normed]
    for part, (rows, proj) in enumerate(zip(part_rows, projs)):
        meansq = {c: jnp.dot(jnp.square(proj[:, chunk_cols[c]]).astype(BF16), bd_ref[...],
                             preferred_element_type=F32) for c in NORM_CHUNKS}
        for c in range(N_CHUNKS):
            cols = chunk_cols[c]
            p = proj[:, cols]
            if c in NORM_CHUNKS:
                p = p * lax.rsqrt(meansq[c] + RMS_EPS) * gain_ref[:, cols]
            if c in ROPE_CHUNKS:
                p = (p * cos_ref[rows, :] + pltpu.roll(p, CHUNK - 16, 1) * sa_ref[rows, :]
                     + pltpu.roll(p, 16, 1) * sb_ref[rows, :])
            if c == CH_VC:
                lane = lax.broadcasted_iota(jnp.int32, p.shape, 1)
                p = jnp.where(lane % LANES >= HEAD_DIM, 1.0, p)
            o_ref[rows, cols] = p.astype(BF16)
            if c in (CH_QB, CH_KB, CH_VB):
                for hf in range(CHUNK // LANES):
                    lane0 = ((c - CH_QB) * (CHUNK // LANES) + hf) * LANES
                    slot = part * n_stage + lane0 // LANES
                    stage_ref[slot] = p[:, hf * LANES:(hf + 1) * LANES]
                    for dil, od_ref in ((4, o4_ref), (16, o16_ref)):
                        nr = tp // dil
                        for r in range(dil):
                            picked = stage_ref[slot, pl.ds(r, nr, stride=dil), :]
                            od_ref[0, r, part * nr:(part + 1) * nr, lane0:lane0 + LANES] = picked.astype(BF16)


def _inproj(x2, g1, w, gain, bd, cos_t, sa_t, sb_t, seq, tm=512):
    n = x2.shape[0]
    nsb = seq // tm
    bsz = n // seq
    dil_spec = lambda dil: pl.BlockSpec((1, dil, tm // dil, 3 * CHUNK), lambda i: (i // nsb, 0, i % nsb, 0))
    dil_shape = lambda dil: jax.ShapeDtypeStruct((bsz, dil, seq // dil, 3 * CHUNK), BF16)
    return pl.pallas_call(
        _inproj_kernel,
        grid=(n // tm,),
        in_specs=[
            pl.BlockSpec((tm, D_MODEL), lambda i: (i, 0)),
            pl.BlockSpec((1, D_MODEL), lambda i: (0, 0)),
            pl.BlockSpec((D_MODEL, PROJ_W), lambda i: (0, 0)),
            pl.BlockSpec((1, PROJ_W), lambda i: (0, 0)),
            pl.BlockSpec((CHUNK, CHUNK), lambda i: (0, 0)),
            pl.BlockSpec((tm, CHUNK), lambda i: (i % nsb, 0)),
            pl.BlockSpec((tm, CHUNK), lambda i: (i % nsb, 0)),
            pl.BlockSpec((tm, CHUNK), lambda i: (i % nsb, 0)),
        ],
        out_specs=[pl.BlockSpec((tm, PROJ_W), lambda i: (i, 0)), dil_spec(4), dil_spec(16)],
        out_shape=[jax.ShapeDtypeStruct((n, PROJ_W), BF16), dil_shape(4), dil_shape(16)],
        scratch_shapes=[pltpu.VMEM((INPROJ_PARTS * 3 * CHUNK // LANES, tm // INPROJ_PARTS, LANES), F32)],
        compiler_params=_cparams(("arbitrary",)),
        name="inproj",
    )(x2, g1, w, gain, bd, cos_t, sa_t, sb_t)


def _stack_head_pair(qp, lo):
    zero = jnp.zeros_like(qp)
    return jnp.concatenate([jnp.where(lo, qp, zero), jnp.where(lo, zero, qp)], axis=0)


PACKED = jnp.uint32


def _pack_bf16_pair(a, b):
    ua = lax.bitcast_convert_type(a.astype(BF16).astype(F32), PACKED)
    ub = lax.bitcast_convert_type(b.astype(BF16).astype(F32), PACKED)
    return ua | (ub >> 16)


def _unpack_bf16_pair(u):
    hi = lax.bitcast_convert_type(u & PACKED(0xFFFF0000), F32)
    lo = lax.bitcast_convert_type(u << 16, F32)
    return hi, lo


def _lo_mask(rows):
    return lax.broadcasted_iota(jnp.int32, (rows, LANES), 1) < HEAD_DIM


NA_ROW_GROUP = 4


def _na_kernel(q_ref, k_ref, v_ref, bias_ref, o_ref, *, rows_per_step, n_rows):
    i = pl.program_id(1)
    lo = _lo_mask(GRID_W)
    win = NA_ROWS * GRID_W

    def window(rl):
        r = i * rows_per_step + rl
        r0 = jnp.clip(r - NA_ROWS // 2, 0, n_rows - NA_ROWS)
        return pl.multiple_of(r0 * GRID_W, GRID_W), r - r0

    for g0 in range(0, rows_per_step, NA_ROW_GROUP):
        chains = [(rl, p) for rl in range(g0, g0 + NA_ROW_GROUP) for p in range(2)]
        scores = []
        for rl, p in chains:
            start, rr = window(rl)
            cols = slice(p * LANES, (p + 1) * LANES)
            qs = _stack_head_pair(q_ref[0, rl * GRID_W:(rl + 1) * GRID_W, cols], lo)
            s = lax.dot_general(qs, k_ref[0, pl.ds(start, win), cols], NT_DIMS, preferred_element_type=F32)
            scores.append(s + bias_ref[p, rr])
        stats = []
        for s in scores:
            e = jnp.exp2(s - jnp.max(s, axis=-1, keepdims=True))
            stats.append((e, jnp.sum(e, axis=-1, keepdims=True)))
        for (rl, p), (e, l) in zip(chains, stats):
            start, _ = window(rl)
            cols = slice(p * LANES, (p + 1) * LANES)
            pv = jnp.dot(e.astype(BF16), v_ref[0, pl.ds(start, win), cols], preferred_element_type=F32)
            o = pv / l
            o_ref[0, rl * GRID_W:(rl + 1) * GRID_W, cols] = jnp.where(lo, o[:GRID_W], o[GRID_W:]).astype(BF16)


def _na_attention(proj3, bias, rows_per_step=16):
    b, s, _ = proj3.shape
    n_rows = s // GRID_W
    tq = rows_per_step * GRID_W
    return pl.pallas_call(
        functools.partial(_na_kernel, rows_per_step=rows_per_step, n_rows=n_rows),
        grid=(b, n_rows // rows_per_step),
        in_specs=[
            pl.BlockSpec((1, tq, CHUNK), lambda bi, i: (bi, i, CH_QA)),
            pl.BlockSpec((1, s, CHUNK), lambda bi, i: (bi, 0, CH_KA)),
            pl.BlockSpec((1, s, CHUNK), lambda bi, i: (bi, 0, CH_VA)),
            pl.BlockSpec((2, NA_ROWS, 2 * GRID_W, NA_ROWS * GRID_W), lambda bi, i: (0, 0, 0, 0)),
        ],
        out_specs=pl.BlockSpec((1, tq, CHUNK), lambda bi, i: (bi, i, 0)),
        out_shape=jax.ShapeDtypeStruct((b, s, CHUNK), BF16),
        compiler_params=_cparams(("arbitrary", "arbitrary")),
        name="na_attention",
    )(proj3, proj3, proj3, bias)


def _na_bias_table(rpb):
    c = np.arange(GRID_W)
    cs = np.clip(c - NA_COLS // 2, 0, GRID_W - NA_COLS)
    kc = np.arange(GRID_W)
    inwin = (kc[None, :] >= cs[:, None]) & (kc[None, :] < cs[:, None] + NA_COLS)
    dc = kc[None, :] - c[:, None] + NA_COLS - 1
    rr = np.arange(NA_ROWS)
    dr = rr[None, :] - rr[:, None] + NA_ROWS - 1
    sel_r = jnp.asarray(dr[:, :, None] == np.arange(2 * NA_ROWS - 1), F32)
    sel_c = jnp.asarray((dc[:, :, None] == np.arange(2 * NA_COLS - 1)) & inwin[:, :, None], F32)
    tab = jnp.einsum("rka,hab,cjb->hrckj", sel_r, rpb, sel_c, precision=lax.Precision.HIGHEST)
    tab = jnp.where(inwin[None, None, :, None, :], tab * LOG2E, NEG)
    tab = tab.reshape(2, 2, NA_ROWS, GRID_W, NA_ROWS * GRID_W)
    return tab.transpose(0, 2, 1, 3, 4).reshape(2, NA_ROWS, 2 * GRID_W, NA_ROWS * GRID_W)


DIL_TQ = 128
DIL_WIN = DIL_TQ + 2 * DIL_HALF


def _dil_kernel(q_ref, k_ref, v_ref, bias_ref, o_ref, lw_ref, *, length, rg, ug, nb):
    u0 = pl.program_id(2) * ug
    lo = _lo_mask(DIL_TQ)
    chains = [(r, ub, p) for r in range(rg) for ub in range(ug) for p in range(2)]

    def window(ub):
        gb = u0 + ub
        ws = pl.multiple_of(jnp.clip(gb * DIL_TQ - DIL_HALF, 0, length - DIL_WIN), DIL_HALF)
        case = jnp.where(gb == 0, 0, jnp.where(gb == nb - 1, 2, 1))
        return ws, case

    scores = []
    for r, ub, p in chains:
        ws, case = window(ub)
        cols = slice(p * LANES, (p + 1) * LANES)
        qs = _stack_head_pair(q_ref[r, ub * DIL_TQ:(ub + 1) * DIL_TQ, cols], lo)
        s = lax.dot_general(qs, k_ref[r, pl.ds(ws, DIL_WIN), cols], NT_DIMS, preferred_element_type=F32)
        scores.append(s + bias_ref[case, p])
    stats = []
    for s in scores:
        m = jnp.max(s, axis=-1, keepdims=True)
        e = jnp.exp2(s - m)
        stats.append((m, e, jnp.sum(e, axis=-1, keepdims=True)))
    for (r, ub, p), (m, e, l) in zip(chains, stats):
        ws, _ = window(ub)
        cols = slice(p * LANES, (p + 1) * LANES)
        pv = jnp.dot(e.astype(BF16), v_ref[r, pl.ds(ws, DIL_WIN), cols], preferred_element_type=F32)
        o = pv / l
        lw = jnp.broadcast_to(m + jnp.log2(l), (2 * DIL_TQ, LANES))
        rows = slice(ub * DIL_TQ, (ub + 1) * DIL_TQ)
        ocols = slice(r * CHUNK + p * LANES, r * CHUNK + (p + 1) * LANES)
        o_ref[0, rows, ocols] = jnp.where(lo, o[:DIL_TQ], o[DIL_TQ:]).astype(BF16)
        lw_ref[0, rows, ocols] = jnp.where(lo, lw[:DIL_TQ], lw[DIL_TQ:])


def _dil_bias_table(dil):
    slopes = np.exp2(-8.0 * np.arange(1, 5) / 4.0)
    qi = np.arange(DIL_TQ)[:, None]
    kj = np.arange(DIL_WIN)[None, :]
    tab = np.zeros((3, 4, DIL_TQ, DIL_WIN), np.float32)
    for case, shift in enumerate((0, -DIL_HALF, -2 * DIL_HALF)):
        rel = kj + shift - qi
        for h in range(4):
            tab[case, h] = np.where(np.abs(rel) <= DIL_HALF, -slopes[h] * dil * np.abs(rel) * LOG2E, NEG)
    return tab.reshape(3, 2, 2 * DIL_TQ, DIL_WIN)


def _dilated_branch(src, q_chunk, rg, ug):
    b, dil, length, _ = src.shape
    nb = length // DIL_TQ
    o, lw = pl.pallas_call(
        functools.partial(_dil_kernel, length=length, rg=rg, ug=ug, nb=nb),
        grid=(b, dil // rg, nb // ug),
        in_specs=[
            pl.BlockSpec((None, rg, ug * DIL_TQ, CHUNK), lambda bi, r, u: (bi, r, u, q_chunk)),
            pl.BlockSpec((None, rg, length, CHUNK), lambda bi, r, u: (bi, r, 0, q_chunk + 1)),
            pl.BlockSpec((None, rg, length, CHUNK), lambda bi, r, u: (bi, r, 0, q_chunk + 2)),
            pl.BlockSpec((3, 2, 2 * DIL_TQ, DIL_WIN), lambda bi, r, u: (0, 0, 0, 0)),
        ],
        out_specs=[
            pl.BlockSpec((1, ug * DIL_TQ, rg * CHUNK), lambda bi, r, u: (bi, u, r)),
            pl.BlockSpec((1, ug * DIL_TQ, rg * CHUNK), lambda bi, r, u: (bi, u, r)),
        ],
        out_shape=[
            jax.ShapeDtypeStruct((b, length, dil * CHUNK), BF16),
            jax.ShapeDtypeStruct((b, length, dil * CHUNK), F32),
        ],
        compiler_params=_cparams(("arbitrary", "arbitrary", "arbitrary")),
        name=f"dilated_d{dil}",
    )(src, src, src, jnp.asarray(_dil_bias_table(dil)))
    return o, lw


TN_DIMS = (((0,), (0,)), ((), ()))


def _gqa_kernel(q_ref, k_ref, v_ref, o_ref, m_s, acc_s, *, tq, tk, seq):
    lo = _lo_mask(tq)
    n_kt = seq // tk
    qs = jnp.concatenate([_stack_head_pair(q_ref[0, :, j * LANES:(j + 1) * LANES], lo) for j in range(2)],
                         axis=0)
    m_s[...] = jnp.full(m_s.shape, NEG, F32)
    acc_s[...] = jnp.zeros(acc_s.shape, F32)

    def scores(kt):
        return lax.dot_general(k_ref[0, kt * tk:(kt + 1) * tk, :], qs, NT_DIMS, preferred_element_type=F32)

    def softmax_values(kt, s):
        k0 = kt * tk
        m_old = m_s[...]
        m_new = jnp.maximum(m_old, jnp.max(s, axis=0, keepdims=True))
        alpha = jnp.exp2(m_old - m_new)
        e = jnp.exp2((s - m_new[0:1, :]).astype(BF16))
        m_s[...] = m_new
        pv = lax.dot_general(v_ref[0, k0:k0 + tk, :], e, TN_DIMS, preferred_element_type=F32)
        acc_s[...] = alpha[0:1, :] * acc_s[...] + pv

    s_cur = scores(0)
    for kt in range(n_kt):
        s_next = scores(kt + 1) if kt + 1 < n_kt else None
        softmax_values(kt, s_cur)
        s_cur = s_next

    acc = acc_s[...]
    o_t = acc[:HEAD_DIM] / acc[HEAD_DIM:HEAD_DIM + 1]
    for j in range(2):
        pair_t = jnp.concatenate([o_t[:, (2 * j) * tq:(2 * j + 1) * tq], o_t[:, (2 * j + 1) * tq:(2 * j + 2) * tq]],
                                 axis=0)
        o_ref[0, :, j * LANES:(j + 1) * LANES] = pair_t.T.astype(BF16)


def _gqa_attention(proj3, tq=256, tk=256):
    b, s, _ = proj3.shape
    assert (s // tk) % 2 == 0 and s // tk >= 2
    return pl.pallas_call(
        functools.partial(_gqa_kernel, tq=tq, tk=tk, seq=s),
        grid=(b, 2, s // tq),
        in_specs=[
            pl.BlockSpec((1, tq, CHUNK), lambda bi, g, i: (bi, i, CH_QC + g)),
            pl.BlockSpec((1, s, LANES), lambda bi, g, i: (bi, 0, 2 * CH_KC + g)),
            pl.BlockSpec((1, s, LANES), lambda bi, g, i: (bi, 0, 2 * CH_VC + g)),
        ],
        out_specs=pl.BlockSpec((1, tq, CHUNK), lambda bi, g, i: (bi, i, g)),
        out_shape=jax.ShapeDtypeStruct((b, s, 2 * CHUNK), BF16),
        scratch_shapes=[
            pltpu.VMEM((8, 4 * tq), F32),
            pltpu.VMEM((LANES, 4 * tq), F32),
        ],
        compiler_params=_cparams(("arbitrary", "arbitrary", "arbitrary")),
        name="gqa_attention",
    )(proj3, proj3, proj3)


def _rms(v, g):
    return v * lax.rsqrt(jnp.mean(v * v, axis=-1, keepdims=True) + RMS_EPS) * g


def _outproj_kernel(oa_ref, o1_ref, o4_ref, o16_ref, lw1_ref, lw4_ref, lw16_ref, oc_ref, x_ref,
                    w_ref, mg_ref, g2_ref, wrh_ref, wrl_ref, br_ref, x1_ref, h_ref, route_ref, cnt_ref, il_ref):
    tm = x_ref.shape[0]

    def interleaved(src_ref, dil, slot):
        for r in range(dil):
            for hf in range(CHUNK // LANES):
                piece = src_ref[0, :, r * CHUNK + hf * LANES:r * CHUNK + (hf + 1) * LANES].astype(F32)
                il_ref[slot * 2 + hf, pl.ds(r, tm // dil, stride=dil), :] = piece
        return jnp.concatenate([il_ref[slot * 2], il_ref[slot * 2 + 1]], axis=1)

    lw1 = lw1_ref[...]
    lw4, lw16 = interleaved(lw4_ref, 4, 0), interleaved(lw16_ref, 16, 1)
    o4, o16 = interleaved(o4_ref, 4, 2), interleaved(o16_ref, 16, 3)
    lwm = jnp.maximum(jnp.maximum(lw1, lw4), lw16)
    w1, w4, w16 = jnp.exp2(lw1 - lwm), jnp.exp2(lw4 - lwm), jnp.exp2(lw16 - lwm)
    ob = (w1 * o1_ref[...].astype(F32) + w4 * o4 + w16 * o16) / (w1 + w4 + w16)
    merged = jnp.concatenate([
        _rms(oa_ref[...].astype(F32), mg_ref[:, 0:CHUNK]).astype(BF16),
        _rms(ob, mg_ref[:, CHUNK:2 * CHUNK]).astype(BF16),
        _rms(oc_ref[...].astype(F32), mg_ref[:, 2 * CHUNK:]).astype(BF16),
    ], axis=-1)
    x1 = x_ref[...] + jnp.dot(merged, w_ref[...], preferred_element_type=F32)
    x1_ref[...] = x1
    h = _rms(x1, g2_ref[...])
    h_ref[...] = _pack_bf16_pair(h[:, :D_MODEL // 2], h[:, D_MODEL // 2:])

    hh = h.astype(BF16)
    hl = (h - hh.astype(F32)).astype(BF16)
    logits = (jnp.dot(hh, wrh_ref[...], preferred_element_type=F32)
              + jnp.dot(hl, wrh_ref[...], preferred_element_type=F32)
              + jnp.dot(hh, wrl_ref[...], preferred_element_type=F32)) + br_ref[...]
    lane = lax.broadcasted_iota(jnp.int32, logits.shape, 1)
    lane_f = lane.astype(F32)
    big = float(LANES)
    is_g = lane < N_GROUPS
    gl = jnp.where(is_g, logits, NEG)
    gmax = jnp.max(gl, axis=-1, keepdims=True)
    grp = jnp.min(jnp.where(gl == gmax, lane_f, big), axis=-1, keepdims=True)
    p_grp = 1.0 / jnp.sum(jnp.where(is_g, jnp.exp(logits - gmax), 0.0), axis=-1, keepdims=True)
    e_lo = N_GROUPS + EXPERTS_PER_GROUP * grp
    in_grp = (lane_f >= e_lo) & (lane_f < e_lo + EXPERTS_PER_GROUP)
    el = jnp.where(in_grp, logits, NEG)
    t1 = jnp.max(el, axis=-1, keepdims=True)
    i1 = jnp.min(jnp.where(el == t1, lane_f, big), axis=-1, keepdims=True)
    el2 = jnp.where(lane_f == i1, NEG, el)
    t2 = jnp.max(el2, axis=-1, keepdims=True)
    i2 = jnp.min(jnp.where(el2 == t2, lane_f, big), axis=-1, keepdims=True)
    e21 = jnp.exp(t2 - t1)
    gate1 = p_grp / (1.0 + e21)
    gate2 = p_grp * e21 / (1.0 + e21)
    route = jnp.where(lane == 0, i1 - N_GROUPS,
                      jnp.where(lane == 1, i2 - N_GROUPS,
                                jnp.where(lane == 2, gate1, jnp.where(lane == 3, gate2, 0.0))))
    route_ref[...] = route

    picked = jnp.where((lane_f == i1 - N_GROUPS) | (lane_f == i2 - N_GROUPS), 1.0, 0.0)

    @pl.when(pl.program_id(0) == 0)
    def _():
        cnt_ref[...] = jnp.zeros(cnt_ref.shape, F32)

    cnt_ref[...] = cnt_ref[...] + jnp.sum(picked, axis=0, keepdims=True)


def _outproj(oa, ob_parts, oc, x2, w_out, mix_g, g2, wr_hi, wr_lo, b_r, seq, tm=512):
    n = x2.shape[0]
    nsb = seq // tm
    (o1, lw1), (o4, lw4), (o16, lw16) = ob_parts
    o1, lw1 = o1.reshape(n, CHUNK), lw1.reshape(n, CHUNK)
    row = lambda w: pl.BlockSpec((tm, w), lambda i: (i, 0))
    full = lambda r, w: pl.BlockSpec((r, w), lambda i: (0, 0))
    dil = lambda d: pl.BlockSpec((1, tm // d, d * CHUNK), lambda i: (i // nsb, i % nsb, 0))
    return pl.pallas_call(
        _outproj_kernel,
        grid=(n // tm,),
        in_specs=[row(CHUNK), row(CHUNK), dil(4), dil(16), row(CHUNK), dil(4), dil(16),
                  row(2 * CHUNK), row(D_MODEL),
                  full(D_MODEL, D_MODEL), full(1, D_MODEL), full(1, D_MODEL),
                  full(D_MODEL, LANES), full(D_MODEL, LANES), full(1, LANES)],
        out_specs=[row(D_MODEL), row(D_MODEL // 2), row(LANES), full(8, LANES)],
        out_shape=[jax.ShapeDtypeStruct((n, D_MODEL), F32),
                   jax.ShapeDtypeStruct((n, D_MODEL // 2), PACKED),
                   jax.ShapeDtypeStruct((n, LANES), F32),
                   jax.ShapeDtypeStruct((8, LANES), F32)],
        scratch_shapes=[pltpu.VMEM((8, tm, LANES), F32)],
        compiler_params=_cparams(("arbitrary",)),
        name="outproj_router",
    )(oa, o1, o4, o16, lw1, lw4, lw16, oc, x2, w_out, mix_g, g2, wr_hi, wr_lo, b_r)


META_ROWS = 384


def _rank_kernel(route_ref, cnt_ref, dest_ref, meta_ref, base_s, tri_s, *, tm):
    i = pl.program_id(0)
    lane = lax.broadcasted_iota(jnp.int32, (tm, LANES), 1)
    lane_f = lane.astype(F32)
    route = route_ref[...]
    oh0 = jnp.where(lane_f == route[:, 0:1], 1.0, 0.0)
    oh1 = jnp.where(lane_f == route[:, 1:2], 1.0, 0.0)
    oh = oh0 + oh1

    @pl.when(i == 0)
    def _():
        r = lax.broadcasted_iota(jnp.int32, (tm, tm), 0)
        c = lax.broadcasted_iota(jnp.int32, (tm, tm), 1)
        tri_s[...] = jnp.where(c < r, 1.0, 0.0).astype(BF16)
        lane8 = lax.broadcasted_iota(jnp.int32, cnt_ref.shape, 1)
        padded = jnp.floor((cnt_ref[...] + (MOE_BLOCK - 1)) * (1.0 / MOE_BLOCK)) * MOE_BLOCK
        ends = padded
        for sh in (1, 2, 4, 8, 16):
            ends = ends + jnp.where(lane8 >= sh, pltpu.roll(ends, sh, 1), 0.0)
        base_s[...] = ends - padded
        pos = lax.broadcasted_iota(jnp.int32, (META_ROWS, LANES), 0).astype(F32) * MOE_BLOCK
        lane_m = lax.broadcasted_iota(jnp.int32, (META_ROWS, LANES), 1)
        ends_row = ends[0:1, :]
        hit = jnp.where((lane_m < N_EXPERTS) & (ends_row <= pos), 1.0, 0.0)
        blk_e = jnp.minimum(jnp.sum(hit, axis=-1, keepdims=True), N_EXPERTS - 1.0)
        total = jnp.sum(jnp.where(lane_m[0:1] == N_EXPERTS - 1, ends_row, 0.0), axis=-1, keepdims=True)
        row_m = lax.broadcasted_iota(jnp.int32, (META_ROWS, 1), 0)
        meta = jnp.where(row_m == META_ROWS - 1, total * (1.0 / MOE_BLOCK), blk_e)
        meta_ref[...] = meta.astype(jnp.int32)

    prior = jnp.dot(tri_s[...], oh.astype(BF16), preferred_element_type=F32)
    val = base_s[0:1, :] + prior
    d0 = jnp.sum(oh0 * val, axis=-1, keepdims=True)
    d1 = jnp.sum(oh1 * val, axis=-1, keepdims=True)
    dest_ref[...] = jnp.where(lane == 0, d0, jnp.where(lane == 1, d1, 0.0)).astype(jnp.int32)
    base_s[...] = base_s[...] + jnp.sum(oh, axis=0, keepdims=True)


def _rank(route, counts, tm=512):
    n = route.shape[0]
    return pl.pallas_call(
        functools.partial(_rank_kernel, tm=tm),
        grid=(n // tm,),
        in_specs=[pl.BlockSpec((tm, LANES), lambda i: (i, 0)), pl.BlockSpec((8, LANES), lambda i: (0, 0))],
        out_specs=[pl.BlockSpec((tm, LANES), lambda i: (i, 0)),
                   pl.BlockSpec((META_ROWS, 1), lambda i: (0, 0))],
        out_shape=[jax.ShapeDtypeStruct((n, LANES), jnp.int32),
                   jax.ShapeDtypeStruct((META_ROWS, 1), jnp.int32)],
        scratch_shapes=[pltpu.VMEM((8, LANES), F32), pltpu.VMEM((tm, tm), BF16)],
        compiler_params=_cparams(("arbitrary",)),
        name="moe_rank",
    )(route, counts)


DISPATCH_BUFS = 3


def _dispatch_kernel(dest_ref, h_hbm, xs_in_hbm, xs_hbm, hbuf, in_sem, out_sem, *, n, tc):
    del xs_in_hbm
    i = pl.program_id(0)
    nt = pl.num_programs(0)
    buf = i % DISPATCH_BUFS
    prev = (i + DISPATCH_BUFS - 1) % DISPATCH_BUFS
    t0 = i * tc

    def load(tile, b):
        return pltpu.make_async_copy(h_hbm.at[pl.ds(tile * tc, tc)], hbuf.at[b], in_sem.at[b])

    def drain_rows(b):
        for _ in range(2):
            pltpu.make_async_copy(hbuf.at[b], xs_hbm.at[pl.ds(0, tc)], out_sem.at[b]).wait()

    @pl.when(i == 0)
    def _():
        load(0, 0).start()

        @pl.when(nt > 1)
        def _():
            load(1, 1).start()

    load(i, buf).wait()

    def group(g, carry):
        base = pl.multiple_of(g * SUBLANES, SUBLANES)
        for u in range(SUBLANES):
            src = hbuf.at[buf, pl.ds(base + u, 1)]
            for k in range(2):
                pltpu.make_async_copy(src, xs_hbm.at[pl.ds(dest_ref[k * n + t0 + base + u], 1)],
                                      out_sem.at[buf]).start(priority=k)
        return carry

    lax.fori_loop(0, tc // SUBLANES, group, 0)

    @pl.when(i > 0)
    def _():
        drain_rows(prev)

    @pl.when(i + 2 < nt)
    def _():
        load(i + 2, prev).start()

    @pl.when(i == nt - 1)
    def _():
        drain_rows(buf)


def _dispatch(dest_flat, h, slots, tc=512):
    n, w = h.shape
    cap = slots.shape[0]
    return pl.pallas_call(
        functools.partial(_dispatch_kernel, n=n, tc=tc),
        grid_spec=pltpu.PrefetchScalarGridSpec(
            num_scalar_prefetch=1,
            grid=(n // tc,),
            in_specs=[pl.BlockSpec(memory_space=pl.ANY), pl.BlockSpec(memory_space=pl.ANY)],
            out_specs=pl.BlockSpec(memory_space=pl.ANY),
            scratch_shapes=[pltpu.VMEM((DISPATCH_BUFS, tc, w), h.dtype),
                            pltpu.SemaphoreType.DMA((DISPATCH_BUFS,)), pltpu.SemaphoreType.DMA((DISPATCH_BUFS,))],
        ),
        out_shape=jax.ShapeDtypeStruct((cap, w), h.dtype),
        input_output_aliases={2: 0},
        compiler_params=_cparams(("arbitrary",)),
        name="moe_dispatch",
    )(dest_flat, h, slots)


def _expert_kernel(blk_e_ref, nact_ref, xs_ref, wg_ref, wu_ref, wd_ref, y_ref, wg_s, wu_s, wd_s):
    i = pl.program_id(0)
    active = i < nact_ref[0]
    changed = (i == 0) | (blk_e_ref[i] != blk_e_ref[jnp.maximum(i - 1, 0)])

    @pl.when(active & changed)
    def _():
        wg_s[...] = wg_ref[...].astype(BF16)
        wu_s[...] = wu_ref[...].astype(BF16)
        wd_s[...] = wd_ref[...].astype(BF16)

    @pl.when(active)
    def _():
        hi, lo = _unpack_bf16_pair(xs_ref[...])
        xb = jnp.concatenate([hi.astype(BF16), lo.astype(BF16)], axis=1)
        g = jnp.dot(xb, wg_s[...], preferred_element_type=F32)
        u = jnp.dot(xb, wu_s[...], preferred_element_type=F32)
        a = (g * jax.nn.sigmoid(g) * u).astype(BF16)
        y = jnp.dot(a, wd_s[...], preferred_element_type=F32)
        y_ref[...] = _pack_bf16_pair(y[:, :D_MODEL // 2], y[:, D_MODEL // 2:])

    @pl.when(jnp.logical_not(active))
    def _():
        y_ref[...] = jnp.zeros(y_ref.shape, y_ref.dtype)


def _experts(blk_e, nact, xs, w_gate, w_up, w_down, layer):
    cap = xs.shape[0]
    n_blk = cap // MOE_BLOCK

    def blk(i, be, na):
        return jnp.minimum(i, na[0] - 1)

    return pl.pallas_call(
        _expert_kernel,
        grid_spec=pltpu.PrefetchScalarGridSpec(
            num_scalar_prefetch=2,
            grid=(n_blk,),
            in_specs=[
                pl.BlockSpec((MOE_BLOCK, D_MODEL // 2), lambda i, be, na: (blk(i, be, na), 0)),
                pl.BlockSpec((None, None, D_MODEL, D_EXPERT), lambda i, be, na: (layer, be[blk(i, be, na)], 0, 0)),
                pl.BlockSpec((None, None, D_MODEL, D_EXPERT), lambda i, be, na: (layer, be[blk(i, be, na)], 0, 0)),
                pl.BlockSpec((None, None, D_EXPERT, D_MODEL), lambda i, be, na: (layer, be[blk(i, be, na)], 0, 0)),
            ],
            out_specs=pl.BlockSpec((MOE_BLOCK, D_MODEL // 2), lambda i, be, na: (i, 0)),
            scratch_shapes=[pltpu.VMEM((D_MODEL, D_EXPERT), BF16), pltpu.VMEM((D_MODEL, D_EXPERT), BF16),
                            pltpu.VMEM((D_EXPERT, D_MODEL), BF16)],
        ),
        out_shape=jax.ShapeDtypeStruct((cap, D_MODEL // 2), PACKED),
        compiler_params=_cparams(("arbitrary",)),
        name="moe_experts",
    )(blk_e, nact, xs, w_gate, w_up, w_down)


def _combine_kernel(dest_ref, x1_ref, route_ref, y_hbm, o_ref, yb, sem, *, n, tc):
    i = pl.program_id(0)
    last = pl.num_programs(0) - 1

    def gather(tile, sl):
        t0 = tile * tc

        def body(j, carry):
            for k in range(2):
                pltpu.make_async_copy(y_hbm.at[pl.ds(dest_ref[k * n + t0 + j], 1)], yb.at[sl, k, pl.ds(j, 1)],
                                      sem.at[sl]).start(priority=k)
            return carry

        lax.fori_loop(0, tc, body, 0, unroll=8)

    def drain(sl):
        for k in range(2):
            pltpu.make_async_copy(y_hbm.at[pl.ds(0, tc)], yb.at[sl, k], sem.at[sl]).wait()

    def combine(sl):
        rows = slice(sl * tc, (sl + 1) * tc)
        route = route_ref[rows, :]
        hi0, lo0 = _unpack_bf16_pair(yb[sl, 0])
        hi1, lo1 = _unpack_bf16_pair(yb[sl, 1])
        half = D_MODEL // 2
        o_ref[rows, :half] = x1_ref[rows, :half] + route[:, 2:3] * hi0 + route[:, 3:4] * hi1
        o_ref[rows, half:] = x1_ref[rows, half:] + route[:, 2:3] * lo0 + route[:, 3:4] * lo1

    @pl.when(i == 0)
    def _():
        gather(0, 0)

    drain(0)
    gather(2 * i + 1, 1)
    combine(0)
    drain(1)

    @pl.when(i < last)
    def _():
        gather(2 * i + 2, 0)

    combine(1)


def _combine(dest_flat, x1, route, y, tc=512):
    n = x1.shape[0]
    return pl.pallas_call(
        functools.partial(_combine_kernel, n=n, tc=tc),
        grid_spec=pltpu.PrefetchScalarGridSpec(
            num_scalar_prefetch=1,
            grid=(n // (2 * tc),),
            in_specs=[pl.BlockSpec((2 * tc, D_MODEL), lambda i, d: (i, 0)),
                      pl.BlockSpec((2 * tc, LANES), lambda i, d: (i, 0)),
                      pl.BlockSpec(memory_space=pl.ANY)],
            out_specs=pl.BlockSpec((2 * tc, D_MODEL), lambda i, d: (i, 0)),
            scratch_shapes=[pltpu.VMEM((2, 2, tc, D_MODEL // 2), PACKED), pltpu.SemaphoreType.DMA((2,))],
        ),
        out_shape=jax.ShapeDtypeStruct((n, D_MODEL), F32),
        compiler_params=_cparams(("arbitrary",)),
        name="moe_combine",
    )(dest_flat, x1, route, y)


def _rope_tables(seq):
    n_freq = HEAD_DIM // 4
    t = np.arange(seq)
    inv_freq = jnp.asarray(ROPE_THETA, F32) ** (-jnp.arange(n_freq, dtype=F32) / n_freq)
    pos = jnp.stack([jnp.asarray(t // GRID_W, F32), jnp.asarray(t % GRID_W, F32)], axis=1)
    ang = pos[:, :, None] * inv_freq
    cos = jnp.broadcast_to(jnp.cos(ang)[:, :, None, :], (seq, 2, 2, n_freq)).reshape(seq, HEAD_DIM)
    sin = jnp.broadcast_to(jnp.sin(ang)[:, :, None, :], (seq, 2, 2, n_freq)).reshape(seq, HEAD_DIM)
    first_half = (np.arange(HEAD_DIM) % (2 * n_freq)) < n_freq
    sa = jnp.where(first_half, -sin, 0.0)
    sb = jnp.where(first_half, 0.0, sin)
    rep = CHUNK // HEAD_DIM
    return jnp.tile(cos, (1, rep)), jnp.tile(sa, (1, rep)), jnp.tile(sb, (1, rep))


def _prep_w_in(w):
    base = w[:, :2048]
    kc0, kc1 = w[:, 2048:2112], w[:, 2112:2176]
    vc0, vc1 = w[:, 2176:2240], w[:, 2240:2304]
    z = jnp.zeros_like(vc0)
    return jnp.concatenate([base, kc0, kc0, kc1, kc1, vc0, z, vc1, z], axis=1).astype(BF16)


def _prep_gain(qk_g):
    qs = SCALE * LOG2E
    ones = jnp.ones((CHUNK,), F32)
    t4 = lambda g: jnp.tile(g, CHUNK // HEAD_DIM)
    return jnp.concatenate([
        t4(qk_g[0, 0]) * qs, t4(qk_g[0, 1]), ones,
        t4(qk_g[1, 0]) * qs, t4(qk_g[1, 1]), ones,
        t4(qk_g[2, 0]) * qs, t4(qk_g[2, 0]) * qs, t4(qk_g[2, 1]), ones,
    ])[None, :]


def _split_hi_lo(w):
    hi = w.astype(BF16)
    lo = (w - hi.astype(F32)).astype(BF16)
    return hi, lo


def _mixer_layer(x2, bsz, seq, tables, norm1_g, w_in, qk_g, rpb, mix_g, w_out, norm2_g, w_rg, b_rg, w_re, b_re):
    n, d = x2.shape
    cos_t, sa_t, sb_t, bd = tables
    proj, pb4, pb16 = _inproj(x2, norm1_g[None, :], _prep_w_in(w_in), _prep_gain(qk_g), bd, cos_t, sa_t, sb_t, seq)
    proj3 = proj.reshape(bsz, seq, PROJ_W)
    oa = _na_attention(proj3, _na_bias_table(rpb)).reshape(n, CHUNK)
    ob_parts = [_dilated_branch(proj3[:, None], CH_QB, 1, 8), _dilated_branch(pb4, 0, 1, 8),
                _dilated_branch(pb16, 0, 4, 2)]
    oc = _gqa_attention(proj3).reshape(n, 2 * CHUNK)

    w_r = jnp.zeros((d, LANES), F32)
    w_r = w_r.at[:, :N_GROUPS].set(w_rg).at[:, N_GROUPS:N_GROUPS + N_EXPERTS].set(w_re)
    b_r = jnp.zeros((1, LANES), F32)
    b_r = b_r.at[0, :N_GROUPS].set(b_rg).at[0, N_GROUPS:N_GROUPS + N_EXPERTS].set(b_re)
    wr_hi, wr_lo = _split_hi_lo(w_r)
    return _outproj(oa, ob_parts, oc, x2, w_out.astype(BF16), mix_g[None, :], norm2_g[None, :], wr_hi, wr_lo, b_r,
                    seq)


def _moe_layer(x1, h, route, counts, slots, w_gate, w_up, w_down, layer):
    n = x1.shape[0]
    n_blk = slots.shape[0] // MOE_BLOCK
    dest, meta = _rank(route, counts)
    dest_flat = dest[:, :2].T.reshape(2 * n)
    xs = _dispatch(dest_flat, h, slots)
    y = _experts(meta[:n_blk, 0], meta[META_ROWS - 1:, 0], xs, w_gate, w_up, w_down, layer)
    return _combine(dest_flat, x1, route, y), xs


def _tables(seq):
    hd = np.arange(CHUNK) // HEAD_DIM
    return (*_rope_tables(seq), jnp.asarray((hd[:, None] == hd[None, :]) * (1.0 / HEAD_DIM), BF16))


@jax.jit
def kernel(x, norm1_g, w_in, qk_norm_g, na_rpb, mix_norm_g, w_out, norm2_g, w_router_group, b_router_group,
           w_router_expert, b_router_expert, w_gate, w_up, w_down):
    bsz, seq, d = x.shape
    tables = _tables(seq)
    x2 = x.reshape(bsz * seq, d)
    slots = jnp.zeros((2 * bsz * seq + N_EXPERTS * MOE_BLOCK, d // 2), PACKED)
    for l in range(DEPTH):
        x1, h, route, counts = _mixer_layer(x2, bsz, seq, tables, norm1_g[l], w_in[l], qk_norm_g[l], na_rpb[l],
                                            mix_norm_g[l], w_out[l], norm2_g[l], w_router_group[l],
                                            b_router_group[l], w_router_expert[l], b_router_expert[l])
        x2, slots = _moe_layer(x1, h, route, counts, slots, w_gate, w_up, w_down, l)
    return x2.reshape(bsz, seq, d)
```

```python
import functools
import math

import numpy as np
import jax
import jax.numpy as jnp
from jax import lax
from jax.experimental import pallas as pl
from jax.experimental.pallas import tpu as pltpu

F32 = jnp.float32
BF16 = jnp.bfloat16

D_MODEL = 1024
DEPTH = 4
GRID_W = 64
HEAD_DIM = 64
RMS_EPS = 1e-6
NEG = -1e30
LOG2E = math.log2(math.e)
SCALE = HEAD_DIM ** -0.5

NA_ROWS = 8
NA_COLS = 16
DILATIONS = (1, 4, 16)
DIL_HALF = 64
ROPE_THETA = 10000.0

N_GROUPS = 4
EXPERTS_PER_GROUP = 8
N_EXPERTS = 32
D_EXPERT = 512
MOE_BLOCK = 512

LANES = 128
SUBLANES = 8
CHUNK = 256
CH_QA, CH_KA, CH_VA, CH_QB, CH_KB, CH_VB, CH_QC, CH_KC, CH_VC = 0, 1, 2, 3, 4, 5, 6, 8, 9
N_CHUNKS = 10
PROJ_W = N_CHUNKS * CHUNK
NORM_CHUNKS = (CH_QA, CH_KA, CH_QB, CH_KB, CH_QC, CH_QC + 1, CH_KC)
ROPE_CHUNKS = (CH_QC, CH_QC + 1, CH_KC)

VMEM_LIMIT = 56 * 1024 * 1024

NT_DIMS = (((1,), (1,)), ((), ()))


def _cparams(sem):
    return pltpu.CompilerParams(dimension_semantics=sem, vmem_limit_bytes=VMEM_LIMIT)


ROPE_SHIFT = HEAD_DIM // 4


INPROJ_PARTS = 2


def _inproj_kernel(x_ref, g1_ref, w_ref, gain_ref, bd_ref, cos_ref, sa_ref, sb_ref, o_ref, o4_ref, o16_ref, stage_ref):
    tm = x_ref.shape[0]
    tp = tm // INPROJ_PARTS
    part_rows = [slice(i * tp, (i + 1) * tp) for i in range(INPROJ_PARTS)]
    chunk_cols = [slice(c * CHUNK, (c + 1) * CHUNK) for c in range(N_CHUNKS)]
    n_stage = 3 * CHUNK // LANES

    normed = []
    for rows in part_rows:
        x = x_ref[rows, :]
        ms = jnp.mean(x * x, axis=-1, keepdims=True)
        normed.append((x * lax.rsqrt(ms + RMS_EPS) * g1_ref[...]).astype(BF16))
    projs = [jnp.dot(h, w_ref[...], preferred_element_type=F32) for h in normed]
    for part, (rows, proj) in enumerate(zip(part_rows, projs)):
        meansq = {c: jnp.dot(jnp.square(proj[:, chunk_cols[c]]).astype(BF16), bd_ref[...],
                             preferred_element_type=F32) for c in NORM_CHUNKS}
        for c in range(N_CHUNKS):
            cols = chunk_cols[c]
            p = proj[:, cols]
            if c in NORM_CHUNKS:
                p = p * lax.rsqrt(meansq[c] + RMS_EPS) * gain_ref[:, cols]
            if c in ROPE_CHUNKS:
                p = (p * cos_ref[rows, :] + pltpu.roll(p, CHUNK - ROPE_SHIFT, 1) * sa_ref[rows, :]
                     + pltpu.roll(p, ROPE_SHIFT, 1) * sb_ref[rows, :])
            if c == CH_VC:
                lane = lax.broadcasted_iota(jnp.int32, p.shape, 1)
                p = jnp.where(lane % LANES >= HEAD_DIM, 1.0, p)
            o_ref[rows, cols] = p.astype(BF16)
            if c in (CH_QB, CH_KB, CH_VB):
                for hf in range(CHUNK // LANES):
                    lane0 = ((c - CH_QB) * (CHUNK // LANES) + hf) * LANES
                    slot = part * n_stage + lane0 // LANES
                    stage_ref[slot] = p[:, hf * LANES:(hf + 1) * LANES]
                    for dil, od_ref in ((4, o4_ref), (16, o16_ref)):
                        nr = tp // dil
                        for r in range(dil):
                            picked = stage_ref[slot, pl.ds(r, nr, stride=dil), :]
                            od_ref[0, r, part * nr:(part + 1) * nr, lane0:lane0 + LANES] = picked.astype(BF16)


def _inproj(x2, g1, w, gain, bd, cos_t, sa_t, sb_t, seq, tm=512):
    n = x2.shape[0]
    nsb = seq // tm
    bsz = n // seq
    dil_spec = lambda dil: pl.BlockSpec((1, dil, tm // dil, 3 * CHUNK), lambda i: (i // nsb, 0, i % nsb, 0))
    dil_shape = lambda dil: jax.ShapeDtypeStruct((bsz, dil, seq // dil, 3 * CHUNK), BF16)
    return pl.pallas_call(
        _inproj_kernel,
        grid=(n // tm,),
        in_specs=[
            pl.BlockSpec((tm, D_MODEL), lambda i: (i, 0)),
            pl.BlockSpec((1, D_MODEL), lambda i: (0, 0)),
            pl.BlockSpec((D_MODEL, PROJ_W), lambda i: (0, 0)),
            pl.BlockSpec((1, PROJ_W), lambda i: (0, 0)),
            pl.BlockSpec((CHUNK, CHUNK), lambda i: (0, 0)),
            pl.BlockSpec((tm, CHUNK), lambda i: (i % nsb, 0)),
            pl.BlockSpec((tm, CHUNK), lambda i: (i % nsb, 0)),
            pl.BlockSpec((tm, CHUNK), lambda i: (i % nsb, 0)),
        ],
        out_specs=[pl.BlockSpec((tm, PROJ_W), lambda i: (i, 0)), dil_spec(4), dil_spec(16)],
        out_shape=[jax.ShapeDtypeStruct((n, PROJ_W), BF16), dil_shape(4), dil_shape(16)],
        scratch_shapes=[pltpu.VMEM((INPROJ_PARTS * 3 * CHUNK // LANES, tm // INPROJ_PARTS, LANES), F32)],
        compiler_params=_cparams(("arbitrary",)),
        name="inproj",
    )(x2, g1, w, gain, bd, cos_t, sa_t, sb_t)


def _stack_head_pair(qp, lo):
    zero = jnp.zeros_like(qp)
    return jnp.concatenate([jnp.where(lo, qp, zero), jnp.where(lo, zero, qp)], axis=0)


PACKED = jnp.uint32


def _pack_bf16_pair(a, b):
    ua = lax.bitcast_convert_type(a.astype(BF16).astype(F32), PACKED)
    ub = lax.bitcast_convert_type(b.astype(BF16).astype(F32), PACKED)
    return ua | (ub >> 16)


def _unpack_bf16_pair(u):
    hi = lax.bitcast_convert_type(u & PACKED(0xFFFF0000), F32)
    lo = lax.bitcast_convert_type(u << 16, F32)
    return hi, lo


def _lo_mask(rows):
    return lax.broadcasted_iota(jnp.int32, (rows, LANES), 1) < HEAD_DIM


NA_ROW_GROUP = 4


def _na_kernel(q_ref, k_ref, v_ref, bias_ref, o_ref, *, rows_per_step, n_rows):
    i = pl.program_id(1)
    lo = _lo_mask(GRID_W)
    win = NA_ROWS * GRID_W

    def window(rl):
        r = i * rows_per_step + rl
        r0 = jnp.clip(r - NA_ROWS // 2, 0, n_rows - NA_ROWS)
        return pl.multiple_of(r0 * GRID_W, GRID_W), r - r0

    for g0 in range(0, rows_per_step, NA_ROW_GROUP):
        chains = [(rl, p) for rl in range(g0, g0 + NA_ROW_GROUP) for p in range(2)]
        scores = []
        for rl, p in chains:
            start, rr = window(rl)
            cols = slice(p * LANES, (p + 1) * LANES)
            qs = _stack_head_pair(q_ref[0, rl * GRID_W:(rl + 1) * GRID_W, cols], lo)
            s = lax.dot_general(qs, k_ref[0, pl.ds(start, win), cols], NT_DIMS, preferred_element_type=F32)
            scores.append(s + bias_ref[p, rr])
        stats = []
        for s in scores:
            e = jnp.exp2(s - jnp.max(s, axis=-1, keepdims=True))
            stats.append((e, jnp.sum(e, axis=-1, keepdims=True)))
        for (rl, p), (e, l) in zip(chains, stats):
            start, _ = window(rl)
            cols = slice(p * LANES, (p + 1) * LANES)
            pv = jnp.dot(e.astype(BF16), v_ref[0, pl.ds(start, win), cols], preferred_element_type=F32)
            o = pv / l
            o_ref[0, rl * GRID_W:(rl + 1) * GRID_W, cols] = jnp.where(lo, o[:GRID_W], o[GRID_W:]).astype(BF16)


def _na_attention(proj3, bias, rows_per_step=16):
    b, s, _ = proj3.shape
    n_rows = s // GRID_W
    tq = rows_per_step * GRID_W
    return pl.pallas_call(
        functools.partial(_na_kernel, rows_per_step=rows_per_step, n_rows=n_rows),
        grid=(b, n_rows // rows_per_step),
        in_specs=[
            pl.BlockSpec((1, tq, CHUNK), lambda bi, i: (bi, i, CH_QA)),
            pl.BlockSpec((1, s, CHUNK), lambda bi, i: (bi, 0, CH_KA)),
            pl.BlockSpec((1, s, CHUNK), lambda bi, i: (bi, 0, CH_VA)),
            pl.BlockSpec((2, NA_ROWS, 2 * GRID_W, NA_ROWS * GRID_W), lambda bi, i: (0, 0, 0, 0)),
        ],
        out_specs=pl.BlockSpec((1, tq, CHUNK), lambda bi, i: (bi, i, 0)),
        out_shape=jax.ShapeDtypeStruct((b, s, CHUNK), BF16),
        compiler_params=_cparams(("arbitrary", "arbitrary")),
        name="na_attention",
    )(proj3, proj3, proj3, bias)


def _na_bias_table(rpb):
    c = np.arange(GRID_W)
    cs = np.clip(c - NA_COLS // 2, 0, GRID_W - NA_COLS)
    kc = np.arange(GRID_W)
    inwin = (kc[None, :] >= cs[:, None]) & (kc[None, :] < cs[:, None] + NA_COLS)
    dc = kc[None, :] - c[:, None] + NA_COLS - 1
    rr = np.arange(NA_ROWS)
    dr = rr[None, :] - rr[:, None] + NA_ROWS - 1
    sel_r = jnp.asarray(dr[:, :, None] == np.arange(2 * NA_ROWS - 1), F32)
    sel_c = jnp.asarray((dc[:, :, None] == np.arange(2 * NA_COLS - 1)) & inwin[:, :, None], F32)
    tab = jnp.einsum("rka,hab,cjb->hrckj", sel_r, rpb, sel_c, precision=lax.Precision.HIGHEST)
    tab = jnp.where(inwin[None, None, :, None, :], tab * LOG2E, NEG)
    tab = tab.reshape(2, 2, NA_ROWS, GRID_W, NA_ROWS * GRID_W)
    return tab.transpose(0, 2, 1, 3, 4).reshape(2, NA_ROWS, 2 * GRID_W, NA_ROWS * GRID_W)


DIL_TQ = 128
DIL_WIN = DIL_TQ + 2 * DIL_HALF


def _dil_kernel(q_ref, k_ref, v_ref, bias_ref, o_ref, lw_ref, *, length, rg, ug, nb):
    u0 = pl.program_id(2) * ug
    lo = _lo_mask(DIL_TQ)
    chains = [(r, ub, p) for r in range(rg) for ub in range(ug) for p in range(2)]

    def window(ub):
        gb = u0 + ub
        ws = pl.multiple_of(jnp.clip(gb * DIL_TQ - DIL_HALF, 0, length - DIL_WIN), DIL_HALF)
        case = jnp.where(gb == 0, 0, jnp.where(gb == nb - 1, 2, 1))
        return ws, case

    scores = []
    for r, ub, p in chains:
        ws, case = window(ub)
        cols = slice(p * LANES, (p + 1) * LANES)
        qs = _stack_head_pair(q_ref[r, ub * DIL_TQ:(ub + 1) * DIL_TQ, cols], lo)
        s = lax.dot_general(qs, k_ref[r, pl.ds(ws, DIL_WIN), cols], NT_DIMS, preferred_element_type=F32)
        scores.append(s + bias_ref[case, p])
    stats = []
    for s in scores:
        m = jnp.max(s, axis=-1, keepdims=True)
        e = jnp.exp2(s - m)
        stats.append((m, e, jnp.sum(e, axis=-1, keepdims=True)))
    for (r, ub, p), (m, e, l) in zip(chains, stats):
        ws, _ = window(ub)
        cols = slice(p * LANES, (p + 1) * LANES)
        pv = jnp.dot(e.astype(BF16), v_ref[r, pl.ds(ws, DIL_WIN), cols], preferred_element_type=F32)
        o = pv / l
        lw = jnp.broadcast_to(m + jnp.log2(l), (2 * DIL_TQ, LANES))
        rows = slice(ub * DIL_TQ, (ub + 1) * DIL_TQ)
        ocols = slice(r * CHUNK + p * LANES, r * CHUNK + (p + 1) * LANES)
        o_ref[0, rows, ocols] = jnp.where(lo, o[:DIL_TQ], o[DIL_TQ:]).astype(BF16)
        lw_ref[0, rows, ocols] = jnp.where(lo, lw[:DIL_TQ], lw[DIL_TQ:])


def _dil_bias_table(dil):
    slopes = np.exp2(-8.0 * np.arange(1, 5) / 4.0)
    qi = np.arange(DIL_TQ)[:, None]
    kj = np.arange(DIL_WIN)[None, :]
    tab = np.zeros((3, 4, DIL_TQ, DIL_WIN), np.float32)
    for case, shift in enumerate((0, -DIL_HALF, -2 * DIL_HALF)):
        rel = kj + shift - qi
        for h in range(4):
            tab[case, h] = np.where(np.abs(rel) <= DIL_HALF, -slopes[h] * dil * np.abs(rel) * LOG2E, NEG)
    return tab.reshape(3, 2, 2 * DIL_TQ, DIL_WIN)


def _dilated_branch(src, q_chunk, rg, ug):
    b, dil, length, _ = src.shape
    nb = length // DIL_TQ
    o, lw = pl.pallas_call(
        functools.partial(_dil_kernel, length=length, rg=rg, ug=ug, nb=nb),
        grid=(b, dil // rg, nb // ug),
        in_specs=[
            pl.BlockSpec((None, rg, ug * DIL_TQ, CHUNK), lambda bi, r, u: (bi, r, u, q_chunk)),
            pl.BlockSpec((None, rg, length, CHUNK), lambda bi, r, u: (bi, r, 0, q_chunk + 1)),
            pl.BlockSpec((None, rg, length, CHUNK), lambda bi, r, u: (bi, r, 0, q_chunk + 2)),
            pl.BlockSpec((3, 2, 2 * DIL_TQ, DIL_WIN), lambda bi, r, u: (0, 0, 0, 0)),
        ],
        out_specs=[
            pl.BlockSpec((1, ug * DIL_TQ, rg * CHUNK), lambda bi, r, u: (bi, u, r)),
            pl.BlockSpec((1, ug * DIL_TQ, rg * CHUNK), lambda bi, r, u: (bi, u, r)),
        ],
        out_shape=[
            jax.ShapeDtypeStruct((b, length, dil * CHUNK), BF16),
            jax.ShapeDtypeStruct((b, length, dil * CHUNK), F32),
        ],
        compiler_params=_cparams(("arbitrary", "arbitrary", "arbitrary")),
        name=f"dilated_d{dil}",
    )(src, src, src, jnp.asarray(_dil_bias_table(dil)))
    return o, lw


TN_DIMS = (((0,), (0,)), ((), ()))


def _gqa_kernel(q_ref, k_ref, v_ref, o_ref, m_s, acc_s, *, tq, tk, seq):
    lo = _lo_mask(tq)
    n_kt = seq // tk
    qs = jnp.concatenate([_stack_head_pair(q_ref[0, :, j * LANES:(j + 1) * LANES], lo) for j in range(2)],
                         axis=0)
    m_s[...] = jnp.full(m_s.shape, NEG, F32)
    acc_s[...] = jnp.zeros(acc_s.shape, F32)

    def scores(kt):
        return lax.dot_general(k_ref[0, kt * tk:(kt + 1) * tk, :], qs, NT_DIMS, preferred_element_type=F32)

    def softmax_values(kt, s):
        k0 = kt * tk
        m_old = m_s[...]
        m_new = jnp.maximum(m_old, jnp.max(s, axis=0, keepdims=True))
        alpha = jnp.exp2(m_old - m_new)
        e = jnp.exp2((s - m_new[0:1, :]).astype(BF16))
        m_s[...] = m_new
        pv = lax.dot_general(v_ref[0, k0:k0 + tk, :], e, TN_DIMS, preferred_element_type=F32)
        acc_s[...] = alpha[0:1, :] * acc_s[...] + pv

    s_cur = scores(0)
    for kt in range(n_kt):
        s_next = scores(kt + 1) if kt + 1 < n_kt else None
        softmax_values(kt, s_cur)
        s_cur = s_next

    acc = acc_s[...]
    o_t = acc[:HEAD_DIM] / acc[HEAD_DIM:HEAD_DIM + 1]
    for j in range(2):
        pair_t = jnp.concatenate([o_t[:, (2 * j) * tq:(2 * j + 1) * tq], o_t[:, (2 * j + 1) * tq:(2 * j + 2) * tq]],
                                 axis=0)
        o_ref[0, :, j * LANES:(j + 1) * LANES] = pair_t.T.astype(BF16)


def _gqa_attention(proj3, tq=256, tk=256):
    b, s, _ = proj3.shape
    assert (s // tk) % 2 == 0 and s // tk >= 2
    return pl.pallas_call(
        functools.partial(_gqa_kernel, tq=tq, tk=tk, seq=s),
        grid=(b, 2, s // tq),
        in_specs=[
            pl.BlockSpec((1, tq, CHUNK), lambda bi, g, i: (bi, i, CH_QC + g)),
            pl.BlockSpec((1, s, LANES), lambda bi, g, i: (bi, 0, 2 * CH_KC + g)),
            pl.BlockSpec((1, s, LANES), lambda bi, g, i: (bi, 0, 2 * CH_VC + g)),
        ],
        out_specs=pl.BlockSpec((1, tq, CHUNK), lambda bi, g, i: (bi, i, g)),
        out_shape=jax.ShapeDtypeStruct((b, s, 2 * CHUNK), BF16),
        scratch_shapes=[
            pltpu.VMEM((8, 4 * tq), F32),
            pltpu.VMEM((LANES, 4 * tq), F32),
        ],
        compiler_params=_cparams(("arbitrary", "arbitrary", "arbitrary")),
        name="gqa_attention",
    )(proj3, proj3, proj3)


def _rms(v, g):
    return v * lax.rsqrt(jnp.mean(v * v, axis=-1, keepdims=True) + RMS_EPS) * g


def _outproj_kernel(oa_ref, o1_ref, o4_ref, o16_ref, lw1_ref, lw4_ref, lw16_ref, oc_ref, x_ref,
                    w_ref, mg_ref, g2_ref, wrh_ref, wrl_ref, br_ref, x1_ref, h_ref, route_ref, cnt_ref, il_ref):
    tm = x_ref.shape[0]

    def interleaved(src_ref, dil, slot):
        for r in range(dil):
            for hf in range(CHUNK // LANES):
                piece = src_ref[0, :, r * CHUNK + hf * LANES:r * CHUNK + (hf + 1) * LANES].astype(F32)
                il_ref[slot * 2 + hf, pl.ds(r, tm // dil, stride=dil), :] = piece
        return jnp.concatenate([il_ref[slot * 2], il_ref[slot * 2 + 1]], axis=1)

    lw1 = lw1_ref[...]
    lw4, lw16 = interleaved(lw4_ref, 4, 0), interleaved(lw16_ref, 16, 1)
    o4, o16 = interleaved(o4_ref, 4, 2), interleaved(o16_ref, 16, 3)
    lwm = jnp.maximum(jnp.maximum(lw1, lw4), lw16)
    w1, w4, w16 = jnp.exp2(lw1 - lwm), jnp.exp2(lw4 - lwm), jnp.exp2(lw16 - lwm)
    ob = (w1 * o1_ref[...].astype(F32) + w4 * o4 + w16 * o16) / (w1 + w4 + w16)
    merged = jnp.concatenate([
        _rms(oa_ref[...].astype(F32), mg_ref[:, 0:CHUNK]).astype(BF16),
        _rms(ob, mg_ref[:, CHUNK:2 * CHUNK]).astype(BF16),
        _rms(oc_ref[...].astype(F32), mg_ref[:, 2 * CHUNK:]).astype(BF16),
    ], axis=-1)
    x1 = x_ref[...] + jnp.dot(merged, w_ref[...], preferred_element_type=F32)
    x1_ref[...] = x1
    h = _rms(x1, g2_ref[...])
    h_ref[...] = _pack_bf16_pair(h[:, :D_MODEL // 2], h[:, D_MODEL // 2:])

    hh = h.astype(BF16)
    hl = (h - hh.astype(F32)).astype(BF16)
    logits = (jnp.dot(hh, wrh_ref[...], preferred_element_type=F32)
              + jnp.dot(hl, wrh_ref[...], preferred_element_type=F32)
              + jnp.dot(hh, wrl_ref[...], preferred_element_type=F32)) + br_ref[...]
    lane = lax.broadcasted_iota(jnp.int32, logits.shape, 1)
    lane_f = lane.astype(F32)
    big = float(LANES)
    is_g = lane < N_GROUPS
    gl = jnp.where(is_g, logits, NEG)
    gmax = jnp.max(gl, axis=-1, keepdims=True)
    grp = jnp.min(jnp.where(gl == gmax, lane_f, big), axis=-1, keepdims=True)
    p_grp = 1.0 / jnp.sum(jnp.where(is_g, jnp.exp(logits - gmax), 0.0), axis=-1, keepdims=True)
    e_lo = N_GROUPS + EXPERTS_PER_GROUP * grp
    in_grp = (lane_f >= e_lo) & (lane_f < e_lo + EXPERTS_PER_GROUP)
    el = jnp.where(in_grp, logits, NEG)
    t1 = jnp.max(el, axis=-1, keepdims=True)
    i1 = jnp.min(jnp.where(el == t1, lane_f, big), axis=-1, keepdims=True)
    el2 = jnp.where(lane_f == i1, NEG, el)
    t2 = jnp.max(el2, axis=-1, keepdims=True)
    i2 = jnp.min(jnp.where(el2 == t2, lane_f, big), axis=-1, keepdims=True)
    e21 = jnp.exp(t2 - t1)
    gate1 = p_grp / (1.0 + e21)
    gate2 = p_grp * e21 / (1.0 + e21)
    route = jnp.where(lane == 0, i1 - N_GROUPS,
                      jnp.where(lane == 1, i2 - N_GROUPS,
                                jnp.where(lane == 2, gate1, jnp.where(lane == 3, gate2, 0.0))))
    route_ref[...] = route

    picked = jnp.where((lane_f == i1 - N_GROUPS) | (lane_f == i2 - N_GROUPS), 1.0, 0.0)

    @pl.when(pl.program_id(0) == 0)
    def _():
        cnt_ref[...] = jnp.zeros(cnt_ref.shape, F32)

    cnt_ref[...] = cnt_ref[...] + jnp.sum(picked, axis=0, keepdims=True)


def _outproj(oa, ob_parts, oc, x2, w_out, mix_g, g2, wr_hi, wr_lo, b_r, seq, tm=512):
    n = x2.shape[0]
    nsb = seq // tm
    (o1, lw1), (o4, lw4), (o16, lw16) = ob_parts
    o1, lw1 = o1.reshape(n, CHUNK), lw1.reshape(n, CHUNK)
    row = lambda w: pl.BlockSpec((tm, w), lambda i: (i, 0))
    full = lambda r, w: pl.BlockSpec((r, w), lambda i: (0, 0))
    dil = lambda d: pl.BlockSpec((1, tm // d, d * CHUNK), lambda i: (i // nsb, i % nsb, 0))
    return pl.pallas_call(
        _outproj_kernel,
        grid=(n // tm,),
        in_specs=[row(CHUNK), row(CHUNK), dil(4), dil(16), row(CHUNK), dil(4), dil(16),
                  row(2 * CHUNK), row(D_MODEL),
                  full(D_MODEL, D_MODEL), full(1, D_MODEL), full(1, D_MODEL),
                  full(D_MODEL, LANES), full(D_MODEL, LANES), full(1, LANES)],
        out_specs=[row(D_MODEL), row(D_MODEL // 2), row(LANES), full(SUBLANES, LANES)],
        out_shape=[jax.ShapeDtypeStruct((n, D_MODEL), F32),
                   jax.ShapeDtypeStruct((n, D_MODEL // 2), PACKED),
                   jax.ShapeDtypeStruct((n, LANES), F32),
                   jax.ShapeDtypeStruct((SUBLANES, LANES), F32)],
        scratch_shapes=[pltpu.VMEM((8, tm, LANES), F32)],
        compiler_params=_cparams(("arbitrary",)),
        name="outproj_router",
    )(oa, o1, o4, o16, lw1, lw4, lw16, oc, x2, w_out, mix_g, g2, wr_hi, wr_lo, b_r)


META_ROWS = 384


def _rank_kernel(route_ref, cnt_ref, dest_ref, meta_ref, base_s, tri_s, *, tm):
    i = pl.program_id(0)
    lane = lax.broadcasted_iota(jnp.int32, (tm, LANES), 1)
    lane_f = lane.astype(F32)
    route = route_ref[...]
    oh0 = jnp.where(lane_f == route[:, 0:1], 1.0, 0.0)
    oh1 = jnp.where(lane_f == route[:, 1:2], 1.0, 0.0)
    oh = oh0 + oh1

    @pl.when(i == 0)
    def _():
        r = lax.broadcasted_iota(jnp.int32, (tm, tm), 0)
        c = lax.broadcasted_iota(jnp.int32, (tm, tm), 1)
        tri_s[...] = jnp.where(c < r, 1.0, 0.0).astype(BF16)
        lane8 = lax.broadcasted_iota(jnp.int32, cnt_ref.shape, 1)
        padded = jnp.floor((cnt_ref[...] + (MOE_BLOCK - 1)) * (1.0 / MOE_BLOCK)) * MOE_BLOCK
        ends = padded
        for sh in (1, 2, 4, 8, 16):
            ends = ends + jnp.where(lane8 >= sh, pltpu.roll(ends, sh, 1), 0.0)
        base_s[...] = ends - padded
        pos = lax.broadcasted_iota(jnp.int32, (META_ROWS, LANES), 0).astype(F32) * MOE_BLOCK
        lane_m = lax.broadcasted_iota(jnp.int32, (META_ROWS, LANES), 1)
        ends_row = ends[0:1, :]
        hit = jnp.where((lane_m < N_EXPERTS) & (ends_row <= pos), 1.0, 0.0)
        blk_e = jnp.minimum(jnp.sum(hit, axis=-1, keepdims=True), N_EXPERTS - 1.0)
        total = jnp.sum(jnp.where(lane_m[0:1] == N_EXPERTS - 1, ends_row, 0.0), axis=-1, keepdims=True)
        row_m = lax.broadcasted_iota(jnp.int32, (META_ROWS, 1), 0)
        meta = jnp.where(row_m == META_ROWS - 1, total * (1.0 / MOE_BLOCK), blk_e)
        meta_ref[...] = meta.astype(jnp.int32)

    prior = jnp.dot(tri_s[...], oh.astype(BF16), preferred_element_type=F32)
    val = base_s[0:1, :] + prior
    d0 = jnp.sum(oh0 * val, axis=-1, keepdims=True)
    d1 = jnp.sum(oh1 * val, axis=-1, keepdims=True)
    dest_ref[...] = jnp.where(lane == 0, d0, jnp.where(lane == 1, d1, 0.0)).astype(jnp.int32)
    base_s[...] = base_s[...] + jnp.sum(oh, axis=0, keepdims=True)


def _rank(route, counts, tm=512):
    n = route.shape[0]
    return pl.pallas_call(
        functools.partial(_rank_kernel, tm=tm),
        grid=(n // tm,),
        in_specs=[pl.BlockSpec((tm, LANES), lambda i: (i, 0)), pl.BlockSpec((SUBLANES, LANES), lambda i: (0, 0))],
        out_specs=[pl.BlockSpec((tm, LANES), lambda i: (i, 0)),
                   pl.BlockSpec((META_ROWS, 1), lambda i: (0, 0))],
        out_shape=[jax.ShapeDtypeStruct((n, LANES), jnp.int32),
                   jax.ShapeDtypeStruct((META_ROWS, 1), jnp.int32)],
        scratch_shapes=[pltpu.VMEM((SUBLANES, LANES), F32), pltpu.VMEM((tm, tm), BF16)],
        compiler_params=_cparams(("arbitrary",)),
        name="moe_rank",
    )(route, counts)


DISPATCH_BUFS = 3


def _dispatch_kernel(dest_ref, h_hbm, xs_in_hbm, xs_hbm, hbuf, in_sem, out_sem, *, n, tc):
    del xs_in_hbm
    i = pl.program_id(0)
    nt = pl.num_programs(0)
    buf = i % DISPATCH_BUFS
    prev = (i + DISPATCH_BUFS - 1) % DISPATCH_BUFS
    t0 = i * tc

    def load(tile, b):
        return pltpu.make_async_copy(h_hbm.at[pl.ds(tile * tc, tc)], hbuf.at[b], in_sem.at[b])

    def drain_rows(b):
        for _ in range(2):
            pltpu.make_async_copy(hbuf.at[b], xs_hbm.at[pl.ds(0, tc)], out_sem.at[b]).wait()

    @pl.when(i == 0)
    def _():
        load(0, 0).start()

        @pl.when(nt > 1)
        def _():
            load(1, 1).start()

    load(i, buf).wait()

    def group(g, carry):
        base = pl.multiple_of(g * SUBLANES, SUBLANES)
        for u in range(SUBLANES):
            src = hbuf.at[buf, pl.ds(base + u, 1)]
            for k in range(2):
                pltpu.make_async_copy(src, xs_hbm.at[pl.ds(dest_ref[k * n + t0 + base + u], 1)],
                                      out_sem.at[buf]).start(priority=k)
        return carry

    lax.fori_loop(0, tc // SUBLANES, group, 0)

    @pl.when(i > 0)
    def _():
        drain_rows(prev)

    @pl.when(i + 2 < nt)
    def _():
        load(i + 2, prev).start()

    @pl.when(i == nt - 1)
    def _():
        drain_rows(buf)


def _dispatch(dest_flat, h, slots, tc=512):
    n, w = h.shape
    cap = slots.shape[0]
    return pl.pallas_call(
        functools.partial(_dispatch_kernel, n=n, tc=tc),
        grid_spec=pltpu.PrefetchScalarGridSpec(
            num_scalar_prefetch=1,
            grid=(n // tc,),
            in_specs=[pl.BlockSpec(memory_space=pl.ANY), pl.BlockSpec(memory_space=pl.ANY)],
            out_specs=pl.BlockSpec(memory_space=pl.ANY),
            scratch_shapes=[pltpu.VMEM((DISPATCH_BUFS, tc, w), h.dtype),
                            pltpu.SemaphoreType.DMA((DISPATCH_BUFS,)), pltpu.SemaphoreType.DMA((DISPATCH_BUFS,))],
        ),
        out_shape=jax.ShapeDtypeStruct((cap, w), h.dtype),
        input_output_aliases={2: 0},
        compiler_params=_cparams(("arbitrary",)),
        name="moe_dispatch",
    )(dest_flat, h, slots)


def _expert_kernel(blk_e_ref, nact_ref, xs_ref, wg_ref, wu_ref, wd_ref, y_ref, wg_s, wu_s, wd_s):
    i = pl.program_id(0)
    active = i < nact_ref[0]
    changed = (i == 0) | (blk_e_ref[i] != blk_e_ref[jnp.maximum(i - 1, 0)])

    @pl.when(active & changed)
    def _():
        wg_s[...] = wg_ref[...].astype(BF16)
        wu_s[...] = wu_ref[...].astype(BF16)
        wd_s[...] = wd_ref[...].astype(BF16)

    @pl.when(active)
    def _():
        hi, lo = _unpack_bf16_pair(xs_ref[...])
        xb = jnp.concatenate([hi.astype(BF16), lo.astype(BF16)], axis=1)
        g = jnp.dot(xb, wg_s[...], preferred_element_type=F32)
        u = jnp.dot(xb, wu_s[...], preferred_element_type=F32)
        a = (g * jax.nn.sigmoid(g) * u).astype(BF16)
        y = jnp.dot(a, wd_s[...], preferred_element_type=F32)
        y_ref[...] = _pack_bf16_pair(y[:, :D_MODEL // 2], y[:, D_MODEL // 2:])

    @pl.when(jnp.logical_not(active))
    def _():
        y_ref[...] = jnp.zeros(y_ref.shape, y_ref.dtype)


def _experts(blk_e, nact, xs, w_gate, w_up, w_down, layer):
    cap = xs.shape[0]
    n_blk = cap // MOE_BLOCK

    def blk(i, be, na):
        return jnp.minimum(i, na[0] - 1)

    return pl.pallas_call(
        _expert_kernel,
        grid_spec=pltpu.PrefetchScalarGridSpec(
            num_scalar_prefetch=2,
            grid=(n_blk,),
            in_specs=[
                pl.BlockSpec((MOE_BLOCK, D_MODEL // 2), lambda i, be, na: (blk(i, be, na), 0)),
                pl.BlockSpec((None, None, D_MODEL, D_EXPERT), lambda i, be, na: (layer, be[blk(i, be, na)], 0, 0)),
                pl.BlockSpec((None, None, D_MODEL, D_EXPERT), lambda i, be, na: (layer, be[blk(i, be, na)], 0, 0)),
                pl.BlockSpec((None, None, D_EXPERT, D_MODEL), lambda i, be, na: (layer, be[blk(i, be, na)], 0, 0)),
            ],
            out_specs=pl.BlockSpec((MOE_BLOCK, D_MODEL // 2), lambda i, be, na: (i, 0)),
            scratch_shapes=[pltpu.VMEM((D_MODEL, D_EXPERT), BF16), pltpu.VMEM((D_MODEL, D_EXPERT), BF16),
                            pltpu.VMEM((D_EXPERT, D_MODEL), BF16)],
        ),
        out_shape=jax.ShapeDtypeStruct((cap, D_MODEL // 2), PACKED),
        compiler_params=_cparams(("arbitrary",)),
        name="moe_experts",
    )(blk_e, nact, xs, w_gate, w_up, w_down)


def _combine_kernel(dest_ref, x1_ref, route_ref, y_hbm, o_ref, yb, sem, *, n, tc):
    i = pl.program_id(0)
    last = pl.num_programs(0) - 1

    def gather(tile, sl):
        t0 = tile * tc

        def body(j, carry):
            for k in range(2):
                pltpu.make_async_copy(y_hbm.at[pl.ds(dest_ref[k * n + t0 + j], 1)], yb.at[sl, k, pl.ds(j, 1)],
                                      sem.at[sl]).start(priority=k)
            return carry

        lax.fori_loop(0, tc, body, 0, unroll=8)

    def drain(sl):
        for k in range(2):
            pltpu.make_async_copy(y_hbm.at[pl.ds(0, tc)], yb.at[sl, k], sem.at[sl]).wait()

    def combine(sl):
        rows = slice(sl * tc, (sl + 1) * tc)
        route = route_ref[rows, :]
        hi0, lo0 = _unpack_bf16_pair(yb[sl, 0])
        hi1, lo1 = _unpack_bf16_pair(yb[sl, 1])
        half = D_MODEL // 2
        o_ref[rows, :half] = x1_ref[rows, :half] + route[:, 2:3] * hi0 + route[:, 3:4] * hi1
        o_ref[rows, half:] = x1_ref[rows, half:] + route[:, 2:3] * lo0 + route[:, 3:4] * lo1

    @pl.when(i == 0)
    def _():
        gather(0, 0)

    drain(0)
    gather(2 * i + 1, 1)
    combine(0)
    drain(1)

    @pl.when(i < last)
    def _():
        gather(2 * i + 2, 0)

    combine(1)


def _combine(dest_flat, x1, route, y, tc=512):
    n = x1.shape[0]
    return pl.pallas_call(
        functools.partial(_combine_kernel, n=n, tc=tc),
        grid_spec=pltpu.PrefetchScalarGridSpec(
            num_scalar_prefetch=1,
            grid=(n // (2 * tc),),
            in_specs=[pl.BlockSpec((2 * tc, D_MODEL), lambda i, d: (i, 0)),
                      pl.BlockSpec((2 * tc, LANES), lambda i, d: (i, 0)),
                      pl.BlockSpec(memory_space=pl.ANY)],
            out_specs=pl.BlockSpec((2 * tc, D_MODEL), lambda i, d: (i, 0)),
            scratch_shapes=[pltpu.VMEM((2, 2, tc, D_MODEL // 2), PACKED), pltpu.SemaphoreType.DMA((2,))],
        ),
        out_shape=jax.ShapeDtypeStruct((n, D_MODEL), F32),
        compiler_params=_cparams(("arbitrary",)),
        name="moe_combine",
    )(dest_flat, x1, route, y)


def _rope_tables(seq):
    n_freq = HEAD_DIM // 4
    t = np.arange(seq)
    inv_freq = jnp.asarray(ROPE_THETA, F32) ** (-jnp.arange(n_freq, dtype=F32) / n_freq)
    pos = jnp.stack([jnp.asarray(t // GRID_W, F32), jnp.asarray(t % GRID_W, F32)], axis=1)
    ang = pos[:, :, None] * inv_freq
    cos = jnp.broadcast_to(jnp.cos(ang)[:, :, None, :], (seq, 2, 2, n_freq)).reshape(seq, HEAD_DIM)
    sin = jnp.broadcast_to(jnp.sin(ang)[:, :, None, :], (seq, 2, 2, n_freq)).reshape(seq, HEAD_DIM)
    first_half = (np.arange(HEAD_DIM) % (2 * n_freq)) < n_freq
    sa = jnp.where(first_half, -sin, 0.0)
    sb = jnp.where(first_half, 0.0, sin)
    rep = CHUNK // HEAD_DIM
    return jnp.tile(cos, (1, rep)), jnp.tile(sa, (1, rep)), jnp.tile(sb, (1, rep))


def _prep_w_in(w):
    base = w[:, :2048]
    kc0, kc1 = w[:, 2048:2112], w[:, 2112:2176]
    vc0, vc1 = w[:, 2176:2240], w[:, 2240:2304]
    z = jnp.zeros_like(vc0)
    return jnp.concatenate([base, kc0, kc0, kc1, kc1, vc0, z, vc1, z], axis=1).astype(BF16)


def _prep_gain(qk_g):
    qs = SCALE * LOG2E
    ones = jnp.ones((CHUNK,), F32)
    t4 = lambda g: jnp.tile(g, CHUNK // HEAD_DIM)
    return jnp.concatenate([
        t4(qk_g[0, 0]) * qs, t4(qk_g[0, 1]), ones,
        t4(qk_g[1, 0]) * qs, t4(qk_g[1, 1]), ones,
        t4(qk_g[2, 0]) * qs, t4(qk_g[2, 0]) * qs, t4(qk_g[2, 1]), ones,
    ])[None, :]


def _split_hi_lo(w):
    hi = w.astype(BF16)
    lo = (w - hi.astype(F32)).astype(BF16)
    return hi, lo


def _mixer_layer(x2, bsz, seq, tables, norm1_g, w_in, qk_g, rpb, mix_g, w_out, norm2_g, w_rg, b_rg, w_re, b_re):
    n, d = x2.shape
    cos_t, sa_t, sb_t, bd = tables
    proj, pb4, pb16 = _inproj(x2, norm1_g[None, :], _prep_w_in(w_in), _prep_gain(qk_g), bd, cos_t, sa_t, sb_t, seq)
    proj3 = proj.reshape(bsz, seq, PROJ_W)
    oa = _na_attention(proj3, _na_bias_table(rpb)).reshape(n, CHUNK)
    ob_parts = [_dilated_branch(proj3[:, None], CH_QB, 1, 8), _dilated_branch(pb4, 0, 1, 8),
                _dilated_branch(pb16, 0, 4, 2)]
    oc = _gqa_attention(proj3).reshape(n, 2 * CHUNK)

    w_r = jnp.zeros((d, LANES), F32)
    w_r = w_r.at[:, :N_GROUPS].set(w_rg).at[:, N_GROUPS:N_GROUPS + N_EXPERTS].set(w_re)
    b_r = jnp.zeros((1, LANES), F32)
    b_r = b_r.at[0, :N_GROUPS].set(b_rg).at[0, N_GROUPS:N_GROUPS + N_EXPERTS].set(b_re)
    wr_hi, wr_lo = _split_hi_lo(w_r)
    return _outproj(oa, ob_parts, oc, x2, w_out.astype(BF16), mix_g[None, :], norm2_g[None, :], wr_hi, wr_lo, b_r,
                    seq)


def _moe_layer(x1, h, route, counts, slots, w_gate, w_up, w_down, layer):
    n = x1.shape[0]
    n_blk = slots.shape[0] // MOE_BLOCK
    assert n_blk < META_ROWS and slots.shape[0] <= 2 ** 24
    dest, meta = _rank(route, counts)
    dest_flat = dest[:, :2].T.reshape(2 * n)
    xs = _dispatch(dest_flat, h, slots)
    y = _experts(meta[:n_blk, 0], meta[META_ROWS - 1:, 0], xs, w_gate, w_up, w_down, layer)
    return _combine(dest_flat, x1, route, y), xs


def _tables(seq):
    hd = np.arange(CHUNK) // HEAD_DIM
    return (*_rope_tables(seq), jnp.asarray((hd[:, None] == hd[None, :]) * (1.0 / HEAD_DIM), BF16))


@jax.jit
def kernel(x, norm1_g, w_in, qk_norm_g, na_rpb, mix_norm_g, w_out, norm2_g, w_router_group, b_router_group,
           w_router_expert, b_router_expert, w_gate, w_up, w_down):
    bsz, seq, d = x.shape
    tables = _tables(seq)
    x2 = x.reshape(bsz * seq, d)
    slots = jnp.zeros((2 * bsz * seq + N_EXPERTS * MOE_BLOCK, d // 2), PACKED)
    for l in range(DEPTH):
        x1, h, route, counts = _mixer_layer(x2, bsz, seq, tables, norm1_g[l], w_in[l], qk_norm_g[l], na_rpb[l],
                                            mix_norm_g[l], w_out[l], norm2_g[l], w_router_group[l],
                                            b_router_group[l], w_router_expert[l], b_router_expert[l])
        x2, slots = _moe_layer(x1, h, route, counts, slots, w_gate, w_up, w_down, l)
    return x2.reshape(bsz, seq, d)
```

```python
import functools
import math

import numpy as np
import jax
import jax.numpy as jnp
from jax import lax
from jax.experimental import pallas as pl
from jax.experimental.pallas import tpu as pltpu

F32 = jnp.float32
BF16 = jnp.bfloat16

D_MODEL = 1024
DEPTH = 4
GRID_W = 64
HEAD_DIM = 64
RMS_EPS = 1e-6
NEG = -1e30
LOG2E = math.log2(math.e)
SCALE = HEAD_DIM ** -0.5

NA_ROWS = 8
NA_COLS = 16
DILATIONS = (1, 4, 16)
DIL_HALF = 64
ROPE_THETA = 10000.0

N_GROUPS = 4
EXPERTS_PER_GROUP = 8
N_EXPERTS = 32
D_EXPERT = 512
MOE_BLOCK = 512

LANES = 128
SUBLANES = 8
CHUNK = 256
CH_QA, CH_KA, CH_VA, CH_QB, CH_KB, CH_VB, CH_QC, CH_KC, CH_VC = 0, 1, 2, 3, 4, 5, 6, 8, 9
N_CHUNKS = 10
PROJ_W = N_CHUNKS * CHUNK
NORM_CHUNKS = (CH_QA, CH_KA, CH_QB, CH_KB, CH_QC, CH_QC + 1, CH_KC)
ROPE_CHUNKS = (CH_QC, CH_QC + 1, CH_KC)

VMEM_LIMIT = 56 * 1024 * 1024

NT_DIMS = (((1,), (1,)), ((), ()))


def _cparams(sem):
    return pltpu.CompilerParams(dimension_semantics=sem, vmem_limit_bytes=VMEM_LIMIT)


ROPE_SHIFT = HEAD_DIM // 4


INPROJ_PARTS = 2


def _inproj_kernel(x_ref, g1_ref, w_ref, gain_ref, bd_ref, cos_ref, sa_ref, sb_ref, o_ref, o4_ref, o16_ref, stage_ref):
    tm = x_ref.shape[0]
    tp = tm // INPROJ_PARTS
    part_rows = [slice(i * tp, (i + 1) * tp) for i in range(INPROJ_PARTS)]
    chunk_cols = [slice(c * CHUNK, (c + 1) * CHUNK) for c in range(N_CHUNKS)]
    n_stage = 3 * CHUNK // LANES

    normed = []
    for rows in part_rows:
        x = x_ref[rows, :]
        ms = jnp.mean(x * x, axis=-1, keepdims=True)
        normed.append((x * lax.rsqrt(ms + RMS_EPS) * g1_ref[...]).astype(BF16))
    projs = [jnp.dot(h, w_ref[...], preferred_element_type=F32) for h in normed]
    for part, (rows, proj) in enumerate(zip(part_rows, projs)):
        meansq = {c: jnp.dot(jnp.square(proj[:, chunk_cols[c]]).astype(BF16), bd_ref[...],
                             preferred_element_type=F32) for c in NORM_CHUNKS}
        for c in range(N_CHUNKS):
            cols = chunk_cols[c]
            p = proj[:, cols]
            if c in NORM_CHUNKS:
                p = p * lax.rsqrt(meansq[c] + RMS_EPS) * gain_ref[:, cols]
            if c in ROPE_CHUNKS:
                p = (p * cos_ref[rows, :] + pltpu.roll(p, CHUNK - ROPE_SHIFT, 1) * sa_ref[rows, :]
                     + pltpu.roll(p, ROPE_SHIFT, 1) * sb_ref[rows, :])
            if c == CH_VC:
                lane = lax.broadcasted_iota(jnp.int32, p.shape, 1)
                p = jnp.where(lane % LANES >= HEAD_DIM, 1.0, p)
            o_ref[rows, cols] = p.astype(BF16)
            if c in (CH_QB, CH_KB, CH_VB):
                for hf in range(CHUNK // LANES):
                    lane0 = ((c - CH_QB) * (CHUNK // LANES) + hf) * LANES
                    slot = part * n_stage + lane0 // LANES
                    stage_ref[slot] = p[:, hf * LANES:(hf + 1) * LANES]
                    for dil, od_ref in ((4, o4_ref), (16, o16_ref)):
                        nr = tp // dil
                        for r in range(dil):
                            picked = stage_ref[slot, pl.ds(r, nr, stride=dil), :]
                            od_ref[0, r, part * nr:(part + 1) * nr, lane0:lane0 + LANES] = picked.astype(BF16)


def _inproj(x2, g1, w, gain, bd, cos_t, sa_t, sb_t, seq, tm=512):
    n = x2.shape[0]
    nsb = seq // tm
    bsz = n // seq
    dil_spec = lambda dil: pl.BlockSpec((1, dil, tm // dil, 3 * CHUNK), lambda i: (i // nsb, 0, i % nsb, 0))
    dil_shape = lambda dil: jax.ShapeDtypeStruct((bsz, dil, seq // dil, 3 * CHUNK), BF16)
    return pl.pallas_call(
        _inproj_kernel,
        grid=(n // tm,),
        in_specs=[
            pl.BlockSpec((tm, D_MODEL), lambda i: (i, 0)),
            pl.BlockSpec((1, D_MODEL), lambda i: (0, 0)),
            pl.BlockSpec((D_MODEL, PROJ_W), lambda i: (0, 0)),
            pl.BlockSpec((1, PROJ_W), lambda i: (0, 0)),
            pl.BlockSpec((CHUNK, CHUNK), lambda i: (0, 0)),
            pl.BlockSpec((tm, CHUNK), lambda i: (i % nsb, 0)),
            pl.BlockSpec((tm, CHUNK), lambda i: (i % nsb, 0)),
            pl.BlockSpec((tm, CHUNK), lambda i: (i % nsb, 0)),
        ],
        out_specs=[pl.BlockSpec((tm, PROJ_W), lambda i: (i, 0)), dil_spec(4), dil_spec(16)],
        out_shape=[jax.ShapeDtypeStruct((n, PROJ_W), BF16), dil_shape(4), dil_shape(16)],
        scratch_shapes=[pltpu.VMEM((INPROJ_PARTS * 3 * CHUNK // LANES, tm // INPROJ_PARTS, LANES), F32)],
        compiler_params=_cparams(("arbitrary",)),
        name="inproj",
    )(x2, g1, w, gain, bd, cos_t, sa_t, sb_t)


def _stack_head_pair(qp, lo):
    zero = jnp.zeros_like(qp)
    return jnp.concatenate([jnp.where(lo, qp, zero), jnp.where(lo, zero, qp)], axis=0)


PACKED = jnp.uint32


def _pack_bf16_pair(a, b):
    ua = lax.bitcast_convert_type(a.astype(BF16).astype(F32), PACKED)
    ub = lax.bitcast_convert_type(b.astype(BF16).astype(F32), PACKED)
    return ua | (ub >> 16)


def _unpack_bf16_pair(u):
    hi = lax.bitcast_convert_type(u & PACKED(0xFFFF0000), F32)
    lo = lax.bitcast_convert_type(u << 16, F32)
    return hi, lo


def _lo_mask(rows):
    return lax.broadcasted_iota(jnp.int32, (rows, LANES), 1) < HEAD_DIM


NA_ROW_GROUP = 4


def _na_kernel(q_ref, k_ref, v_ref, bias_ref, o_ref, *, rows_per_step, n_rows):
    i = pl.program_id(1)
    lo = _lo_mask(GRID_W)
    win = NA_ROWS * GRID_W

    def window(rl):
        r = i * rows_per_step + rl
        r0 = jnp.clip(r - NA_ROWS // 2, 0, n_rows - NA_ROWS)
        return pl.multiple_of(r0 * GRID_W, GRID_W), r - r0

    for g0 in range(0, rows_per_step, NA_ROW_GROUP):
        chains = [(rl, p) for rl in range(g0, g0 + NA_ROW_GROUP) for p in range(2)]
        scores = []
        for rl, p in chains:
            start, rr = window(rl)
            cols = slice(p * LANES, (p + 1) * LANES)
            qs = _stack_head_pair(q_ref[0, rl * GRID_W:(rl + 1) * GRID_W, cols], lo)
            s = lax.dot_general(qs, k_ref[0, pl.ds(start, win), cols], NT_DIMS, preferred_element_type=F32)
            scores.append(s + bias_ref[p, rr])
        stats = []
        for s in scores:
            e = jnp.exp2(s - jnp.max(s, axis=-1, keepdims=True))
            stats.append((e, jnp.sum(e, axis=-1, keepdims=True)))
        for (rl, p), (e, l) in zip(chains, stats):
            start, _ = window(rl)
            cols = slice(p * LANES, (p + 1) * LANES)
            pv = jnp.dot(e.astype(BF16), v_ref[0, pl.ds(start, win), cols], preferred_element_type=F32)
            o = pv / l
            o_ref[0, rl * GRID_W:(rl + 1) * GRID_W, cols] = jnp.where(lo, o[:GRID_W], o[GRID_W:]).astype(BF16)


def _na_attention(proj3, bias, rows_per_step=16):
    b, s, _ = proj3.shape
    n_rows = s // GRID_W
    tq = rows_per_step * GRID_W
    return pl.pallas_call(
        functools.partial(_na_kernel, rows_per_step=rows_per_step, n_rows=n_rows),
        grid=(b, n_rows // rows_per_step),
        in_specs=[
            pl.BlockSpec((1, tq, CHUNK), lambda bi, i: (bi, i, CH_QA)),
            pl.BlockSpec((1, s, CHUNK), lambda bi, i: (bi, 0, CH_KA)),
            pl.BlockSpec((1, s, CHUNK), lambda bi, i: (bi, 0, CH_VA)),
            pl.BlockSpec((2, NA_ROWS, 2 * GRID_W, NA_ROWS * GRID_W), lambda bi, i: (0, 0, 0, 0)),
        ],
        out_specs=pl.BlockSpec((1, tq, CHUNK), lambda bi, i: (bi, i, 0)),
        out_shape=jax.ShapeDtypeStruct((b, s, CHUNK), BF16),
        compiler_params=_cparams(("arbitrary", "arbitrary")),
        name="na_attention",
    )(proj3, proj3, proj3, bias)


def _na_bias_table(rpb):
    c = np.arange(GRID_W)
    cs = np.clip(c - NA_COLS // 2, 0, GRID_W - NA_COLS)
    kc = np.arange(GRID_W)
    inwin = (kc[None, :] >= cs[:, None]) & (kc[None, :] < cs[:, None] + NA_COLS)
    dc = kc[None, :] - c[:, None] + NA_COLS - 1
    rr = np.arange(NA_ROWS)
    dr = rr[None, :] - rr[:, None] + NA_ROWS - 1
    sel_r = jnp.asarray(dr[:, :, None] == np.arange(2 * NA_ROWS - 1), F32)
    sel_c = jnp.asarray((dc[:, :, None] == np.arange(2 * NA_COLS - 1)) & inwin[:, :, None], F32)
    tab = jnp.einsum("rka,hab,cjb->hrckj", sel_r, rpb, sel_c, precision=lax.Precision.HIGHEST)
    tab = jnp.where(inwin[None, None, :, None, :], tab * LOG2E, NEG)
    tab = tab.reshape(2, 2, NA_ROWS, GRID_W, NA_ROWS * GRID_W)
    return tab.transpose(0, 2, 1, 3, 4).reshape(2, NA_ROWS, 2 * GRID_W, NA_ROWS * GRID_W)


DIL_TQ = 128
DIL_WIN = DIL_TQ + 2 * DIL_HALF


def _dil_kernel(q_ref, k_ref, v_ref, bias_ref, o_ref, lw_ref, *, length, rg, ug, nb):
    u0 = pl.program_id(2) * ug
    lo = _lo_mask(DIL_TQ)
    chains = [(r, ub, p) for r in range(rg) for ub in range(ug) for p in range(2)]

    def window(ub):
        gb = u0 + ub
        ws = pl.multiple_of(jnp.clip(gb * DIL_TQ - DIL_HALF, 0, length - DIL_WIN), DIL_HALF)
        case = jnp.where(gb == 0, 0, jnp.where(gb == nb - 1, 2, 1))
        return ws, case

    scores = []
    for r, ub, p in chains:
        ws, case = window(ub)
        cols = slice(p * LANES, (p + 1) * LANES)
        qs = _stack_head_pair(q_ref[r, ub * DIL_TQ:(ub + 1) * DIL_TQ, cols], lo)
        s = lax.dot_general(qs, k_ref[r, pl.ds(ws, DIL_WIN), cols], NT_DIMS, preferred_element_type=F32)
        scores.append(s + bias_ref[case, p])
    stats = []
    for s in scores:
        m = jnp.max(s, axis=-1, keepdims=True)
        e = jnp.exp2(s - m)
        stats.append((m, e, jnp.sum(e, axis=-1, keepdims=True)))
    for (r, ub, p), (m, e, l) in zip(chains, stats):
        ws, _ = window(ub)
        cols = slice(p * LANES, (p + 1) * LANES)
        pv = jnp.dot(e.astype(BF16), v_ref[r, pl.ds(ws, DIL_WIN), cols], preferred_element_type=F32)
        o = pv / l
        lw = jnp.broadcast_to(m + jnp.log2(l), (2 * DIL_TQ, LANES))
        rows = slice(ub * DIL_TQ, (ub + 1) * DIL_TQ)
        ocols = slice(r * CHUNK + p * LANES, r * CHUNK + (p + 1) * LANES)
        o_ref[0, rows, ocols] = jnp.where(lo, o[:DIL_TQ], o[DIL_TQ:]).astype(BF16)
        lw_ref[0, rows, ocols] = jnp.where(lo, lw[:DIL_TQ], lw[DIL_TQ:])


def _dil_bias_table(dil):
    slopes = np.exp2(-8.0 * np.arange(1, 5) / 4.0)
    qi = np.arange(DIL_TQ)[:, None]
    kj = np.arange(DIL_WIN)[None, :]
    tab = np.zeros((3, 4, DIL_TQ, DIL_WIN), np.float32)
    for case, shift in enumerate((0, -DIL_HALF, -2 * DIL_HALF)):
        rel = kj + shift - qi
        for h in range(4):
            tab[case, h] = np.where(np.abs(rel) <= DIL_HALF, -slopes[h] * dil * np.abs(rel) * LOG2E, NEG)
    return tab.reshape(3, 2, 2 * DIL_TQ, DIL_WIN)


def _dilated_branch(src, q_chunk, rg, ug):
    b, dil, length, _ = src.shape
    nb = length // DIL_TQ
    o, lw = pl.pallas_call(
        functools.partial(_dil_kernel, length=length, rg=rg, ug=ug, nb=nb),
        grid=(b, dil // rg, nb // ug),
        in_specs=[
            pl.BlockSpec((None, rg, ug * DIL_TQ, CHUNK), lambda bi, r, u: (bi, r, u, q_chunk)),
            pl.BlockSpec((None, rg, length, CHUNK), lambda bi, r, u: (bi, r, 0, q_chunk + 1)),
            pl.BlockSpec((None, rg, length, CHUNK), lambda bi, r, u: (bi, r, 0, q_chunk + 2)),
            pl.BlockSpec((3, 2, 2 * DIL_TQ, DIL_WIN), lambda bi, r, u: (0, 0, 0, 0)),
        ],
        out_specs=[
            pl.BlockSpec((1, ug * DIL_TQ, rg * CHUNK), lambda bi, r, u: (bi, u, r)),
            pl.BlockSpec((1, ug * DIL_TQ, rg * CHUNK), lambda bi, r, u: (bi, u, r)),
        ],
        out_shape=[
            jax.ShapeDtypeStruct((b, length, dil * CHUNK), BF16),
            jax.ShapeDtypeStruct((b, length, dil * CHUNK), F32),
        ],
        compiler_params=_cparams(("arbitrary", "arbitrary", "arbitrary")),
        name=f"dilated_d{dil}",
    )(src, src, src, jnp.asarray(_dil_bias_table(dil)))
    return o, lw


TN_DIMS = (((0,), (0,)), ((), ()))


def _gqa_kernel(q_ref, k_ref, v_ref, o_ref, m_s, acc_s, *, tq, tk, seq):
    lo = _lo_mask(tq)
    n_kt = seq // tk
    qs = jnp.concatenate([_stack_head_pair(q_ref[0, :, j * LANES:(j + 1) * LANES], lo) for j in range(2)],
                         axis=0)
    m_s[...] = jnp.full(m_s.shape, NEG, F32)
    acc_s[...] = jnp.zeros(acc_s.shape, F32)

    def scores(kt):
        return lax.dot_general(k_ref[0, kt * tk:(kt + 1) * tk, :], qs, NT_DIMS, preferred_element_type=F32)

    def softmax_values(kt, s):
        k0 = kt * tk
        m_old = m_s[...]
        m_new = jnp.maximum(m_old, jnp.max(s, axis=0, keepdims=True))
        alpha = jnp.exp2(m_old - m_new)
        e = jnp.exp2((s - m_new[0:1, :]).astype(BF16))
        m_s[...] = m_new
        pv = lax.dot_general(v_ref[0, k0:k0 + tk, :], e, TN_DIMS, preferred_element_type=F32)
        acc_s[...] = alpha[0:1, :] * acc_s[...] + pv

    s_cur = scores(0)
    for kt in range(n_kt):
        s_next = scores(kt + 1) if kt + 1 < n_kt else None
        softmax_values(kt, s_cur)
        s_cur = s_next

    acc = acc_s[...]
    o_t = acc[:HEAD_DIM] / acc[HEAD_DIM:HEAD_DIM + 1]
    for j in range(2):
        pair_t = jnp.concatenate([o_t[:, (2 * j) * tq:(2 * j + 1) * tq], o_t[:, (2 * j + 1) * tq:(2 * j + 2) * tq]],
                                 axis=0)
        o_ref[0, :, j * LANES:(j + 1) * LANES] = pair_t.T.astype(BF16)


def _gqa_attention(proj3, tq=256, tk=256):
    b, s, _ = proj3.shape
    assert (s // tk) % 2 == 0 and s // tk >= 2
    return pl.pallas_call(
        functools.partial(_gqa_kernel, tq=tq, tk=tk, seq=s),
        grid=(b, 2, s // tq),
        in_specs=[
            pl.BlockSpec((1, tq, CHUNK), lambda bi, g, i: (bi, i, CH_QC + g)),
            pl.BlockSpec((1, s, LANES), lambda bi, g, i: (bi, 0, 2 * CH_KC + g)),
            pl.BlockSpec((1, s, LANES), lambda bi, g, i: (bi, 0, 2 * CH_VC + g)),
        ],
        out_specs=pl.BlockSpec((1, tq, CHUNK), lambda bi, g, i: (bi, i, g)),
        out_shape=jax.ShapeDtypeStruct((b, s, 2 * CHUNK), BF16),
        scratch_shapes=[
            pltpu.VMEM((8, 4 * tq), F32),
            pltpu.VMEM((LANES, 4 * tq), F32),
        ],
        compiler_params=_cparams(("arbitrary", "arbitrary", "arbitrary")),
        name="gqa_attention",
    )(proj3, proj3, proj3)


def _rms(v, g):
    return v * lax.rsqrt(jnp.mean(v * v, axis=-1, keepdims=True) + RMS_EPS) * g


def _outproj_kernel(oa_ref, o1_ref, o4_ref, o16_ref, lw1_ref, lw4_ref, lw16_ref, oc_ref, x_ref,
                    w_ref, mg_ref, g2_ref, wrh_ref, wrl_ref, br_ref, x1_ref, h_ref, route_ref, cnt_ref, il_ref):
    tm = x_ref.shape[0]

    def interleaved(src_ref, dil, slot):
        for r in range(dil):
            for hf in range(CHUNK // LANES):
                piece = src_ref[0, :, r * CHUNK + hf * LANES:r * CHUNK + (hf + 1) * LANES].astype(F32)
                il_ref[slot * 2 + hf, pl.ds(r, tm // dil, stride=dil), :] = piece
        return jnp.concatenate([il_ref[slot * 2], il_ref[slot * 2 + 1]], axis=1)

    lw1 = lw1_ref[...]
    lw4, lw16 = interleaved(lw4_ref, 4, 0), interleaved(lw16_ref, 16, 1)
    o4, o16 = interleaved(o4_ref, 4, 2), interleaved(o16_ref, 16, 3)
    lwm = jnp.maximum(jnp.maximum(lw1, lw4), lw16)
    w1, w4, w16 = jnp.exp2(lw1 - lwm), jnp.exp2(lw4 - lwm), jnp.exp2(lw16 - lwm)
    ob = (w1 * o1_ref[...].astype(F32) + w4 * o4 + w16 * o16) / (w1 + w4 + w16)
    merged = jnp.concatenate([
        _rms(oa_ref[...].astype(F32), mg_ref[:, 0:CHUNK]).astype(BF16),
        _rms(ob, mg_ref[:, CHUNK:2 * CHUNK]).astype(BF16),
        _rms(oc_ref[...].astype(F32), mg_ref[:, 2 * CHUNK:]).astype(BF16),
    ], axis=-1)
    x1 = x_ref[...] + jnp.dot(merged, w_ref[...], preferred_element_type=F32)
    x1_ref[...] = x1
    h = _rms(x1, g2_ref[...])
    h_ref[...] = _pack_bf16_pair(h[:, :D_MODEL // 2], h[:, D_MODEL // 2:])

    hh = h.astype(BF16)
    hl = (h - hh.astype(F32)).astype(BF16)
    logits = (jnp.dot(hh, wrh_ref[...], preferred_element_type=F32)
              + jnp.dot(hl, wrh_ref[...], preferred_element_type=F32)
              + jnp.dot(hh, wrl_ref[...], preferred_element_type=F32)) + br_ref[...]
    lane = lax.broadcasted_iota(jnp.int32, logits.shape, 1)
    lane_f = lane.astype(F32)
    big = float(LANES)
    is_g = lane < N_GROUPS
    gl = jnp.where(is_g, logits, NEG)
    gmax = jnp.max(gl, axis=-1, keepdims=True)
    grp = jnp.min(jnp.where(gl == gmax, lane_f, big), axis=-1, keepdims=True)
    p_grp = 1.0 / jnp.sum(jnp.where(is_g, jnp.exp(logits - gmax), 0.0), axis=-1, keepdims=True)
    e_lo = N_GROUPS + EXPERTS_PER_GROUP * grp
    in_grp = (lane_f >= e_lo) & (lane_f < e_lo + EXPERTS_PER_GROUP)
    el = jnp.where(in_grp, logits, NEG)
    t1 = jnp.max(el, axis=-1, keepdims=True)
    i1 = jnp.min(jnp.where(el == t1, lane_f, big), axis=-1, keepdims=True)
    el2 = jnp.where(lane_f == i1, NEG, el)
    t2 = jnp.max(el2, axis=-1, keepdims=True)
    i2 = jnp.min(jnp.where(el2 == t2, lane_f, big), axis=-1, keepdims=True)
    e21 = jnp.exp(t2 - t1)
    gate1 = p_grp / (1.0 + e21)
    gate2 = p_grp * e21 / (1.0 + e21)
    route = jnp.where(lane == 0, i1 - N_GROUPS,
                      jnp.where(lane == 1, i2 - N_GROUPS,
                                jnp.where(lane == 2, gate1, jnp.where(lane == 3, gate2, 0.0))))
    route_ref[...] = route

    picked = jnp.where((lane_f == i1 - N_GROUPS) | (lane_f == i2 - N_GROUPS), 1.0, 0.0)

    @pl.when(pl.program_id(0) == 0)
    def _():
        cnt_ref[...] = jnp.zeros(cnt_ref.shape, F32)

    cnt_ref[...] = cnt_ref[...] + jnp.sum(picked, axis=0, keepdims=True)


def _outproj(oa, ob_parts, oc, x2, w_out, mix_g, g2, wr_hi, wr_lo, b_r, seq, tm=512):
    n = x2.shape[0]
    nsb = seq // tm
    (o1, lw1), (o4, lw4), (o16, lw16) = ob_parts
    o1, lw1 = o1.reshape(n, CHUNK), lw1.reshape(n, CHUNK)
    row = lambda w: pl.BlockSpec((tm, w), lambda i: (i, 0))
    full = lambda r, w: pl.BlockSpec((r, w), lambda i: (0, 0))
    dil = lambda d: pl.BlockSpec((1, tm // d, d * CHUNK), lambda i: (i // nsb, i % nsb, 0))
    return pl.pallas_call(
        _outproj_kernel,
        grid=(n // tm,),
        in_specs=[row(CHUNK), row(CHUNK), dil(4), dil(16), row(CHUNK), dil(4), dil(16),
                  row(2 * CHUNK), row(D_MODEL),
                  full(D_MODEL, D_MODEL), full(1, D_MODEL), full(1, D_MODEL),
                  full(D_MODEL, LANES), full(D_MODEL, LANES), full(1, LANES)],
        out_specs=[row(D_MODEL), row(D_MODEL // 2), row(LANES), full(SUBLANES, LANES)],
        out_shape=[jax.ShapeDtypeStruct((n, D_MODEL), F32),
                   jax.ShapeDtypeStruct((n, D_MODEL // 2), PACKED),
                   jax.ShapeDtypeStruct((n, LANES), F32),
                   jax.ShapeDtypeStruct((SUBLANES, LANES), F32)],
        scratch_shapes=[pltpu.VMEM((8, tm, LANES), F32)],
        compiler_params=_cparams(("arbitrary",)),
        name="outproj_router",
    )(oa, o1, o4, o16, lw1, lw4, lw16, oc, x2, w_out, mix_g, g2, wr_hi, wr_lo, b_r)


META_ROWS = 384


def _rank_kernel(route_ref, cnt_ref, dest_ref, meta_ref, base_s, tri_s, *, tm):
    i = pl.program_id(0)
    lane = lax.broadcasted_iota(jnp.int32, (tm, LANES), 1)
    lane_f = lane.astype(F32)
    route = route_ref[...]
    oh0 = jnp.where(lane_f == route[:, 0:1], 1.0, 0.0)
    oh1 = jnp.where(lane_f == route[:, 1:2], 1.0, 0.0)
    oh = oh0 + oh1

    @pl.when(i == 0)
    def _():
        r = lax.broadcasted_iota(jnp.int32, (tm, tm), 0)
        c = lax.broadcasted_iota(jnp.int32, (tm, tm), 1)
        tri_s[...] = jnp.where(c < r, 1.0, 0.0).astype(BF16)
        lane8 = lax.broadcasted_iota(jnp.int32, cnt_ref.shape, 1)
        padded = jnp.floor((cnt_ref[...] + (MOE_BLOCK - 1)) * (1.0 / MOE_BLOCK)) * MOE_BLOCK
        ends = padded
        for sh in (1, 2, 4, 8, 16):
            ends = ends + jnp.where(lane8 >= sh, pltpu.roll(ends, sh, 1), 0.0)
        base_s[...] = ends - padded
        pos = lax.broadcasted_iota(jnp.int32, (META_ROWS, LANES), 0).astype(F32) * MOE_BLOCK
        lane_m = lax.broadcasted_iota(jnp.int32, (META_ROWS, LANES), 1)
        ends_row = ends[0:1, :]
        hit = jnp.where((lane_m < N_EXPERTS) & (ends_row <= pos), 1.0, 0.0)
        blk_e = jnp.minimum(jnp.sum(hit, axis=-1, keepdims=True), N_EXPERTS - 1.0)
        total = jnp.sum(jnp.where(lane_m[0:1] == N_EXPERTS - 1, ends_row, 0.0), axis=-1, keepdims=True)
        row_m = lax.broadcasted_iota(jnp.int32, (META_ROWS, 1), 0)
        meta = jnp.where(row_m == META_ROWS - 1, total * (1.0 / MOE_BLOCK), blk_e)
        meta_ref[...] = meta.astype(jnp.int32)

    prior = jnp.dot(tri_s[...], oh.astype(BF16), preferred_element_type=F32)
    val = base_s[0:1, :] + prior
    d0 = jnp.sum(oh0 * val, axis=-1, keepdims=True)
    d1 = jnp.sum(oh1 * val, axis=-1, keepdims=True)
    dest_ref[...] = jnp.where(lane == 0, d0, jnp.where(lane == 1, d1, 0.0)).astype(jnp.int32)
    base_s[...] = base_s[...] + jnp.sum(oh, axis=0, keepdims=True)


def _rank(route, counts, tm=512):
    n = route.shape[0]
    return pl.pallas_call(
        functools.partial(_rank_kernel, tm=tm),
        grid=(n // tm,),
        in_specs=[pl.BlockSpec((tm, LANES), lambda i: (i, 0)), pl.BlockSpec((SUBLANES, LANES), lambda i: (0, 0))],
        out_specs=[pl.BlockSpec((tm, LANES), lambda i: (i, 0)),
                   pl.BlockSpec((META_ROWS, 1), lambda i: (0, 0))],
        out_shape=[jax.ShapeDtypeStruct((n, LANES), jnp.int32),
                   jax.ShapeDtypeStruct((META_ROWS, 1), jnp.int32)],
        scratch_shapes=[pltpu.VMEM((SUBLANES, LANES), F32), pltpu.VMEM((tm, tm), BF16)],
        compiler_params=_cparams(("arbitrary",)),
        name="moe_rank",
    )(route, counts)


DISPATCH_BUFS = 3


def _dispatch_kernel(dest_ref, h_hbm, xs_in_hbm, xs_hbm, hbuf, in_sem, out_sem, *, n, tc):
    del xs_in_hbm
    i = pl.program_id(0)
    nt = pl.num_programs(0)
    buf = i % DISPATCH_BUFS
    prev = (i + DISPATCH_BUFS - 1) % DISPATCH_BUFS
    t0 = i * tc

    def load(tile, b):
        return pltpu.make_async_copy(h_hbm.at[pl.ds(tile * tc, tc)], hbuf.at[b], in_sem.at[b])

    def drain_rows(b):
        for _ in range(2):
            pltpu.make_async_copy(hbuf.at[b], xs_hbm.at[pl.ds(0, tc)], out_sem.at[b]).wait()

    @pl.when(i == 0)
    def _():
        load(0, 0).start()

        @pl.when(nt > 1)
        def _():
            load(1, 1).start()

    load(i, buf).wait()

    def group(g, carry):
        base = pl.multiple_of(g * SUBLANES, SUBLANES)
        for u in range(SUBLANES):
            src = hbuf.at[buf, pl.ds(base + u, 1)]
            for k in range(2):
                pltpu.make_async_copy(src, xs_hbm.at[pl.ds(dest_ref[k * n + t0 + base + u], 1)],
                                      out_sem.at[buf]).start(priority=k)
        return carry

    lax.fori_loop(0, tc // SUBLANES, group, 0)

    @pl.when(i > 0)
    def _():
        drain_rows(prev)

    @pl.when(i + 2 < nt)
    def _():
        load(i + 2, prev).start()

    @pl.when(i == nt - 1)
    def _():
        drain_rows(buf)


def _dispatch(dest_flat, h, slots, tc=512):
    n, w = h.shape
    cap = slots.shape[0]
    return pl.pallas_call(
        functools.partial(_dispatch_kernel, n=n, tc=tc),
        grid_spec=pltpu.PrefetchScalarGridSpec(
            num_scalar_prefetch=1,
            grid=(n // tc,),
            in_specs=[pl.BlockSpec(memory_space=pl.ANY), pl.BlockSpec(memory_space=pl.ANY)],
            out_specs=pl.BlockSpec(memory_space=pl.ANY),
            scratch_shapes=[pltpu.VMEM((DISPATCH_BUFS, tc, w), h.dtype),
                            pltpu.SemaphoreType.DMA((DISPATCH_BUFS,)), pltpu.SemaphoreType.DMA((DISPATCH_BUFS,))],
        ),
        out_shape=jax.ShapeDtypeStruct((cap, w), h.dtype),
        input_output_aliases={2: 0},
        compiler_params=_cparams(("arbitrary",)),
        name="moe_dispatch",
    )(dest_flat, h, slots)


def _expert_kernel(blk_e_ref, nact_ref, xs_ref, wg0_ref, wu0_ref, wd0_ref, wgn_ref, wun_ref, wdn_ref, y_ref,
                   wg_s, wu_s, wd_s, cur_ref):
    i = pl.program_id(0)
    last = nact_ref[0] - 1
    active = i <= last
    e_cur = blk_e_ref[jnp.minimum(i, last)]
    e_next = blk_e_ref[jnp.minimum(i + 1, last)]

    @pl.when(i == 0)
    def _():
        wg_s[0] = wg0_ref[...].astype(BF16)
        wu_s[0] = wu0_ref[...].astype(BF16)
        wd_s[0] = wd0_ref[...].astype(BF16)
        cur_ref[0] = 0

    cur = cur_ref[0]

    @pl.when(active)
    def _():
        hi, lo = _unpack_bf16_pair(xs_ref[...])
        xb = jnp.concatenate([hi.astype(BF16), lo.astype(BF16)], axis=1)
        g = jnp.dot(xb, wg_s[cur], preferred_element_type=F32)
        u = jnp.dot(xb, wu_s[cur], preferred_element_type=F32)
        a = (g * jax.nn.sigmoid(g) * u).astype(BF16)
        y = jnp.dot(a, wd_s[cur], preferred_element_type=F32)
        y_ref[...] = _pack_bf16_pair(y[:, :D_MODEL // 2], y[:, D_MODEL // 2:])

    @pl.when(jnp.logical_not(active))
    def _():
        y_ref[...] = jnp.zeros(y_ref.shape, y_ref.dtype)

    @pl.when(e_next != e_cur)
    def _():
        wg_s[1 - cur] = wgn_ref[...].astype(BF16)
        wu_s[1 - cur] = wun_ref[...].astype(BF16)
        wd_s[1 - cur] = wdn_ref[...].astype(BF16)
        cur_ref[0] = 1 - cur


def _experts(blk_e, nact, xs, w_gate, w_up, w_down, layer):
    cap = xs.shape[0]
    n_blk = cap // MOE_BLOCK

    def blk(i, na):
        return jnp.minimum(i, na[0] - 1)

    first = lambda i, be, na: (layer, be[0], 0, 0)
    ahead = lambda i, be, na: (layer, be[blk(i + 1, na)], 0, 0)
    up_shape, down_shape = (None, None, D_MODEL, D_EXPERT), (None, None, D_EXPERT, D_MODEL)
    return pl.pallas_call(
        _expert_kernel,
        grid_spec=pltpu.PrefetchScalarGridSpec(
            num_scalar_prefetch=2,
            grid=(n_blk,),
            in_specs=[
                pl.BlockSpec((MOE_BLOCK, D_MODEL // 2), lambda i, be, na: (blk(i, na), 0)),
                pl.BlockSpec(up_shape, first), pl.BlockSpec(up_shape, first), pl.BlockSpec(down_shape, first),
                pl.BlockSpec(up_shape, ahead), pl.BlockSpec(up_shape, ahead), pl.BlockSpec(down_shape, ahead),
            ],
            out_specs=pl.BlockSpec((MOE_BLOCK, D_MODEL // 2), lambda i, be, na: (i, 0)),
            scratch_shapes=[pltpu.VMEM((2, D_MODEL, D_EXPERT), BF16), pltpu.VMEM((2, D_MODEL, D_EXPERT), BF16),
                            pltpu.VMEM((2, D_EXPERT, D_MODEL), BF16), pltpu.SMEM((1,), jnp.int32)],
        ),
        out_shape=jax.ShapeDtypeStruct((cap, D_MODEL // 2), PACKED),
        compiler_params=_cparams(("arbitrary",)),
        name="moe_experts",
    )(blk_e, nact, xs, w_gate, w_up, w_down, w_gate, w_up, w_down)


def _combine_kernel(dest_ref, x1_ref, route_ref, y_hbm, o_ref, yb, sem, *, n, tc):
    i = pl.program_id(0)
    last = pl.num_programs(0) - 1

    def gather(tile, sl):
        t0 = tile * tc

        def body(j, carry):
            for k in range(2):
                pltpu.make_async_copy(y_hbm.at[pl.ds(dest_ref[k * n + t0 + j], 1)], yb.at[sl, k, pl.ds(j, 1)],
                                      sem.at[sl]).start(priority=k)
            return carry

        lax.fori_loop(0, tc, body, 0, unroll=8)

    def drain(sl):
        for k in range(2):
            pltpu.make_async_copy(y_hbm.at[pl.ds(0, tc)], yb.at[sl, k], sem.at[sl]).wait()

    def combine(sl):
        rows = slice(sl * tc, (sl + 1) * tc)
        route = route_ref[rows, :]
        hi0, lo0 = _unpack_bf16_pair(yb[sl, 0])
        hi1, lo1 = _unpack_bf16_pair(yb[sl, 1])
        half = D_MODEL // 2
        o_ref[rows, :half] = x1_ref[rows, :half] + route[:, 2:3] * hi0 + route[:, 3:4] * hi1
        o_ref[rows, half:] = x1_ref[rows, half:] + route[:, 2:3] * lo0 + route[:, 3:4] * lo1

    @pl.when(i == 0)
    def _():
        gather(0, 0)

    drain(0)
    gather(2 * i + 1, 1)
    combine(0)
    drain(1)

    @pl.when(i < last)
    def _():
        gather(2 * i + 2, 0)

    combine(1)


def _combine(dest_flat, x1, route, y, tc=512):
    n = x1.shape[0]
    return pl.pallas_call(
        functools.partial(_combine_kernel, n=n, tc=tc),
        grid_spec=pltpu.PrefetchScalarGridSpec(
            num_scalar_prefetch=1,
            grid=(n // (2 * tc),),
            in_specs=[pl.BlockSpec((2 * tc, D_MODEL), lambda i, d: (i, 0)),
                      pl.BlockSpec((2 * tc, LANES), lambda i, d: (i, 0)),
                      pl.BlockSpec(memory_space=pl.ANY)],
            out_specs=pl.BlockSpec((2 * tc, D_MODEL), lambda i, d: (i, 0)),
            scratch_shapes=[pltpu.VMEM((2, 2, tc, D_MODEL // 2), PACKED), pltpu.SemaphoreType.DMA((2,))],
        ),
        out_shape=jax.ShapeDtypeStruct((n, D_MODEL), F32),
        compiler_params=_cparams(("arbitrary",)),
        name="moe_combine",
    )(dest_flat, x1, route, y)


def _rope_tables(seq):
    n_freq = HEAD_DIM // 4
    t = np.arange(seq)
    inv_freq = jnp.asarray(ROPE_THETA, F32) ** (-jnp.arange(n_freq, dtype=F32) / n_freq)
    pos = jnp.stack([jnp.asarray(t // GRID_W, F32), jnp.asarray(t % GRID_W, F32)], axis=1)
    ang = pos[:, :, None] * inv_freq
    cos = jnp.broadcast_to(jnp.cos(ang)[:, :, None, :], (seq, 2, 2, n_freq)).reshape(seq, HEAD_DIM)
    sin = jnp.broadcast_to(jnp.sin(ang)[:, :, None, :], (seq, 2, 2, n_freq)).reshape(seq, HEAD_DIM)
    first_half = (np.arange(HEAD_DIM) % (2 * n_freq)) < n_freq
    sa = jnp.where(first_half, -sin, 0.0)
    sb = jnp.where(first_half, 0.0, sin)
    rep = CHUNK // HEAD_DIM
    return jnp.tile(cos, (1, rep)), jnp.tile(sa, (1, rep)), jnp.tile(sb, (1, rep))


def _prep_w_in(w):
    base = w[:, :2048]
    kc0, kc1 = w[:, 2048:2112], w[:, 2112:2176]
    vc0, vc1 = w[:, 2176:2240], w[:, 2240:2304]
    z = jnp.zeros_like(vc0)
    return jnp.concatenate([base, kc0, kc0, kc1, kc1, vc0, z, vc1, z], axis=1).astype(BF16)


def _prep_gain(qk_g):
    qs = SCALE * LOG2E
    ones = jnp.ones((CHUNK,), F32)
    t4 = lambda g: jnp.tile(g, CHUNK // HEAD_DIM)
    return jnp.concatenate([
        t4(qk_g[0, 0]) * qs, t4(qk_g[0, 1]), ones,
        t4(qk_g[1, 0]) * qs, t4(qk_g[1, 1]), ones,
        t4(qk_g[2, 0]) * qs, t4(qk_g[2, 0]) * qs, t4(qk_g[2, 1]), ones,
    ])[None, :]


def _split_hi_lo(w):
    hi = w.astype(BF16)
    lo = (w - hi.astype(F32)).astype(BF16)
    return hi, lo


def _mixer_layer(x2, bsz, seq, tables, norm1_g, w_in, qk_g, rpb, mix_g, w_out, norm2_g, w_rg, b_rg, w_re, b_re):
    n, d = x2.shape
    cos_t, sa_t, sb_t, bd = tables
    proj, pb4, pb16 = _inproj(x2, norm1_g[None, :], _prep_w_in(w_in), _prep_gain(qk_g), bd, cos_t, sa_t, sb_t, seq)
    proj3 = proj.reshape(bsz, seq, PROJ_W)
    oa = _na_attention(proj3, _na_bias_table(rpb)).reshape(n, CHUNK)
    ob_parts = [_dilated_branch(proj3[:, None], CH_QB, 1, 8), _dilated_branch(pb4, 0, 1, 8),
                _dilated_branch(pb16, 0, 4, 2)]
    oc = _gqa_attention(proj3).reshape(n, 2 * CHUNK)

    w_r = jnp.zeros((d, LANES), F32)
    w_r = w_r.at[:, :N_GROUPS].set(w_rg).at[:, N_GROUPS:N_GROUPS + N_EXPERTS].set(w_re)
    b_r = jnp.zeros((1, LANES), F32)
    b_r = b_r.at[0, :N_GROUPS].set(b_rg).at[0, N_GROUPS:N_GROUPS + N_EXPERTS].set(b_re)
    wr_hi, wr_lo = _split_hi_lo(w_r)
    return _outproj(oa, ob_parts, oc, x2, w_out.astype(BF16), mix_g[None, :], norm2_g[None, :], wr_hi, wr_lo, b_r,
                    seq)


def _moe_layer(x1, h, route, counts, slots, w_gate, w_up, w_down, layer):
    n = x1.shape[0]
    n_blk = slots.shape[0] // MOE_BLOCK
    assert n_blk < META_ROWS and slots.shape[0] <= 2 ** 24
    dest, meta = _rank(route, counts)
    dest_flat = dest[:, :2].T.reshape(2 * n)
    xs = _dispatch(dest_flat, h, slots)
    y = _experts(meta[:n_blk, 0], meta[META_ROWS - 1:, 0], xs, w_gate, w_up, w_down, layer)
    return _combine(dest_flat, x1, route, y), xs


def _tables(seq):
    hd = np.arange(CHUNK) // HEAD_DIM
    return (*_rope_tables(seq), jnp.asarray((hd[:, None] == hd[None, :]) * (1.0 / HEAD_DIM), BF16))


@jax.jit
def kernel(x, norm1_g, w_in, qk_norm_g, na_rpb, mix_norm_g, w_out, norm2_g, w_router_group, b_router_group,
           w_router_expert, b_router_expert, w_gate, w_up, w_down):
    bsz, seq, d = x.shape
    tables = _tables(seq)
    x2 = x.reshape(bsz * seq, d)
    slots = jnp.zeros((2 * bsz * seq + N_EXPERTS * MOE_BLOCK, d // 2), PACKED)
    for l in range(DEPTH):
        x1, h, route, counts = _mixer_layer(x2, bsz, seq, tables, norm1_g[l], w_in[l], qk_norm_g[l], na_rpb[l],
                                            mix_norm_g[l], w_out[l], norm2_g[l], w_router_group[l],
                                            b_router_group[l], w_router_expert[l], b_router_expert[l])
        x2, slots = _moe_layer(x1, h, route, counts, slots, w_gate, w_up, w_down, l)
    return x2.reshape(bsz, seq, d)
```

```python
import functools
import math

import numpy as np
import jax
import jax.numpy as jnp
from jax import lax
from jax.experimental import pallas as pl
from jax.experimental.pallas import tpu as pltpu

F32 = jnp.float32
BF16 = jnp.bfloat16

D_MODEL = 1024
DEPTH = 4
GRID_W = 64
HEAD_DIM = 64
RMS_EPS = 1e-6
NEG = -1e30
LOG2E = math.log2(math.e)
SCALE = HEAD_DIM ** -0.5

NA_ROWS = 8
NA_COLS = 16
DILATIONS = (1, 4, 16)
DIL_HALF = 64
ROPE_THETA = 10000.0

N_GROUPS = 4
EXPERTS_PER_GROUP = 8
N_EXPERTS = 32
D_EXPERT = 512
MOE_BLOCK = 512

LANES = 128
SUBLANES = 8
CHUNK = 256
CH_QA, CH_KA, CH_VA, CH_QB, CH_KB, CH_VB, CH_QC, CH_KC, CH_VC = 0, 1, 2, 3, 4, 5, 6, 8, 9
N_CHUNKS = 10
PROJ_W = N_CHUNKS * CHUNK
NORM_CHUNKS = (CH_QA, CH_KA, CH_QB, CH_KB, CH_QC, CH_QC + 1, CH_KC)
ROPE_CHUNKS = (CH_QC, CH_QC + 1, CH_KC)

VMEM_LIMIT = 56 * 1024 * 1024

NT_DIMS = (((1,), (1,)), ((), ()))


def _cparams(sem):
    return pltpu.CompilerParams(dimension_semantics=sem, vmem_limit_bytes=VMEM_LIMIT)


ROPE_SHIFT = HEAD_DIM // 4


INPROJ_PARTS = 2


def _inproj_kernel(x_ref, g1_ref, w_ref, gain_ref, bd_ref, cos_ref, sa_ref, sb_ref, o_ref, o4_ref, o16_ref, stage_ref):
    tm = x_ref.shape[0]
    tp = tm // INPROJ_PARTS
    part_rows = [slice(i * tp, (i + 1) * tp) for i in range(INPROJ_PARTS)]
    chunk_cols = [slice(c * CHUNK, (c + 1) * CHUNK) for c in range(N_CHUNKS)]
    n_stage = 3 * CHUNK // LANES

    normed = []
    for rows in part_rows:
        x = x_ref[rows, :]
        ms = jnp.mean(x * x, axis=-1, keepdims=True)
        normed.append((x * lax.rsqrt(ms + RMS_EPS) * g1_ref[...]).astype(BF16))
    projs = [jnp.dot(h, w_ref[...], preferred_element_type=F32) for h in normed]
    for part, (rows, proj) in enumerate(zip(part_rows, projs)):
        meansq = {c: jnp.dot(jnp.square(proj[:, chunk_cols[c]]).astype(BF16), bd_ref[...],
                             preferred_element_type=F32) for c in NORM_CHUNKS}
        for c in range(N_CHUNKS):
            cols = chunk_cols[c]
            p = proj[:, cols]
            if c in NORM_CHUNKS:
                p = p * lax.rsqrt(meansq[c] + RMS_EPS) * gain_ref[:, cols]
            if c in ROPE_CHUNKS:
                p = (p * cos_ref[rows, :] + pltpu.roll(p, CHUNK - ROPE_SHIFT, 1) * sa_ref[rows, :]
                     + pltpu.roll(p, ROPE_SHIFT, 1) * sb_ref[rows, :])
            if c == CH_VC:
                lane = lax.broadcasted_iota(jnp.int32, p.shape, 1)
                p = jnp.where(lane % LANES >= HEAD_DIM, 1.0, p)
            o_ref[rows, cols] = p.astype(BF16)
            if c in (CH_QB, CH_KB, CH_VB):
                for hf in range(CHUNK // LANES):
                    lane0 = ((c - CH_QB) * (CHUNK // LANES) + hf) * LANES
                    slot = part * n_stage + lane0 // LANES
                    stage_ref[slot] = p[:, hf * LANES:(hf + 1) * LANES]
                    for dil, od_ref in ((4, o4_ref), (16, o16_ref)):
                        nr = tp // dil
                        for r in range(dil):
                            picked = stage_ref[slot, pl.ds(r, nr, stride=dil), :]
                            od_ref[0, r, part * nr:(part + 1) * nr, lane0:lane0 + LANES] = picked.astype(BF16)


def _inproj(x2, g1, w, gain, bd, cos_t, sa_t, sb_t, seq, tm=512):
    n = x2.shape[0]
    nsb = seq // tm
    bsz = n // seq
    dil_spec = lambda dil: pl.BlockSpec((1, dil, tm // dil, 3 * CHUNK), lambda i: (i // nsb, 0, i % nsb, 0))
    dil_shape = lambda dil: jax.ShapeDtypeStruct((bsz, dil, seq // dil, 3 * CHUNK), BF16)
    return pl.pallas_call(
        _inproj_kernel,
        grid=(n // tm,),
        in_specs=[
            pl.BlockSpec((tm, D_MODEL), lambda i: (i, 0)),
            pl.BlockSpec((1, D_MODEL), lambda i: (0, 0)),
            pl.BlockSpec((D_MODEL, PROJ_W), lambda i: (0, 0)),
            pl.BlockSpec((1, PROJ_W), lambda i: (0, 0)),
            pl.BlockSpec((CHUNK, CHUNK), lambda i: (0, 0)),
            pl.BlockSpec((tm, CHUNK), lambda i: (i % nsb, 0)),
            pl.BlockSpec((tm, CHUNK), lambda i: (i % nsb, 0)),
            pl.BlockSpec((tm, CHUNK), lambda i: (i % nsb, 0)),
        ],
        out_specs=[pl.BlockSpec((tm, PROJ_W), lambda i: (i, 0)), dil_spec(4), dil_spec(16)],
        out_shape=[jax.ShapeDtypeStruct((n, PROJ_W), BF16), dil_shape(4), dil_shape(16)],
        scratch_shapes=[pltpu.VMEM((INPROJ_PARTS * 3 * CHUNK // LANES, tm // INPROJ_PARTS, LANES), F32)],
        compiler_params=_cparams(("arbitrary",)),
        name="inproj",
    )(x2, g1, w, gain, bd, cos_t, sa_t, sb_t)


def _stack_head_pair(qp, lo):
    zero = jnp.zeros_like(qp)
    return jnp.concatenate([jnp.where(lo, qp, zero), jnp.where(lo, zero, qp)], axis=0)


PACKED = jnp.uint32


def _pack_bf16_pair(a, b):
    ua = lax.bitcast_convert_type(a.astype(BF16).astype(F32), PACKED)
    ub = lax.bitcast_convert_type(b.astype(BF16).astype(F32), PACKED)
    return ua | (ub >> 16)


def _unpack_bf16_pair(u):
    hi = lax.bitcast_convert_type(u & PACKED(0xFFFF0000), F32)
    lo = lax.bitcast_convert_type(u << 16, F32)
    return hi, lo


def _lo_mask(rows):
    return lax.broadcasted_iota(jnp.int32, (rows, LANES), 1) < HEAD_DIM


NA_ROW_GROUP = 4


def _na_kernel(q_ref, k_ref, v_ref, bias_ref, o_ref, *, rows_per_step, n_rows):
    i = pl.program_id(1)
    lo = _lo_mask(GRID_W)
    win = NA_ROWS * GRID_W

    def window(rl):
        r = i * rows_per_step + rl
        r0 = jnp.clip(r - NA_ROWS // 2, 0, n_rows - NA_ROWS)
        return pl.multiple_of(r0 * GRID_W, GRID_W), r - r0

    for g0 in range(0, rows_per_step, NA_ROW_GROUP):
        chains = [(rl, p) for rl in range(g0, g0 + NA_ROW_GROUP) for p in range(2)]
        scores = []
        for rl, p in chains:
            start, rr = window(rl)
            cols = slice(p * LANES, (p + 1) * LANES)
            qs = _stack_head_pair(q_ref[0, rl * GRID_W:(rl + 1) * GRID_W, cols], lo)
            s = lax.dot_general(qs, k_ref[0, pl.ds(start, win), cols], NT_DIMS, preferred_element_type=F32)
            scores.append(s + bias_ref[p, rr])
        stats = []
        for s in scores:
            e = jnp.exp2(s - jnp.max(s, axis=-1, keepdims=True))
            stats.append((e, jnp.sum(e, axis=-1, keepdims=True)))
        for (rl, p), (e, l) in zip(chains, stats):
            start, _ = window(rl)
            cols = slice(p * LANES, (p + 1) * LANES)
            pv = jnp.dot(e.astype(BF16), v_ref[0, pl.ds(start, win), cols], preferred_element_type=F32)
            o = pv / l
            o_ref[0, rl * GRID_W:(rl + 1) * GRID_W, cols] = jnp.where(lo, o[:GRID_W], o[GRID_W:]).astype(BF16)


def _na_attention(proj3, bias, rows_per_step=16):
    b, s, _ = proj3.shape
    n_rows = s // GRID_W
    tq = rows_per_step * GRID_W
    return pl.pallas_call(
        functools.partial(_na_kernel, rows_per_step=rows_per_step, n_rows=n_rows),
        grid=(b, n_rows // rows_per_step),
        in_specs=[
            pl.BlockSpec((1, tq, CHUNK), lambda bi, i: (bi, i, CH_QA)),
            pl.BlockSpec((1, s, CHUNK), lambda bi, i: (bi, 0, CH_KA)),
            pl.BlockSpec((1, s, CHUNK), lambda bi, i: (bi, 0, CH_VA)),
            pl.BlockSpec((2, NA_ROWS, 2 * GRID_W, NA_ROWS * GRID_W), lambda bi, i: (0, 0, 0, 0)),
        ],
        out_specs=pl.BlockSpec((1, tq, CHUNK), lambda bi, i: (bi, i, 0)),
        out_shape=jax.ShapeDtypeStruct((b, s, CHUNK), BF16),
        compiler_params=_cparams(("arbitrary", "arbitrary")),
        name="na_attention",
    )(proj3, proj3, proj3, bias)


def _na_bias_table(rpb):
    c = np.arange(GRID_W)
    cs = np.clip(c - NA_COLS // 2, 0, GRID_W - NA_COLS)
    kc = np.arange(GRID_W)
    inwin = (kc[None, :] >= cs[:, None]) & (kc[None, :] < cs[:, None] + NA_COLS)
    dc = kc[None, :] - c[:, None] + NA_COLS - 1
    rr = np.arange(NA_ROWS)
    dr = rr[None, :] - rr[:, None] + NA_ROWS - 1
    sel_r = jnp.asarray(dr[:, :, None] == np.arange(2 * NA_ROWS - 1), F32)
    sel_c = jnp.asarray((dc[:, :, None] == np.arange(2 * NA_COLS - 1)) & inwin[:, :, None], F32)
    tab = jnp.einsum("rka,hab,cjb->hrckj", sel_r, rpb, sel_c, precision=lax.Precision.HIGHEST)
    tab = jnp.where(inwin[None, None, :, None, :], tab * LOG2E, NEG)
    tab = tab.reshape(2, 2, NA_ROWS, GRID_W, NA_ROWS * GRID_W)
    return tab.transpose(0, 2, 1, 3, 4).reshape(2, NA_ROWS, 2 * GRID_W, NA_ROWS * GRID_W)


DIL_TQ = 128
DIL_WIN = DIL_TQ + 2 * DIL_HALF


def _dil_kernel(q_ref, k_ref, v_ref, bias_ref, o_ref, lw_ref, *, length, rg, ug, nb):
    u0 = pl.program_id(2) * ug
    lo = _lo_mask(DIL_TQ)
    chains = [(r, ub, p) for r in range(rg) for ub in range(ug) for p in range(2)]

    def window(ub):
        gb = u0 + ub
        ws = pl.multiple_of(jnp.clip(gb * DIL_TQ - DIL_HALF, 0, length - DIL_WIN), DIL_HALF)
        case = jnp.where(gb == 0, 0, jnp.where(gb == nb - 1, 2, 1))
        return ws, case

    scores = []
    for r, ub, p in chains:
        ws, case = window(ub)
        cols = slice(p * LANES, (p + 1) * LANES)
        qs = _stack_head_pair(q_ref[r, ub * DIL_TQ:(ub + 1) * DIL_TQ, cols], lo)
        s = lax.dot_general(qs, k_ref[r, pl.ds(ws, DIL_WIN), cols], NT_DIMS, preferred_element_type=F32)
        scores.append(s + bias_ref[case, p])
    stats = []
    for s in scores:
        m = jnp.max(s, axis=-1, keepdims=True)
        e = jnp.exp2(s - m)
        stats.append((m, e, jnp.sum(e, axis=-1, keepdims=True)))
    for (r, ub, p), (m, e, l) in zip(chains, stats):
        ws, _ = window(ub)
        cols = slice(p * LANES, (p + 1) * LANES)
        pv = jnp.dot(e.astype(BF16), v_ref[r, pl.ds(ws, DIL_WIN), cols], preferred_element_type=F32)
        o = pv / l
        lw = jnp.broadcast_to(m + jnp.log2(l), (2 * DIL_TQ, LANES))
        rows = slice(ub * DIL_TQ, (ub + 1) * DIL_TQ)
        ocols = slice(r * CHUNK + p * LANES, r * CHUNK + (p + 1) * LANES)
        o_ref[0, rows, ocols] = jnp.where(lo, o[:DIL_TQ], o[DIL_TQ:]).astype(BF16)
        lw_ref[0, rows, ocols] = jnp.where(lo, lw[:DIL_TQ], lw[DIL_TQ:])


def _dil_bias_table(dil):
    slopes = np.exp2(-8.0 * np.arange(1, 5) / 4.0)
    qi = np.arange(DIL_TQ)[:, None]
    kj = np.arange(DIL_WIN)[None, :]
    tab = np.zeros((3, 4, DIL_TQ, DIL_WIN), np.float32)
    for case, shift in enumerate((0, -DIL_HALF, -2 * DIL_HALF)):
        rel = kj + shift - qi
        for h in range(4):
            tab[case, h] = np.where(np.abs(rel) <= DIL_HALF, -slopes[h] * dil * np.abs(rel) * LOG2E, NEG)
    return tab.reshape(3, 2, 2 * DIL_TQ, DIL_WIN)


def _dilated_branch(src, q_chunk, rg, ug):
    b, dil, length, _ = src.shape
    nb = length // DIL_TQ
    o, lw = pl.pallas_call(
        functools.partial(_dil_kernel, length=length, rg=rg, ug=ug, nb=nb),
        grid=(b, dil // rg, nb // ug),
        in_specs=[
            pl.BlockSpec((None, rg, ug * DIL_TQ, CHUNK), lambda bi, r, u: (bi, r, u, q_chunk)),
            pl.BlockSpec((None, rg, length, CHUNK), lambda bi, r, u: (bi, r, 0, q_chunk + 1)),
            pl.BlockSpec((None, rg, length, CHUNK), lambda bi, r, u: (bi, r, 0, q_chunk + 2)),
            pl.BlockSpec((3, 2, 2 * DIL_TQ, DIL_WIN), lambda bi, r, u: (0, 0, 0, 0)),
        ],
        out_specs=[
            pl.BlockSpec((1, ug * DIL_TQ, rg * CHUNK), lambda bi, r, u: (bi, u, r)),
            pl.BlockSpec((1, ug * DIL_TQ, rg * CHUNK), lambda bi, r, u: (bi, u, r)),
        ],
        out_shape=[
            jax.ShapeDtypeStruct((b, length, dil * CHUNK), BF16),
            jax.ShapeDtypeStruct((b, length, dil * CHUNK), F32),
        ],
        compiler_params=_cparams(("arbitrary", "arbitrary", "arbitrary")),
        name=f"dilated_d{dil}",
    )(src, src, src, jnp.asarray(_dil_bias_table(dil)))
    return o, lw


TN_DIMS = (((0,), (0,)), ((), ()))


def _gqa_kernel(q_ref, k_ref, v_ref, o_ref, m_s, acc_s, *, tq, tk, seq):
    lo = _lo_mask(tq)
    n_kt = seq // tk
    qs = jnp.concatenate([_stack_head_pair(q_ref[0, :, j * LANES:(j + 1) * LANES], lo) for j in range(2)],
                         axis=0)
    m_s[...] = jnp.full(m_s.shape, NEG, F32)
    acc_s[...] = jnp.zeros(acc_s.shape, F32)

    def scores(kt):
        return lax.dot_general(k_ref[0, kt * tk:(kt + 1) * tk, :], qs, NT_DIMS, preferred_element_type=F32)

    def softmax_values(kt, s):
        k0 = kt * tk
        m_old = m_s[...]
        m_new = jnp.maximum(m_old, jnp.max(s, axis=0, keepdims=True))
        alpha = jnp.exp2(m_old - m_new)
        e = jnp.exp2((s - m_new[0:1, :]).astype(BF16))
        m_s[...] = m_new
        pv = lax.dot_general(v_ref[0, k0:k0 + tk, :], e, TN_DIMS, preferred_element_type=F32)
        acc_s[...] = alpha[0:1, :] * acc_s[...] + pv

    s_cur = scores(0)
    for kt in range(n_kt):
        s_next = scores(kt + 1) if kt + 1 < n_kt else None
        softmax_values(kt, s_cur)
        s_cur = s_next

    acc = acc_s[...]
    o_t = acc[:HEAD_DIM] / acc[HEAD_DIM:HEAD_DIM + 1]
    for j in range(2):
        pair_t = jnp.concatenate([o_t[:, (2 * j) * tq:(2 * j + 1) * tq], o_t[:, (2 * j + 1) * tq:(2 * j + 2) * tq]],
                                 axis=0)
        o_ref[0, :, j * LANES:(j + 1) * LANES] = pair_t.T.astype(BF16)


def _gqa_attention(proj3, tq=256, tk=256):
    b, s, _ = proj3.shape
    assert (s // tk) % 2 == 0 and s // tk >= 2
    return pl.pallas_call(
        functools.partial(_gqa_kernel, tq=tq, tk=tk, seq=s),
        grid=(b, 2, s // tq),
        in_specs=[
            pl.BlockSpec((1, tq, CHUNK), lambda bi, g, i: (bi, i, CH_QC + g)),
            pl.BlockSpec((1, s, LANES), lambda bi, g, i: (bi, 0, 2 * CH_KC + g)),
            pl.BlockSpec((1, s, LANES), lambda bi, g, i: (bi, 0, 2 * CH_VC + g)),
        ],
        out_specs=pl.BlockSpec((1, tq, CHUNK), lambda bi, g, i: (bi, i, g)),
        out_shape=jax.ShapeDtypeStruct((b, s, 2 * CHUNK), BF16),
        scratch_shapes=[
            pltpu.VMEM((8, 4 * tq), F32),
            pltpu.VMEM((LANES, 4 * tq), F32),
        ],
        compiler_params=_cparams(("arbitrary", "arbitrary", "arbitrary")),
        name="gqa_attention",
    )(proj3, proj3, proj3)


def _rms(v, g):
    return v * lax.rsqrt(jnp.mean(v * v, axis=-1, keepdims=True) + RMS_EPS) * g


OUTPROJ_PARTS = 2


def _outproj_kernel(oa_ref, o1_ref, o4_ref, o16_ref, lw1_ref, lw4_ref, lw16_ref, oc_ref, x_ref,
                    w_ref, mg_ref, g2_ref, wrh_ref, wrl_ref, br_ref, x1_ref, h_ref, route_ref, cnt_ref, il_ref):
    tm = x_ref.shape[0]
    tp = tm // OUTPROJ_PARTS

    def interleaved(src_ref, dil, slot, part):
        nr = tp // dil
        for r in range(dil):
            for hf in range(CHUNK // LANES):
                piece = src_ref[0, part * nr:(part + 1) * nr,
                                r * CHUNK + hf * LANES:r * CHUNK + (hf + 1) * LANES].astype(F32)
                il_ref[slot * 2 + hf, pl.ds(part * tp + r, nr, stride=dil), :] = piece
        rows = slice(part * tp, (part + 1) * tp)
        return jnp.concatenate([il_ref[slot * 2, rows, :], il_ref[slot * 2 + 1, rows, :]], axis=1)

    def mixer_outputs(part):
        rows = slice(part * tp, (part + 1) * tp)
        lw1 = lw1_ref[rows, :]
        lw4, lw16 = interleaved(lw4_ref, 4, 0, part), interleaved(lw16_ref, 16, 1, part)
        o4, o16 = interleaved(o4_ref, 4, 2, part), interleaved(o16_ref, 16, 3, part)
        lwm = jnp.maximum(jnp.maximum(lw1, lw4), lw16)
        w1, w4, w16 = jnp.exp2(lw1 - lwm), jnp.exp2(lw4 - lwm), jnp.exp2(lw16 - lwm)
        ob = (w1 * o1_ref[rows, :].astype(F32) + w4 * o4 + w16 * o16) / (w1 + w4 + w16)
        return jnp.concatenate([
            _rms(oa_ref[rows, :].astype(F32), mg_ref[:, 0:CHUNK]).astype(BF16),
            _rms(ob, mg_ref[:, CHUNK:2 * CHUNK]).astype(BF16),
            _rms(oc_ref[rows, :].astype(F32), mg_ref[:, 2 * CHUNK:]).astype(BF16),
        ], axis=-1)

    def residual_norm(part, proj):
        rows = slice(part * tp, (part + 1) * tp)
        x1 = x_ref[rows, :] + proj
        x1_ref[rows, :] = x1
        h = _rms(x1, g2_ref[...])
        h_ref[rows, :] = _pack_bf16_pair(h[:, :D_MODEL // 2], h[:, D_MODEL // 2:])
        hh = h.astype(BF16)
        return hh, (h - hh.astype(F32)).astype(BF16)

    def route(part, logits):
        rows = slice(part * tp, (part + 1) * tp)
        lane = lax.broadcasted_iota(jnp.int32, logits.shape, 1)
        lane_f = lane.astype(F32)
        big = float(LANES)
        is_g = lane < N_GROUPS
        gl = jnp.where(is_g, logits, NEG)
        gmax = jnp.max(gl, axis=-1, keepdims=True)
        grp = jnp.min(jnp.where(gl == gmax, lane_f, big), axis=-1, keepdims=True)
        p_grp = 1.0 / jnp.sum(jnp.where(is_g, jnp.exp(logits - gmax), 0.0), axis=-1, keepdims=True)
        e_lo = N_GROUPS + EXPERTS_PER_GROUP * grp
        in_grp = (lane_f >= e_lo) & (lane_f < e_lo + EXPERTS_PER_GROUP)
        el = jnp.where(in_grp, logits, NEG)
        t1 = jnp.max(el, axis=-1, keepdims=True)
        i1 = jnp.min(jnp.where(el == t1, lane_f, big), axis=-1, keepdims=True)
        el2 = jnp.where(lane_f == i1, NEG, el)
        t2 = jnp.max(el2, axis=-1, keepdims=True)
        i2 = jnp.min(jnp.where(el2 == t2, lane_f, big), axis=-1, keepdims=True)
        e21 = jnp.exp(t2 - t1)
        gate1 = p_grp / (1.0 + e21)
        gate2 = p_grp * e21 / (1.0 + e21)
        route_ref[rows, :] = jnp.where(lane == 0, i1 - N_GROUPS,
                                       jnp.where(lane == 1, i2 - N_GROUPS,
                                                 jnp.where(lane == 2, gate1, jnp.where(lane == 3, gate2, 0.0))))
        picked = jnp.where((lane_f == i1 - N_GROUPS) | (lane_f == i2 - N_GROUPS), 1.0, 0.0)
        return jnp.sum(picked, axis=0, keepdims=True)

    def router_logits(hh, hl):
        return (jnp.dot(hh, wrh_ref[...], preferred_element_type=F32)
                + jnp.dot(hl, wrh_ref[...], preferred_element_type=F32)
                + jnp.dot(hh, wrl_ref[...], preferred_element_type=F32)) + br_ref[...]

    parts = range(OUTPROJ_PARTS)
    projs, halves, logits, counts = [], [], [], []
    for p in parts:
        projs.append(jnp.dot(mixer_outputs(p), w_ref[...], preferred_element_type=F32))
    for p in parts:
        halves.append(residual_norm(p, projs[p]))
        logits.append(router_logits(*halves[p]))
    for p in parts:
        counts.append(route(p, logits[p]))

    @pl.when(pl.program_id(0) == 0)
    def _():
        cnt_ref[...] = jnp.zeros(cnt_ref.shape, F32)

    cnt_ref[...] = cnt_ref[...] + sum(counts)


def _outproj(oa, ob_parts, oc, x2, w_out, mix_g, g2, wr_hi, wr_lo, b_r, seq, tm=512):
    n = x2.shape[0]
    nsb = seq // tm
    (o1, lw1), (o4, lw4), (o16, lw16) = ob_parts
    o1, lw1 = o1.reshape(n, CHUNK), lw1.reshape(n, CHUNK)
    row = lambda w: pl.BlockSpec((tm, w), lambda i: (i, 0))
    full = lambda r, w: pl.BlockSpec((r, w), lambda i: (0, 0))
    dil = lambda d: pl.BlockSpec((1, tm // d, d * CHUNK), lambda i: (i // nsb, i % nsb, 0))
    return pl.pallas_call(
        _outproj_kernel,
        grid=(n // tm,),
        in_specs=[row(CHUNK), row(CHUNK), dil(4), dil(16), row(CHUNK), dil(4), dil(16),
                  row(2 * CHUNK), row(D_MODEL),
                  full(D_MODEL, D_MODEL), full(1, D_MODEL), full(1, D_MODEL),
                  full(D_MODEL, LANES), full(D_MODEL, LANES), full(1, LANES)],
        out_specs=[row(D_MODEL), row(D_MODEL // 2), row(LANES), full(SUBLANES, LANES)],
        out_shape=[jax.ShapeDtypeStruct((n, D_MODEL), F32),
                   jax.ShapeDtypeStruct((n, D_MODEL // 2), PACKED),
                   jax.ShapeDtypeStruct((n, LANES), F32),
                   jax.ShapeDtypeStruct((SUBLANES, LANES), F32)],
        scratch_shapes=[pltpu.VMEM((8, tm, LANES), F32)],
        compiler_params=_cparams(("arbitrary",)),
        name="outproj_router",
    )(oa, o1, o4, o16, lw1, lw4, lw16, oc, x2, w_out, mix_g, g2, wr_hi, wr_lo, b_r)


META_ROWS = 384


def _rank_kernel(route_ref, cnt_ref, dest_ref, meta_ref, base_s, tri_s, *, tm):
    i = pl.program_id(0)
    lane = lax.broadcasted_iota(jnp.int32, (tm, LANES), 1)
    lane_f = lane.astype(F32)
    route = route_ref[...]
    oh0 = jnp.where(lane_f == route[:, 0:1], 1.0, 0.0)
    oh1 = jnp.where(lane_f == route[:, 1:2], 1.0, 0.0)
    oh = oh0 + oh1

    @pl.when(i == 0)
    def _():
        r = lax.broadcasted_iota(jnp.int32, (tm, tm), 0)
        c = lax.broadcasted_iota(jnp.int32, (tm, tm), 1)
        tri_s[...] = jnp.where(c < r, 1.0, 0.0).astype(BF16)
        lane8 = lax.broadcasted_iota(jnp.int32, cnt_ref.shape, 1)
        padded = jnp.floor((cnt_ref[...] + (MOE_BLOCK - 1)) * (1.0 / MOE_BLOCK)) * MOE_BLOCK
        ends = padded
        for sh in (1, 2, 4, 8, 16):
            ends = ends + jnp.where(lane8 >= sh, pltpu.roll(ends, sh, 1), 0.0)
        base_s[...] = ends - padded
        pos = lax.broadcasted_iota(jnp.int32, (META_ROWS, LANES), 0).astype(F32) * MOE_BLOCK
        lane_m = lax.broadcasted_iota(jnp.int32, (META_ROWS, LANES), 1)
        ends_row = ends[0:1, :]
        hit = jnp.where((lane_m < N_EXPERTS) & (ends_row <= pos), 1.0, 0.0)
        blk_e = jnp.minimum(jnp.sum(hit, axis=-1, keepdims=True), N_EXPERTS - 1.0)
        total = jnp.sum(jnp.where(lane_m[0:1] == N_EXPERTS - 1, ends_row, 0.0), axis=-1, keepdims=True)
        row_m = lax.broadcasted_iota(jnp.int32, (META_ROWS, 1), 0)
        meta = jnp.where(row_m == META_ROWS - 1, total * (1.0 / MOE_BLOCK), blk_e)
        meta_ref[...] = meta.astype(jnp.int32)

    prior = jnp.dot(tri_s[...], oh.astype(BF16), preferred_element_type=F32)
    val = base_s[0:1, :] + prior
    d0 = jnp.sum(oh0 * val, axis=-1, keepdims=True)
    d1 = jnp.sum(oh1 * val, axis=-1, keepdims=True)
    dest_ref[...] = jnp.where(lane == 0, d0, jnp.where(lane == 1, d1, 0.0)).astype(jnp.int32)
    base_s[...] = base_s[...] + jnp.sum(oh, axis=0, keepdims=True)


def _rank(route, counts, tm=512):
    n = route.shape[0]
    return pl.pallas_call(
        functools.partial(_rank_kernel, tm=tm),
        grid=(n // tm,),
        in_specs=[pl.BlockSpec((tm, LANES), lambda i: (i, 0)), pl.BlockSpec((SUBLANES, LANES), lambda i: (0, 0))],
        out_specs=[pl.BlockSpec((tm, LANES), lambda i: (i, 0)),
                   pl.BlockSpec((META_ROWS, 1), lambda i: (0, 0))],
        out_shape=[jax.ShapeDtypeStruct((n, LANES), jnp.int32),
                   jax.ShapeDtypeStruct((META_ROWS, 1), jnp.int32)],
        scratch_shapes=[pltpu.VMEM((SUBLANES, LANES), F32), pltpu.VMEM((tm, tm), BF16)],
        compiler_params=_cparams(("arbitrary",)),
        name="moe_rank",
    )(route, counts)


DISPATCH_BUFS = 3


def _dispatch_kernel(dest_ref, h_hbm, xs_in_hbm, xs_hbm, hbuf, in_sem, out_sem, *, n, tc):
    del xs_in_hbm
    i = pl.program_id(0)
    nt = pl.num_programs(0)
    buf = i % DISPATCH_BUFS
    prev = (i + DISPATCH_BUFS - 1) % DISPATCH_BUFS
    t0 = i * tc

    def load(tile, b):
        return pltpu.make_async_copy(h_hbm.at[pl.ds(tile * tc, tc)], hbuf.at[b], in_sem.at[b])

    def drain_rows(b):
        for _ in range(2):
            pltpu.make_async_copy(hbuf.at[b], xs_hbm.at[pl.ds(0, tc)], out_sem.at[b]).wait()

    @pl.when(i == 0)
    def _():
        load(0, 0).start()

        @pl.when(nt > 1)
        def _():
            load(1, 1).start()

    load(i, buf).wait()

    def group(g, carry):
        base = pl.multiple_of(g * SUBLANES, SUBLANES)
        for u in range(SUBLANES):
            src = hbuf.at[buf, pl.ds(base + u, 1)]
            for k in range(2):
                pltpu.make_async_copy(src, xs_hbm.at[pl.ds(dest_ref[k * n + t0 + base + u], 1)],
                                      out_sem.at[buf]).start(priority=k)
        return carry

    lax.fori_loop(0, tc // SUBLANES, group, 0)

    @pl.when(i > 0)
    def _():
        drain_rows(prev)

    @pl.when(i + 2 < nt)
    def _():
        load(i + 2, prev).start()

    @pl.when(i == nt - 1)
    def _():
        drain_rows(buf)


def _dispatch(dest_flat, h, slots, tc=512):
    n, w = h.shape
    cap = slots.shape[0]
    return pl.pallas_call(
        functools.partial(_dispatch_kernel, n=n, tc=tc),
        grid_spec=pltpu.PrefetchScalarGridSpec(
            num_scalar_prefetch=1,
            grid=(n // tc,),
            in_specs=[pl.BlockSpec(memory_space=pl.ANY), pl.BlockSpec(memory_space=pl.ANY)],
            out_specs=pl.BlockSpec(memory_space=pl.ANY),
            scratch_shapes=[pltpu.VMEM((DISPATCH_BUFS, tc, w), h.dtype),
                            pltpu.SemaphoreType.DMA((DISPATCH_BUFS,)), pltpu.SemaphoreType.DMA((DISPATCH_BUFS,))],
        ),
        out_shape=jax.ShapeDtypeStruct((cap, w), h.dtype),
        input_output_aliases={2: 0},
        compiler_params=_cparams(("arbitrary",)),
        name="moe_dispatch",
    )(dest_flat, h, slots)


def _expert_kernel(blk_e_ref, nact_ref, xs_ref, wg_ref, wu_ref, wd_ref, y_ref, wg_s, wu_s, wd_s):
    i = pl.program_id(0)
    active = i < nact_ref[0]
    changed = (i == 0) | (blk_e_ref[i] != blk_e_ref[jnp.maximum(i - 1, 0)])

    @pl.when(active & changed)
    def _():
        wg_s[...] = wg_ref[...].astype(BF16)
        wu_s[...] = wu_ref[...].astype(BF16)
        wd_s[...] = wd_ref[...].astype(BF16)

    @pl.when(active)
    def _():
        hi, lo = _unpack_bf16_pair(xs_ref[...])
        xb = jnp.concatenate([hi.astype(BF16), lo.astype(BF16)], axis=1)
        g = jnp.dot(xb, wg_s[...], preferred_element_type=F32)
        u = jnp.dot(xb, wu_s[...], preferred_element_type=F32)
        a = (g * jax.nn.sigmoid(g) * u).astype(BF16)
        y = jnp.dot(a, wd_s[...], preferred_element_type=F32)
        y_ref[...] = _pack_bf16_pair(y[:, :D_MODEL // 2], y[:, D_MODEL // 2:])

    @pl.when(jnp.logical_not(active))
    def _():
        y_ref[...] = jnp.zeros(y_ref.shape, y_ref.dtype)


def _experts(blk_e, nact, xs, w_gate, w_up, w_down, layer):
    cap = xs.shape[0]
    n_blk = cap // MOE_BLOCK

    def blk(i, be, na):
        return jnp.minimum(i, na[0] - 1)

    return pl.pallas_call(
        _expert_kernel,
        grid_spec=pltpu.PrefetchScalarGridSpec(
            num_scalar_prefetch=2,
            grid=(n_blk,),
            in_specs=[
                pl.BlockSpec((MOE_BLOCK, D_MODEL // 2), lambda i, be, na: (blk(i, be, na), 0)),
                pl.BlockSpec((None, None, D_MODEL, D_EXPERT), lambda i, be, na: (layer, be[blk(i, be, na)], 0, 0)),
                pl.BlockSpec((None, None, D_MODEL, D_EXPERT), lambda i, be, na: (layer, be[blk(i, be, na)], 0, 0)),
                pl.BlockSpec((None, None, D_EXPERT, D_MODEL), lambda i, be, na: (layer, be[blk(i, be, na)], 0, 0)),
            ],
            out_specs=pl.BlockSpec((MOE_BLOCK, D_MODEL // 2), lambda i, be, na: (i, 0)),
            scratch_shapes=[pltpu.VMEM((D_MODEL, D_EXPERT), BF16), pltpu.VMEM((D_MODEL, D_EXPERT), BF16),
                            pltpu.VMEM((D_EXPERT, D_MODEL), BF16)],
        ),
        out_shape=jax.ShapeDtypeStruct((cap, D_MODEL // 2), PACKED),
        compiler_params=_cparams(("arbitrary",)),
        name="moe_experts",
    )(blk_e, nact, xs, w_gate, w_up, w_down)


def _combine_kernel(dest_ref, x1_ref, route_ref, y_hbm, o_ref, yb, sem, *, n, tc):
    i = pl.program_id(0)
    last = pl.num_programs(0) - 1

    def gather(tile, sl):
        t0 = tile * tc

        def body(j, carry):
            for k in range(2):
                pltpu.make_async_copy(y_hbm.at[pl.ds(dest_ref[k * n + t0 + j], 1)], yb.at[sl, k, pl.ds(j, 1)],
                                      sem.at[sl]).start(priority=k)
            return carry

        lax.fori_loop(0, tc, body, 0, unroll=8)

    def drain(sl):
        for k in range(2):
            pltpu.make_async_copy(y_hbm.at[pl.ds(0, tc)], yb.at[sl, k], sem.at[sl]).wait()

    def combine(sl):
        rows = slice(sl * tc, (sl + 1) * tc)
        route = route_ref[rows, :]
        hi0, lo0 = _unpack_bf16_pair(yb[sl, 0])
        hi1, lo1 = _unpack_bf16_pair(yb[sl, 1])
        half = D_MODEL // 2
        o_ref[rows, :half] = x1_ref[rows, :half] + route[:, 2:3] * hi0 + route[:, 3:4] * hi1
        o_ref[rows, half:] = x1_ref[rows, half:] + route[:, 2:3] * lo0 + route[:, 3:4] * lo1

    @pl.when(i == 0)
    def _():
        gather(0, 0)

    drain(0)
    gather(2 * i + 1, 1)
    combine(0)
    drain(1)

    @pl.when(i < last)
    def _():
        gather(2 * i + 2, 0)

    combine(1)


def _combine(dest_flat, x1, route, y, tc=512):
    n = x1.shape[0]
    return pl.pallas_call(
        functools.partial(_combine_kernel, n=n, tc=tc),
        grid_spec=pltpu.PrefetchScalarGridSpec(
            num_scalar_prefetch=1,
            grid=(n // (2 * tc),),
            in_specs=[pl.BlockSpec((2 * tc, D_MODEL), lambda i, d: (i, 0)),
                      pl.BlockSpec((2 * tc, LANES), lambda i, d: (i, 0)),
                      pl.BlockSpec(memory_space=pl.ANY)],
            out_specs=pl.BlockSpec((2 * tc, D_MODEL), lambda i, d: (i, 0)),
            scratch_shapes=[pltpu.VMEM((2, 2, tc, D_MODEL // 2), PACKED), pltpu.SemaphoreType.DMA((2,))],
        ),
        out_shape=jax.ShapeDtypeStruct((n, D_MODEL), F32),
        compiler_params=_cparams(("arbitrary",)),
        name="moe_combine",
    )(dest_flat, x1, route, y)


def _rope_tables(seq):
    n_freq = HEAD_DIM // 4
    t = np.arange(seq)
    inv_freq = jnp.asarray(ROPE_THETA, F32) ** (-jnp.arange(n_freq, dtype=F32) / n_freq)
    pos = jnp.stack([jnp.asarray(t // GRID_W, F32), jnp.asarray(t % GRID_W, F32)], axis=1)
    ang = pos[:, :, None] * inv_freq
    cos = jnp.broadcast_to(jnp.cos(ang)[:, :, None, :], (seq, 2, 2, n_freq)).reshape(seq, HEAD_DIM)
    sin = jnp.broadcast_to(jnp.sin(ang)[:, :, None, :], (seq, 2, 2, n_freq)).reshape(seq, HEAD_DIM)
    first_half = (np.arange(HEAD_DIM) % (2 * n_freq)) < n_freq
    sa = jnp.where(first_half, -sin, 0.0)
    sb = jnp.where(first_half, 0.0, sin)
    rep = CHUNK // HEAD_DIM
    return jnp.tile(cos, (1, rep)), jnp.tile(sa, (1, rep)), jnp.tile(sb, (1, rep))


def _prep_w_in(w):
    base = w[:, :2048]
    kc0, kc1 = w[:, 2048:2112], w[:, 2112:2176]
    vc0, vc1 = w[:, 2176:2240], w[:, 2240:2304]
    z = jnp.zeros_like(vc0)
    return jnp.concatenate([base, kc0, kc0, kc1, kc1, vc0, z, vc1, z], axis=1).astype(BF16)


def _prep_gain(qk_g):
    qs = SCALE * LOG2E
    ones = jnp.ones((CHUNK,), F32)
    t4 = lambda g: jnp.tile(g, CHUNK // HEAD_DIM)
    return jnp.concatenate([
        t4(qk_g[0, 0]) * qs, t4(qk_g[0, 1]), ones,
        t4(qk_g[1, 0]) * qs, t4(qk_g[1, 1]), ones,
        t4(qk_g[2, 0]) * qs, t4(qk_g[2, 0]) * qs, t4(qk_g[2, 1]), ones,
    ])[None, :]


def _split_hi_lo(w):
    hi = w.astype(BF16)
    lo = (w - hi.astype(F32)).astype(BF16)
    return hi, lo


def _mixer_layer(x2, bsz, seq, tables, norm1_g, w_in, qk_g, rpb, mix_g, w_out, norm2_g, w_rg, b_rg, w_re, b_re):
    n, d = x2.shape
    cos_t, sa_t, sb_t, bd = tables
    proj, pb4, pb16 = _inproj(x2, norm1_g[None, :], _prep_w_in(w_in), _prep_gain(qk_g), bd, cos_t, sa_t, sb_t, seq)
    proj3 = proj.reshape(bsz, seq, PROJ_W)
    oa = _na_attention(proj3, _na_bias_table(rpb)).reshape(n, CHUNK)
    ob_parts = [_dilated_branch(proj3[:, None], CH_QB, 1, 8), _dilated_branch(pb4, 0, 1, 8),
                _dilated_branch(pb16, 0, 4, 2)]
    oc = _gqa_attention(proj3).reshape(n, 2 * CHUNK)

    w_r = jnp.zeros((d, LANES), F32)
    w_r = w_r.at[:, :N_GROUPS].set(w_rg).at[:, N_GROUPS:N_GROUPS + N_EXPERTS].set(w_re)
    b_r = jnp.zeros((1, LANES), F32)
    b_r = b_r.at[0, :N_GROUPS].set(b_rg).at[0, N_GROUPS:N_GROUPS + N_EXPERTS].set(b_re)
    wr_hi, wr_lo = _split_hi_lo(w_r)
    return _outproj(oa, ob_parts, oc, x2, w_out.astype(BF16), mix_g[None, :], norm2_g[None, :], wr_hi, wr_lo, b_r,
                    seq)


def _moe_layer(x1, h, route, counts, slots, w_gate, w_up, w_down, layer):
    n = x1.shape[0]
    n_blk = slots.shape[0] // MOE_BLOCK
    assert n_blk < META_ROWS and slots.shape[0] <= 2 ** 24
    dest, meta = _rank(route, counts)
    dest_flat = dest[:, :2].T.reshape(2 * n)
    xs = _dispatch(dest_flat, h, slots)
    y = _experts(meta[:n_blk, 0], meta[META_ROWS - 1:, 0], xs, w_gate, w_up, w_down, layer)
    return _combine(dest_flat, x1, route, y), xs


def _tables(seq):
    hd = np.arange(CHUNK) // HEAD_DIM
    return (*_rope_tables(seq), jnp.asarray((hd[:, None] == hd[None, :]) * (1.0 / HEAD_DIM), BF16))


@jax.jit
def kernel(x, norm1_g, w_in, qk_norm_g, na_rpb, mix_norm_g, w_out, norm2_g, w_router_group, b_router_group,
           w_router_expert, b_router_expert, w_gate, w_up, w_down):
    bsz, seq, d = x.shape
    tables = _tables(seq)
    x2 = x.reshape(bsz * seq, d)
    slots = jnp.zeros((2 * bsz * seq + N_EXPERTS * MOE_BLOCK, d // 2), PACKED)
    for l in range(DEPTH):
        x1, h, route, counts = _mixer_layer(x2, bsz, seq, tables, norm1_g[l], w_in[l], qk_norm_g[l], na_rpb[l],
                                            mix_norm_g[l], w_out[l], norm2_g[l], w_router_group[l],
                                            b_router_group[l], w_router_expert[l], b_router_expert[l])
        x2, slots = _moe_layer(x1, h, route, counts, slots, w_gate, w_up, w_down, l)
    return x2.reshape(bsz, seq, d)
```

```python
import functools
import math

import numpy as np
import jax
import jax.numpy as jnp
from jax import lax
from jax.experimental import pallas as pl
from jax.experimental.pallas import tpu as pltpu

F32 = jnp.float32
BF16 = jnp.bfloat16

D_MODEL = 1024
DEPTH = 4
GRID_W = 64
HEAD_DIM = 64
RMS_EPS = 1e-6
NEG = -1e30
LOG2E = math.log2(math.e)
SCALE = HEAD_DIM ** -0.5

NA_ROWS = 8
NA_COLS = 16
DILATIONS = (1, 4, 16)
DIL_HALF = 64
ROPE_THETA = 10000.0

N_GROUPS = 4
EXPERTS_PER_GROUP = 8
N_EXPERTS = 32
D_EXPERT = 512
MOE_BLOCK = 512

LANES = 128
SUBLANES = 8
CHUNK = 256
CH_QA, CH_KA, CH_VA, CH_QB, CH_KB, CH_VB, CH_QC, CH_KC, CH_VC = 0, 1, 2, 3, 4, 5, 6, 8, 9
N_CHUNKS = 10
PROJ_W = N_CHUNKS * CHUNK
NORM_CHUNKS = (CH_QA, CH_KA, CH_QB, CH_KB, CH_QC, CH_QC + 1, CH_KC)
ROPE_CHUNKS = (CH_QC, CH_QC + 1, CH_KC)

VMEM_LIMIT = 56 * 1024 * 1024

NT_DIMS = (((1,), (1,)), ((), ()))


def _cparams(sem):
    return pltpu.CompilerParams(dimension_semantics=sem, vmem_limit_bytes=VMEM_LIMIT)


ROPE_SHIFT = HEAD_DIM // 4


INPROJ_PARTS = 2


def _inproj_kernel(x_ref, g1_ref, w_ref, gain_ref, bd_ref, cos_ref, sa_ref, sb_ref, o_ref, o4_ref, o16_ref, stage_ref):
    tm = x_ref.shape[0]
    tp = tm // INPROJ_PARTS
    part_rows = [slice(i * tp, (i + 1) * tp) for i in range(INPROJ_PARTS)]
    chunk_cols = [slice(c * CHUNK, (c + 1) * CHUNK) for c in range(N_CHUNKS)]
    n_stage = 3 * CHUNK // LANES

    normed = []
    for rows in part_rows:
        x = x_ref[rows, :]
        ms = jnp.mean(x * x, axis=-1, keepdims=True)
        normed.append((x * lax.rsqrt(ms + RMS_EPS) * g1_ref[...]).astype(BF16))
    projs = [jnp.dot(h, w_ref[...], preferred_element_type=F32) for h in normed]
    for part, (rows, proj) in enumerate(zip(part_rows, projs)):
        meansq = {c: jnp.dot(jnp.square(proj[:, chunk_cols[c]]).astype(BF16), bd_ref[...],
                             preferred_element_type=F32) for c in NORM_CHUNKS}
        for c in range(N_CHUNKS):
            cols = chunk_cols[c]
            p = proj[:, cols]
            if c in NORM_CHUNKS:
                p = p * lax.rsqrt(meansq[c] + RMS_EPS) * gain_ref[:, cols]
            if c in ROPE_CHUNKS:
                p = (p * cos_ref[rows, :] + pltpu.roll(p, CHUNK - ROPE_SHIFT, 1) * sa_ref[rows, :]
                     + pltpu.roll(p, ROPE_SHIFT, 1) * sb_ref[rows, :])
            if c == CH_VC:
                lane = lax.broadcasted_iota(jnp.int32, p.shape, 1)
                p = jnp.where(lane % LANES >= HEAD_DIM, 1.0, p)
            o_ref[rows, cols] = p.astype(BF16)
            if c in (CH_QB, CH_KB, CH_VB):
                for hf in range(CHUNK // LANES):
                    lane0 = ((c - CH_QB) * (CHUNK // LANES) + hf) * LANES
                    slot = part * n_stage + lane0 // LANES
                    stage_ref[slot] = p[:, hf * LANES:(hf + 1) * LANES]
                    for dil, od_ref in ((4, o4_ref), (16, o16_ref)):
                        nr = tp // dil
                        for r in range(dil):
                            picked = stage_ref[slot, pl.ds(r, nr, stride=dil), :]
                            od_ref[0, r, part * nr:(part + 1) * nr, lane0:lane0 + LANES] = picked.astype(BF16)


def _inproj(x2, g1, w, gain, bd, cos_t, sa_t, sb_t, seq, tm=512):
    n = x2.shape[0]
    nsb = seq // tm
    bsz = n // seq
    dil_spec = lambda dil: pl.BlockSpec((1, dil, tm // dil, 3 * CHUNK), lambda i: (i // nsb, 0, i % nsb, 0))
    dil_shape = lambda dil: jax.ShapeDtypeStruct((bsz, dil, seq // dil, 3 * CHUNK), BF16)
    return pl.pallas_call(
        _inproj_kernel,
        grid=(n // tm,),
        in_specs=[
            pl.BlockSpec((tm, D_MODEL), lambda i: (i, 0)),
            pl.BlockSpec((1, D_MODEL), lambda i: (0, 0)),
            pl.BlockSpec((D_MODEL, PROJ_W), lambda i: (0, 0)),
            pl.BlockSpec((1, PROJ_W), lambda i: (0, 0)),
            pl.BlockSpec((CHUNK, CHUNK), lambda i: (0, 0)),
            pl.BlockSpec((tm, CHUNK), lambda i: (i % nsb, 0)),
            pl.BlockSpec((tm, CHUNK), lambda i: (i % nsb, 0)),
            pl.BlockSpec((tm, CHUNK), lambda i: (i % nsb, 0)),
        ],
        out_specs=[pl.BlockSpec((tm, PROJ_W), lambda i: (i, 0)), dil_spec(4), dil_spec(16)],
        out_shape=[jax.ShapeDtypeStruct((n, PROJ_W), BF16), dil_shape(4), dil_shape(16)],
        scratch_shapes=[pltpu.VMEM((INPROJ_PARTS * 3 * CHUNK // LANES, tm // INPROJ_PARTS, LANES), F32)],
        compiler_params=_cparams(("arbitrary",)),
        name="inproj",
    )(x2, g1, w, gain, bd, cos_t, sa_t, sb_t)


def _stack_head_pair(qp, lo):
    zero = jnp.zeros_like(qp)
    return jnp.concatenate([jnp.where(lo, qp, zero), jnp.where(lo, zero, qp)], axis=0)


PACKED = jnp.uint32


def _pack_bf16_pair(a, b):
    ua = lax.bitcast_convert_type(a.astype(BF16).astype(F32), PACKED)
    ub = lax.bitcast_convert_type(b.astype(BF16).astype(F32), PACKED)
    return ua | (ub >> 16)


def _unpack_bf16_pair(u):
    hi = lax.bitcast_convert_type(u & PACKED(0xFFFF0000), F32)
    lo = lax.bitcast_convert_type(u << 16, F32)
    return hi, lo


def _lo_mask(rows):
    return lax.broadcasted_iota(jnp.int32, (rows, LANES), 1) < HEAD_DIM


NA_ROW_GROUP = 4


def _na_kernel(q_ref, k_ref, v_ref, bias_ref, o_ref, *, rows_per_step, n_rows):
    i = pl.program_id(1)
    lo = _lo_mask(GRID_W)
    win = NA_ROWS * GRID_W

    def window(rl):
        r = i * rows_per_step + rl
        r0 = jnp.clip(r - NA_ROWS // 2, 0, n_rows - NA_ROWS)
        return pl.multiple_of(r0 * GRID_W, GRID_W), r - r0

    for g0 in range(0, rows_per_step, NA_ROW_GROUP):
        chains = [(rl, p) for rl in range(g0, g0 + NA_ROW_GROUP) for p in range(2)]
        scores = []
        for rl, p in chains:
            start, rr = window(rl)
            cols = slice(p * LANES, (p + 1) * LANES)
            qs = _stack_head_pair(q_ref[0, rl * GRID_W:(rl + 1) * GRID_W, cols], lo)
            s = lax.dot_general(qs, k_ref[0, pl.ds(start, win), cols], NT_DIMS, preferred_element_type=F32)
            scores.append(s + bias_ref[p, rr])
        stats = []
        for s in scores:
            e = jnp.exp2(s - jnp.max(s, axis=-1, keepdims=True))
            stats.append((e, jnp.sum(e, axis=-1, keepdims=True)))
        for (rl, p), (e, l) in zip(chains, stats):
            start, _ = window(rl)
            cols = slice(p * LANES, (p + 1) * LANES)
            pv = jnp.dot(e.astype(BF16), v_ref[0, pl.ds(start, win), cols], preferred_element_type=F32)
            o = pv / l
            o_ref[0, rl * GRID_W:(rl + 1) * GRID_W, cols] = jnp.where(lo, o[:GRID_W], o[GRID_W:]).astype(BF16)


def _na_attention(proj3, bias, rows_per_step=16):
    b, s, _ = proj3.shape
    n_rows = s // GRID_W
    tq = rows_per_step * GRID_W
    return pl.pallas_call(
        functools.partial(_na_kernel, rows_per_step=rows_per_step, n_rows=n_rows),
        grid=(b, n_rows // rows_per_step),
        in_specs=[
            pl.BlockSpec((1, tq, CHUNK), lambda bi, i: (bi, i, CH_QA)),
            pl.BlockSpec((1, s, CHUNK), lambda bi, i: (bi, 0, CH_KA)),
            pl.BlockSpec((1, s, CHUNK), lambda bi, i: (bi, 0, CH_VA)),
            pl.BlockSpec((2, NA_ROWS, 2 * GRID_W, NA_ROWS * GRID_W), lambda bi, i: (0, 0, 0, 0)),
        ],
        out_specs=pl.BlockSpec((1, tq, CHUNK), lambda bi, i: (bi, i, 0)),
        out_shape=jax.ShapeDtypeStruct((b, s, CHUNK), BF16),
        compiler_params=_cparams(("arbitrary", "arbitrary")),
        name="na_attention",
    )(proj3, proj3, proj3, bias)


def _na_bias_table(rpb):
    c = np.arange(GRID_W)
    cs = np.clip(c - NA_COLS // 2, 0, GRID_W - NA_COLS)
    kc = np.arange(GRID_W)
    inwin = (kc[None, :] >= cs[:, None]) & (kc[None, :] < cs[:, None] + NA_COLS)
    dc = kc[None, :] - c[:, None] + NA_COLS - 1
    rr = np.arange(NA_ROWS)
    dr = rr[None, :] - rr[:, None] + NA_ROWS - 1
    sel_r = jnp.asarray(dr[:, :, None] == np.arange(2 * NA_ROWS - 1), F32)
    sel_c = jnp.asarray((dc[:, :, None] == np.arange(2 * NA_COLS - 1)) & inwin[:, :, None], F32)
    tab = jnp.einsum("rka,hab,cjb->hrckj", sel_r, rpb, sel_c, precision=lax.Precision.HIGHEST)
    tab = jnp.where(inwin[None, None, :, None, :], tab * LOG2E, NEG)
    tab = tab.reshape(2, 2, NA_ROWS, GRID_W, NA_ROWS * GRID_W)
    return tab.transpose(0, 2, 1, 3, 4).reshape(2, NA_ROWS, 2 * GRID_W, NA_ROWS * GRID_W)


DIL_TQ = 128
DIL_WIN = DIL_TQ + 2 * DIL_HALF
DIL_GROUP = 2


def _dil_kernel(q_ref, k_ref, v_ref, bias_ref, o_ref, lw_ref, *, length, rg, ug, nb):
    u0 = pl.program_id(2) * ug
    lo = _lo_mask(DIL_TQ)
    chains = [(r, ub, p) for r in range(rg) for ub in range(ug) for p in range(2)]

    def window(ub):
        gb = u0 + ub
        ws = pl.multiple_of(jnp.clip(gb * DIL_TQ - DIL_HALF, 0, length - DIL_WIN), DIL_HALF)
        case = jnp.where(gb == 0, 0, jnp.where(gb == nb - 1, 2, 1))
        return ws, case

    def scores_of(group):
        out = []
        for r, ub, p in group:
            ws, case = window(ub)
            cols = slice(p * LANES, (p + 1) * LANES)
            qs = _stack_head_pair(q_ref[r, ub * DIL_TQ:(ub + 1) * DIL_TQ, cols], lo)
            s = lax.dot_general(qs, k_ref[r, pl.ds(ws, DIL_WIN), cols], NT_DIMS, preferred_element_type=F32)
            out.append(s + bias_ref[case, p])
        return out

    def finish(group, scores):
        stats = []
        for s in scores:
            m = jnp.max(s, axis=-1, keepdims=True)
            e = jnp.exp2(s - m)
            stats.append((m, e, jnp.sum(e, axis=-1, keepdims=True)))
        for (r, ub, p), (m, e, l) in zip(group, stats):
            ws, _ = window(ub)
            cols = slice(p * LANES, (p + 1) * LANES)
            pv = jnp.dot(e.astype(BF16), v_ref[r, pl.ds(ws, DIL_WIN), cols], preferred_element_type=F32)
            o = pv / l
            lw = jnp.broadcast_to(m + jnp.log2(l), (2 * DIL_TQ, LANES))
            rows = slice(ub * DIL_TQ, (ub + 1) * DIL_TQ)
            ocols = slice(r * CHUNK + p * LANES, r * CHUNK + (p + 1) * LANES)
            o_ref[0, rows, ocols] = jnp.where(lo, o[:DIL_TQ], o[DIL_TQ:]).astype(BF16)
            lw_ref[0, rows, ocols] = jnp.where(lo, lw[:DIL_TQ], lw[DIL_TQ:])

    groups = [chains[g:g + DIL_GROUP] for g in range(0, len(chains), DIL_GROUP)]
    pending = scores_of(groups[0])
    for g, group in enumerate(groups):
        ahead = scores_of(groups[g + 1]) if g + 1 < len(groups) else None
        finish(group, pending)
        pending = ahead


def _dil_bias_table(dil):
    slopes = np.exp2(-8.0 * np.arange(1, 5) / 4.0)
    qi = np.arange(DIL_TQ)[:, None]
    kj = np.arange(DIL_WIN)[None, :]
    tab = np.zeros((3, 4, DIL_TQ, DIL_WIN), np.float32)
    for case, shift in enumerate((0, -DIL_HALF, -2 * DIL_HALF)):
        rel = kj + shift - qi
        for h in range(4):
            tab[case, h] = np.where(np.abs(rel) <= DIL_HALF, -slopes[h] * dil * np.abs(rel) * LOG2E, NEG)
    return tab.reshape(3, 2, 2 * DIL_TQ, DIL_WIN)


def _dilated_branch(src, q_chunk, rg, ug):
    b, dil, length, _ = src.shape
    nb = length // DIL_TQ
    o, lw = pl.pallas_call(
        functools.partial(_dil_kernel, length=length, rg=rg, ug=ug, nb=nb),
        grid=(b, dil // rg, nb // ug),
        in_specs=[
            pl.BlockSpec((None, rg, ug * DIL_TQ, CHUNK), lambda bi, r, u: (bi, r, u, q_chunk)),
            pl.BlockSpec((None, rg, length, CHUNK), lambda bi, r, u: (bi, r, 0, q_chunk + 1)),
            pl.BlockSpec((None, rg, length, CHUNK), lambda bi, r, u: (bi, r, 0, q_chunk + 2)),
            pl.BlockSpec((3, 2, 2 * DIL_TQ, DIL_WIN), lambda bi, r, u: (0, 0, 0, 0)),
        ],
        out_specs=[
            pl.BlockSpec((1, ug * DIL_TQ, rg * CHUNK), lambda bi, r, u: (bi, u, r)),
            pl.BlockSpec((1, ug * DIL_TQ, rg * CHUNK), lambda bi, r, u: (bi, u, r)),
        ],
        out_shape=[
            jax.ShapeDtypeStruct((b, length, dil * CHUNK), BF16),
            jax.ShapeDtypeStruct((b, length, dil * CHUNK), F32),
        ],
        compiler_params=_cparams(("arbitrary", "arbitrary", "arbitrary")),
        name=f"dilated_d{dil}",
    )(src, src, src, jnp.asarray(_dil_bias_table(dil)))
    return o, lw


TN_DIMS = (((0,), (0,)), ((), ()))


def _gqa_kernel(q_ref, k_ref, v_ref, o_ref, m_s, acc_s, *, tq, tk, seq):
    lo = _lo_mask(tq)
    n_kt = seq // tk
    qs = jnp.concatenate([_stack_head_pair(q_ref[0, :, j * LANES:(j + 1) * LANES], lo) for j in range(2)],
                         axis=0)
    m_s[...] = jnp.full(m_s.shape, NEG, F32)
    acc_s[...] = jnp.zeros(acc_s.shape, F32)

    def scores(kt):
        return lax.dot_general(k_ref[0, kt * tk:(kt + 1) * tk, :], qs, NT_DIMS, preferred_element_type=F32)

    def softmax_values(kt, s):
        k0 = kt * tk
        m_old = m_s[...]
        m_new = jnp.maximum(m_old, jnp.max(s, axis=0, keepdims=True))
        alpha = jnp.exp2(m_old - m_new)
        e = jnp.exp2((s - m_new[0:1, :]).astype(BF16))
        m_s[...] = m_new
        pv = lax.dot_general(v_ref[0, k0:k0 + tk, :], e, TN_DIMS, preferred_element_type=F32)
        acc_s[...] = alpha[0:1, :] * acc_s[...] + pv

    s_cur = scores(0)
    for kt in range(n_kt):
        s_next = scores(kt + 1) if kt + 1 < n_kt else None
        softmax_values(kt, s_cur)
        s_cur = s_next

    acc = acc_s[...]
    o_t = acc[:HEAD_DIM] / acc[HEAD_DIM:HEAD_DIM + 1]
    for j in range(2):
        pair_t = jnp.concatenate([o_t[:, (2 * j) * tq:(2 * j + 1) * tq], o_t[:, (2 * j + 1) * tq:(2 * j + 2) * tq]],
                                 axis=0)
        o_ref[0, :, j * LANES:(j + 1) * LANES] = pair_t.T.astype(BF16)


def _gqa_attention(proj3, tq=256, tk=256):
    b, s, _ = proj3.shape
    assert (s // tk) % 2 == 0 and s // tk >= 2
    return pl.pallas_call(
        functools.partial(_gqa_kernel, tq=tq, tk=tk, seq=s),
        grid=(b, 2, s // tq),
        in_specs=[
            pl.BlockSpec((1, tq, CHUNK), lambda bi, g, i: (bi, i, CH_QC + g)),
            pl.BlockSpec((1, s, LANES), lambda bi, g, i: (bi, 0, 2 * CH_KC + g)),
            pl.BlockSpec((1, s, LANES), lambda bi, g, i: (bi, 0, 2 * CH_VC + g)),
        ],
        out_specs=pl.BlockSpec((1, tq, CHUNK), lambda bi, g, i: (bi, i, g)),
        out_shape=jax.ShapeDtypeStruct((b, s, 2 * CHUNK), BF16),
        scratch_shapes=[
            pltpu.VMEM((8, 4 * tq), F32),
            pltpu.VMEM((LANES, 4 * tq), F32),
        ],
        compiler_params=_cparams(("arbitrary", "arbitrary", "arbitrary")),
        name="gqa_attention",
    )(proj3, proj3, proj3)


def _rms(v, g):
    return v * lax.rsqrt(jnp.mean(v * v, axis=-1, keepdims=True) + RMS_EPS) * g


OUTPROJ_PARTS = 2


def _outproj_kernel(oa_ref, o1_ref, o4_ref, o16_ref, lw1_ref, lw4_ref, lw16_ref, oc_ref, x_ref,
                    w_ref, mg_ref, g2_ref, wrh_ref, wrl_ref, br_ref, x1_ref, h_ref, route_ref, cnt_ref, il_ref):
    tm = x_ref.shape[0]
    tp = tm // OUTPROJ_PARTS

    def interleaved(src_ref, dil, slot, part):
        nr = tp // dil
        for r in range(dil):
            for hf in range(CHUNK // LANES):
                piece = src_ref[0, part * nr:(part + 1) * nr,
                                r * CHUNK + hf * LANES:r * CHUNK + (hf + 1) * LANES].astype(F32)
                il_ref[slot * 2 + hf, pl.ds(part * tp + r, nr, stride=dil), :] = piece
        rows = slice(part * tp, (part + 1) * tp)
        return jnp.concatenate([il_ref[slot * 2, rows, :], il_ref[slot * 2 + 1, rows, :]], axis=1)

    def mixer_outputs(part):
        rows = slice(part * tp, (part + 1) * tp)
        lw1 = lw1_ref[rows, :]
        lw4, lw16 = interleaved(lw4_ref, 4, 0, part), interleaved(lw16_ref, 16, 1, part)
        o4, o16 = interleaved(o4_ref, 4, 2, part), interleaved(o16_ref, 16, 3, part)
        lwm = jnp.maximum(jnp.maximum(lw1, lw4), lw16)
        w1, w4, w16 = jnp.exp2(lw1 - lwm), jnp.exp2(lw4 - lwm), jnp.exp2(lw16 - lwm)
        ob = (w1 * o1_ref[rows, :].astype(F32) + w4 * o4 + w16 * o16) / (w1 + w4 + w16)
        return jnp.concatenate([
            _rms(oa_ref[rows, :].astype(F32), mg_ref[:, 0:CHUNK]).astype(BF16),
            _rms(ob, mg_ref[:, CHUNK:2 * CHUNK]).astype(BF16),
            _rms(oc_ref[rows, :].astype(F32), mg_ref[:, 2 * CHUNK:]).astype(BF16),
        ], axis=-1)

    def residual_norm(part, proj):
        rows = slice(part * tp, (part + 1) * tp)
        x1 = x_ref[rows, :] + proj
        x1_ref[rows, :] = x1
        h = _rms(x1, g2_ref[...])
        h_ref[rows, :] = _pack_bf16_pair(h[:, :D_MODEL // 2], h[:, D_MODEL // 2:])
        hh = h.astype(BF16)
        return hh, (h - hh.astype(F32)).astype(BF16)

    def route(part, logits):
        rows = slice(part * tp, (part + 1) * tp)
        lane = lax.broadcasted_iota(jnp.int32, logits.shape, 1)
        lane_f = lane.astype(F32)
        big = float(LANES)
        is_g = lane < N_GROUPS
        gl = jnp.where(is_g, logits, NEG)
        gmax = jnp.max(gl, axis=-1, keepdims=True)
        grp = jnp.min(jnp.where(gl == gmax, lane_f, big), axis=-1, keepdims=True)
        p_grp = 1.0 / jnp.sum(jnp.where(is_g, jnp.exp(logits - gmax), 0.0), axis=-1, keepdims=True)
        e_lo = N_GROUPS + EXPERTS_PER_GROUP * grp
        in_grp = (lane_f >= e_lo) & (lane_f < e_lo + EXPERTS_PER_GROUP)
        el = jnp.where(in_grp, logits, NEG)
        t1 = jnp.max(el, axis=-1, keepdims=True)
        i1 = jnp.min(jnp.where(el == t1, lane_f, big), axis=-1, keepdims=True)
        el2 = jnp.where(lane_f == i1, NEG, el)
        t2 = jnp.max(el2, axis=-1, keepdims=True)
        i2 = jnp.min(jnp.where(el2 == t2, lane_f, big), axis=-1, keepdims=True)
        e21 = jnp.exp(t2 - t1)
        gate1 = p_grp / (1.0 + e21)
        gate2 = p_grp * e21 / (1.0 + e21)
        route_ref[rows, :] = jnp.where(lane == 0, i1 - N_GROUPS,
                                       jnp.where(lane == 1, i2 - N_GROUPS,
                                                 jnp.where(lane == 2, gate1, jnp.where(lane == 3, gate2, 0.0))))
        picked = jnp.where((lane_f == i1 - N_GROUPS) | (lane_f == i2 - N_GROUPS), 1.0, 0.0)
        return jnp.sum(picked, axis=0, keepdims=True)

    def router_logits(hh, hl):
        return (jnp.dot(hh, wrh_ref[...], preferred_element_type=F32)
                + jnp.dot(hl, wrh_ref[...], preferred_element_type=F32)
                + jnp.dot(hh, wrl_ref[...], preferred_element_type=F32)) + br_ref[...]

    parts = range(OUTPROJ_PARTS)
    projs, halves, logits, counts = [], [], [], []
    for p in parts:
        projs.append(jnp.dot(mixer_outputs(p), w_ref[...], preferred_element_type=F32))
    for p in parts:
        halves.append(residual_norm(p, projs[p]))
        logits.append(router_logits(*halves[p]))
    for p in parts:
        counts.append(route(p, logits[p]))

    @pl.when(pl.program_id(0) == 0)
    def _():
        cnt_ref[...] = jnp.zeros(cnt_ref.shape, F32)

    cnt_ref[...] = cnt_ref[...] + sum(counts)


def _outproj(oa, ob_parts, oc, x2, w_out, mix_g, g2, wr_hi, wr_lo, b_r, seq, tm=512):
    n = x2.shape[0]
    nsb = seq // tm
    (o1, lw1), (o4, lw4), (o16, lw16) = ob_parts
    o1, lw1 = o1.reshape(n, CHUNK), lw1.reshape(n, CHUNK)
    row = lambda w: pl.BlockSpec((tm, w), lambda i: (i, 0))
    full = lambda r, w: pl.BlockSpec((r, w), lambda i: (0, 0))
    dil = lambda d: pl.BlockSpec((1, tm // d, d * CHUNK), lambda i: (i // nsb, i % nsb, 0))
    return pl.pallas_call(
        _outproj_kernel,
        grid=(n // tm,),
        in_specs=[row(CHUNK), row(CHUNK), dil(4), dil(16), row(CHUNK), dil(4), dil(16),
                  row(2 * CHUNK), row(D_MODEL),
                  full(D_MODEL, D_MODEL), full(1, D_MODEL), full(1, D_MODEL),
                  full(D_MODEL, LANES), full(D_MODEL, LANES), full(1, LANES)],
        out_specs=[row(D_MODEL), row(D_MODEL // 2), row(LANES), full(SUBLANES, LANES)],
        out_shape=[jax.ShapeDtypeStruct((n, D_MODEL), F32),
                   jax.ShapeDtypeStruct((n, D_MODEL // 2), PACKED),
                   jax.ShapeDtypeStruct((n, LANES), F32),
                   jax.ShapeDtypeStruct((SUBLANES, LANES), F32)],
        scratch_shapes=[pltpu.VMEM((8, tm, LANES), F32)],
        compiler_params=_cparams(("arbitrary",)),
        name="outproj_router",
    )(oa, o1, o4, o16, lw1, lw4, lw16, oc, x2, w_out, mix_g, g2, wr_hi, wr_lo, b_r)


META_ROWS = 384


def _rank_kernel(route_ref, cnt_ref, dest_ref, meta_ref, base_s, tri_s, *, tm):
    i = pl.program_id(0)
    lane = lax.broadcasted_iota(jnp.int32, (tm, LANES), 1)
    lane_f = lane.astype(F32)
    route = route_ref[...]
    oh0 = jnp.where(lane_f == route[:, 0:1], 1.0, 0.0)
    oh1 = jnp.where(lane_f == route[:, 1:2], 1.0, 0.0)
    oh = oh0 + oh1

    @pl.when(i == 0)
    def _():
        r = lax.broadcasted_iota(jnp.int32, (tm, tm), 0)
        c = lax.broadcasted_iota(jnp.int32, (tm, tm), 1)
        tri_s[...] = jnp.where(c < r, 1.0, 0.0).astype(BF16)
        lane8 = lax.broadcasted_iota(jnp.int32, cnt_ref.shape, 1)
        padded = jnp.floor((cnt_ref[...] + (MOE_BLOCK - 1)) * (1.0 / MOE_BLOCK)) * MOE_BLOCK
        ends = padded
        for sh in (1, 2, 4, 8, 16):
            ends = ends + jnp.where(lane8 >= sh, pltpu.roll(ends, sh, 1), 0.0)
        base_s[...] = ends - padded
        pos = lax.broadcasted_iota(jnp.int32, (META_ROWS, LANES), 0).astype(F32) * MOE_BLOCK
        lane_m = lax.broadcasted_iota(jnp.int32, (META_ROWS, LANES), 1)
        ends_row = ends[0:1, :]
        hit = jnp.where((lane_m < N_EXPERTS) & (ends_row <= pos), 1.0, 0.0)
        blk_e = jnp.minimum(jnp.sum(hit, axis=-1, keepdims=True), N_EXPERTS - 1.0)
        total = jnp.sum(jnp.where(lane_m[0:1] == N_EXPERTS - 1, ends_row, 0.0), axis=-1, keepdims=True)
        row_m = lax.broadcasted_iota(jnp.int32, (META_ROWS, 1), 0)
        meta = jnp.where(row_m == META_ROWS - 1, total * (1.0 / MOE_BLOCK), blk_e)
        meta_ref[...] = meta.astype(jnp.int32)

    prior = jnp.dot(tri_s[...], oh.astype(BF16), preferred_element_type=F32)
    val = base_s[0:1, :] + prior
    d0 = jnp.sum(oh0 * val, axis=-1, keepdims=True)
    d1 = jnp.sum(oh1 * val, axis=-1, keepdims=True)
    dest_ref[...] = jnp.where(lane == 0, d0, jnp.where(lane == 1, d1, 0.0)).astype(jnp.int32)
    base_s[...] = base_s[...] + jnp.sum(oh, axis=0, keepdims=True)


def _rank(route, counts, tm=512):
    n = route.shape[0]
    return pl.pallas_call(
        functools.partial(_rank_kernel, tm=tm),
        grid=(n // tm,),
        in_specs=[pl.BlockSpec((tm, LANES), lambda i: (i, 0)), pl.BlockSpec((SUBLANES, LANES), lambda i: (0, 0))],
        out_specs=[pl.BlockSpec((tm, LANES), lambda i: (i, 0)),
                   pl.BlockSpec((META_ROWS, 1), lambda i: (0, 0))],
        out_shape=[jax.ShapeDtypeStruct((n, LANES), jnp.int32),
                   jax.ShapeDtypeStruct((META_ROWS, 1), jnp.int32)],
        scratch_shapes=[pltpu.VMEM((SUBLANES, LANES), F32), pltpu.VMEM((tm, tm), BF16)],
        compiler_params=_cparams(("arbitrary",)),
        name="moe_rank",
    )(route, counts)


DISPATCH_BUFS = 3


def _dispatch_kernel(dest_ref, h_hbm, xs_in_hbm, xs_hbm, hbuf, in_sem, out_sem, *, n, tc):
    del xs_in_hbm
    i = pl.program_id(0)
    nt = pl.num_programs(0)
    buf = i % DISPATCH_BUFS
    prev = (i + DISPATCH_BUFS - 1) % DISPATCH_BUFS
    t0 = i * tc

    def load(tile, b):
        return pltpu.make_async_copy(h_hbm.at[pl.ds(tile * tc, tc)], hbuf.at[b], in_sem.at[b])

    def drain_rows(b):
        for _ in range(2):
            pltpu.make_async_copy(hbuf.at[b], xs_hbm.at[pl.ds(0, tc)], out_sem.at[b]).wait()

    @pl.when(i == 0)
    def _():
        load(0, 0).start()

        @pl.when(nt > 1)
        def _():
            load(1, 1).start()

    load(i, buf).wait()

    def group(g, carry):
        base = pl.multiple_of(g * SUBLANES, SUBLANES)
        for u in range(SUBLANES):
            src = hbuf.at[buf, pl.ds(base + u, 1)]
            for k in range(2):
                pltpu.make_async_copy(src, xs_hbm.at[pl.ds(dest_ref[k * n + t0 + base + u], 1)],
                                      out_sem.at[buf]).start(priority=k)
        return carry

    lax.fori_loop(0, tc // SUBLANES, group, 0)

    @pl.when(i > 0)
    def _():
        drain_rows(prev)

    @pl.when(i + 2 < nt)
    def _():
        load(i + 2, prev).start()

    @pl.when(i == nt - 1)
    def _():
        drain_rows(buf)


def _dispatch(dest_flat, h, slots, tc=512):
    n, w = h.shape
    cap = slots.shape[0]
    return pl.pallas_call(
        functools.partial(_dispatch_kernel, n=n, tc=tc),
        grid_spec=pltpu.PrefetchScalarGridSpec(
            num_scalar_prefetch=1,
            grid=(n // tc,),
            in_specs=[pl.BlockSpec(memory_space=pl.ANY), pl.BlockSpec(memory_space=pl.ANY)],
            out_specs=pl.BlockSpec(memory_space=pl.ANY),
            scratch_shapes=[pltpu.VMEM((DISPATCH_BUFS, tc, w), h.dtype),
                            pltpu.SemaphoreType.DMA((DISPATCH_BUFS,)), pltpu.SemaphoreType.DMA((DISPATCH_BUFS,))],
        ),
        out_shape=jax.ShapeDtypeStruct((cap, w), h.dtype),
        input_output_aliases={2: 0},
        compiler_params=_cparams(("arbitrary",)),
        name="moe_dispatch",
    )(dest_flat, h, slots)


def _expert_kernel(blk_e_ref, nact_ref, xs_ref, wg_ref, wu_ref, wd_ref, y_ref, wg_s, wu_s, wd_s):
    i = pl.program_id(0)
    active = i < nact_ref[0]
    changed = (i == 0) | (blk_e_ref[i] != blk_e_ref[jnp.maximum(i - 1, 0)])

    @pl.when(active & changed)
    def _():
        wg_s[...] = wg_ref[...].astype(BF16)
        wu_s[...] = wu_ref[...].astype(BF16)
        wd_s[...] = wd_ref[...].astype(BF16)

    @pl.when(active)
    def _():
        hi, lo = _unpack_bf16_pair(xs_ref[...])
        xb = jnp.concatenate([hi.astype(BF16), lo.astype(BF16)], axis=1)
        g = jnp.dot(xb, wg_s[...], preferred_element_type=F32)
        u = jnp.dot(xb, wu_s[...], preferred_element_type=F32)
        a = (g * jax.nn.sigmoid(g) * u).astype(BF16)
        y = jnp.dot(a, wd_s[...], preferred_element_type=F32)
        y_ref[...] = _pack_bf16_pair(y[:, :D_MODEL // 2], y[:, D_MODEL // 2:])

    @pl.when(jnp.logical_not(active))
    def _():
        y_ref[...] = jnp.zeros(y_ref.shape, y_ref.dtype)


def _experts(blk_e, nact, xs, w_gate, w_up, w_down, layer):
    cap = xs.shape[0]
    n_blk = cap // MOE_BLOCK

    def blk(i, be, na):
        return jnp.minimum(i, na[0] - 1)

    return pl.pallas_call(
        _expert_kernel,
        grid_spec=pltpu.PrefetchScalarGridSpec(
            num_scalar_prefetch=2,
            grid=(n_blk,),
            in_specs=[
                pl.BlockSpec((MOE_BLOCK, D_MODEL // 2), lambda i, be, na: (blk(i, be, na), 0)),
                pl.BlockSpec((None, None, D_MODEL, D_EXPERT), lambda i, be, na: (layer, be[blk(i, be, na)], 0, 0)),
                pl.BlockSpec((None, None, D_MODEL, D_EXPERT), lambda i, be, na: (layer, be[blk(i, be, na)], 0, 0)),
                pl.BlockSpec((None, None, D_EXPERT, D_MODEL), lambda i, be, na: (layer, be[blk(i, be, na)], 0, 0)),
            ],
            out_specs=pl.BlockSpec((MOE_BLOCK, D_MODEL // 2), lambda i, be, na: (i, 0)),
            scratch_shapes=[pltpu.VMEM((D_MODEL, D_EXPERT), BF16), pltpu.VMEM((D_MODEL, D_EXPERT), BF16),
                            pltpu.VMEM((D_EXPERT, D_MODEL), BF16)],
        ),
        out_shape=jax.ShapeDtypeStruct((cap, D_MODEL // 2), PACKED),
        compiler_params=_cparams(("arbitrary",)),
        name="moe_experts",
    )(blk_e, nact, xs, w_gate, w_up, w_down)


def _combine_kernel(dest_ref, x1_ref, route_ref, y_hbm, o_ref, yb, sem, *, n, tc):
    i = pl.program_id(0)
    last = pl.num_programs(0) - 1

    def gather(tile, sl):
        t0 = tile * tc

        def body(j, carry):
            for k in range(2):
                pltpu.make_async_copy(y_hbm.at[pl.ds(dest_ref[k * n + t0 + j], 1)], yb.at[sl, k, pl.ds(j, 1)],
                                      sem.at[sl]).start(priority=k)
            return carry

        lax.fori_loop(0, tc, body, 0, unroll=8)

    def drain(sl):
        for k in range(2):
            pltpu.make_async_copy(y_hbm.at[pl.ds(0, tc)], yb.at[sl, k], sem.at[sl]).wait()

    def combine(sl):
        rows = slice(sl * tc, (sl + 1) * tc)
        route = route_ref[rows, :]
        hi0, lo0 = _unpack_bf16_pair(yb[sl, 0])
        hi1, lo1 = _unpack_bf16_pair(yb[sl, 1])
        half = D_MODEL // 2
        o_ref[rows, :half] = x1_ref[rows, :half] + route[:, 2:3] * hi0 + route[:, 3:4] * hi1
        o_ref[rows, half:] = x1_ref[rows, half:] + route[:, 2:3] * lo0 + route[:, 3:4] * lo1

    @pl.when(i == 0)
    def _():
        gather(0, 0)

    drain(0)
    gather(2 * i + 1, 1)
    combine(0)
    drain(1)

    @pl.when(i < last)
    def _():
        gather(2 * i + 2, 0)

    combine(1)


def _combine(dest_flat, x1, route, y, tc=512):
    n = x1.shape[0]
    return pl.pallas_call(
        functools.partial(_combine_kernel, n=n, tc=tc),
        grid_spec=pltpu.PrefetchScalarGridSpec(
            num_scalar_prefetch=1,
            grid=(n // (2 * tc),),
            in_specs=[pl.BlockSpec((2 * tc, D_MODEL), lambda i, d: (i, 0)),
                      pl.BlockSpec((2 * tc, LANES), lambda i, d: (i, 0)),
                      pl.BlockSpec(memory_space=pl.ANY)],
            out_specs=pl.BlockSpec((2 * tc, D_MODEL), lambda i, d: (i, 0)),
            scratch_shapes=[pltpu.VMEM((2, 2, tc, D_MODEL // 2), PACKED), pltpu.SemaphoreType.DMA((2,))],
        ),
        out_shape=jax.ShapeDtypeStruct((n, D_MODEL), F32),
        compiler_params=_cparams(("arbitrary",)),
        name="moe_combine",
    )(dest_flat, x1, route, y)


def _rope_tables(seq):
    n_freq = HEAD_DIM // 4
    t = np.arange(seq)
    inv_freq = jnp.asarray(ROPE_THETA, F32) ** (-jnp.arange(n_freq, dtype=F32) / n_freq)
    pos = jnp.stack([jnp.asarray(t // GRID_W, F32), jnp.asarray(t % GRID_W, F32)], axis=1)
    ang = pos[:, :, None] * inv_freq
    cos = jnp.broadcast_to(jnp.cos(ang)[:, :, None, :], (seq, 2, 2, n_freq)).reshape(seq, HEAD_DIM)
    sin = jnp.broadcast_to(jnp.sin(ang)[:, :, None, :], (seq, 2, 2, n_freq)).reshape(seq, HEAD_DIM)
    first_half = (np.arange(HEAD_DIM) % (2 * n_freq)) < n_freq
    sa = jnp.where(first_half, -sin, 0.0)
    sb = jnp.where(first_half, 0.0, sin)
    rep = CHUNK // HEAD_DIM
    return jnp.tile(cos, (1, rep)), jnp.tile(sa, (1, rep)), jnp.tile(sb, (1, rep))


def _prep_w_in(w):
    base = w[:, :2048]
    kc0, kc1 = w[:, 2048:2112], w[:, 2112:2176]
    vc0, vc1 = w[:, 2176:2240], w[:, 2240:2304]
    z = jnp.zeros_like(vc0)
    return jnp.concatenate([base, kc0, kc0, kc1, kc1, vc0, z, vc1, z], axis=1).astype(BF16)


def _prep_gain(qk_g):
    qs = SCALE * LOG2E
    ones = jnp.ones((CHUNK,), F32)
    t4 = lambda g: jnp.tile(g, CHUNK // HEAD_DIM)
    return jnp.concatenate([
        t4(qk_g[0, 0]) * qs, t4(qk_g[0, 1]), ones,
        t4(qk_g[1, 0]) * qs, t4(qk_g[1, 1]), ones,
        t4(qk_g[2, 0]) * qs, t4(qk_g[2, 0]) * qs, t4(qk_g[2, 1]), ones,
    ])[None, :]


def _split_hi_lo(w):
    hi = w.astype(BF16)
    lo = (w - hi.astype(F32)).astype(BF16)
    return hi, lo


def _mixer_layer(x2, bsz, seq, tables, norm1_g, w_in, qk_g, rpb, mix_g, w_out, norm2_g, w_rg, b_rg, w_re, b_re):
    n, d = x2.shape
    cos_t, sa_t, sb_t, bd = tables
    proj, pb4, pb16 = _inproj(x2, norm1_g[None, :], _prep_w_in(w_in), _prep_gain(qk_g), bd, cos_t, sa_t, sb_t, seq)
    proj3 = proj.reshape(bsz, seq, PROJ_W)
    oa = _na_attention(proj3, _na_bias_table(rpb)).reshape(n, CHUNK)
    ob_parts = [_dilated_branch(proj3[:, None], CH_QB, 1, 8), _dilated_branch(pb4, 0, 1, 8),
                _dilated_branch(pb16, 0, 4, 2)]
    oc = _gqa_attention(proj3).reshape(n, 2 * CHUNK)

    w_r = jnp.zeros((d, LANES), F32)
    w_r = w_r.at[:, :N_GROUPS].set(w_rg).at[:, N_GROUPS:N_GROUPS + N_EXPERTS].set(w_re)
    b_r = jnp.zeros((1, LANES), F32)
    b_r = b_r.at[0, :N_GROUPS].set(b_rg).at[0, N_GROUPS:N_GROUPS + N_EXPERTS].set(b_re)
    wr_hi, wr_lo = _split_hi_lo(w_r)
    return _outproj(oa, ob_parts, oc, x2, w_out.astype(BF16), mix_g[None, :], norm2_g[None, :], wr_hi, wr_lo, b_r,
                    seq)


def _moe_layer(x1, h, route, counts, slots, w_gate, w_up, w_down, layer):
    n = x1.shape[0]
    n_blk = slots.shape[0] // MOE_BLOCK
    assert n_blk < META_ROWS and slots.shape[0] <= 2 ** 24
    dest, meta = _rank(route, counts)
    dest_flat = dest[:, :2].T.reshape(2 * n)
    xs = _dispatch(dest_flat, h, slots)
    y = _experts(meta[:n_blk, 0], meta[META_ROWS - 1:, 0], xs, w_gate, w_up, w_down, layer)
    return _combine(dest_flat, x1, route, y), xs


def _tables(seq):
    hd = np.arange(CHUNK) // HEAD_DIM
    return (*_rope_tables(seq), jnp.asarray((hd[:, None] == hd[None, :]) * (1.0 / HEAD_DIM), BF16))


@jax.jit
def kernel(x, norm1_g, w_in, qk_norm_g, na_rpb, mix_norm_g, w_out, norm2_g, w_router_group, b_router_group,
           w_router_expert, b_router_expert, w_gate, w_up, w_down):
    bsz, seq, d = x.shape
    tables = _tables(seq)
    x2 = x.reshape(bsz * seq, d)
    slots = jnp.zeros((2 * bsz * seq + N_EXPERTS * MOE_BLOCK, d // 2), PACKED)
    for l in range(DEPTH):
        x1, h, route, counts = _mixer_layer(x2, bsz, seq, tables, norm1_g[l], w_in[l], qk_norm_g[l], na_rpb[l],
                                            mix_norm_g[l], w_out[l], norm2_g[l], w_router_group[l],
                                            b_router_group[l], w_router_expert[l], b_router_expert[l])
        x2, slots = _moe_layer(x1, h, route, counts, slots, w_gate, w_up, w_down, l)
    return x2.reshape(bsz, seq, d)
```
